```python
import jax, jax.numpy as jnp
from jax import lax
import numpy as np

D_MODEL = 2048
BATCH = 4
SEQ = 2048
DEPTH = 4

MIX_A = D_MODEL // 2
RET_HEAD_DIM = 256
RET_HEADS = MIX_A // RET_HEAD_DIM
RET_CHUNK = 128
RET_COLS = 4 * MIX_A
ROPE_BASE = 10000.0
MIX_B = D_MODEL - MIX_A
RWKV_HEAD_DIM = 64
RWKV_HEADS = MIX_B // RWKV_HEAD_DIM
RWKV_LORA_W = 64
RWKV_LORA_A = 64
RWKV_LORA_G = 160
RWKV_LORA_V = 32
RWKV_COLS = 3 * MIX_B + RWKV_LORA_W + RWKV_LORA_A + RWKV_LORA_G
RWKV_SPLITS = (MIX_B, 2 * MIX_B, 3 * MIX_B, 3 * MIX_B + RWKV_LORA_W, 3 * MIX_B + RWKV_LORA_W + RWKV_LORA_A)
RWKV_GN_EPS = 64e-5
EVEN_PROJ = RET_COLS + RWKV_COLS
GLA_HEADS = 4
GLA_DK = D_MODEL // 2
GLA_DV = D_MODEL
GLA_HEAD_DK = GLA_DK // GLA_HEADS
GLA_HEAD_DV = GLA_DV // GLA_HEADS
GLA_LORA = 16
GLA_GATE_NORM = 16.0
GLA_CHUNK = 64
ODD_PROJ = 2 * GLA_DK + 2 * GLA_DV + GLA_LORA
GLA_SPLITS = (GLA_DK, 2 * GLA_DK, 2 * GLA_DK + GLA_DV, 2 * GLA_DK + 2 * GLA_DV)
MOE_GROUPS = 4
MOE_PER_GROUP = 8
N_EXPERTS = MOE_GROUPS * MOE_PER_GROUP
MOE_TOPK = 2
EXPERT_FF = D_MODEL // 4
MOE_BLOCK = 128
N_EVEN = (DEPTH + 1) // 2
N_ODD = DEPTH // 2
DEEPNORM_ALPHA = (2 * DEPTH) ** 0.25
DEEPNORM_BETA = (8 * DEPTH) ** -0.25
LN_EPS = 1e-5

kernel_name = "hybrid_retnet_rwkv7_gla_hmoe_deepnorm"

f32 = jnp.float32


def layer_norm(x, g, b):
    xf = x.astype(f32)
    mu = jnp.mean(xf, -1, keepdims=True)
    var = jnp.mean(jnp.square(xf - mu), -1, keepdims=True)
    return ((xf - mu) * lax.rsqrt(var + LN_EPS) * g + b).astype(x.dtype)


def head_layer_norm(y, g, b, eps):
    H, d = y.shape[-2:]
    yf = y.astype(f32)
    mu = jnp.mean(yf, -1, keepdims=True)
    var = jnp.mean(jnp.square(yf - mu), -1, keepdims=True)
    out = (yf - mu) * lax.rsqrt(var + eps) * g.reshape(H, d) + b.reshape(H, d)
    return out.reshape(*y.shape[:-2], H * d)


def head_rms_norm(y, g, eps):
    H, d = y.shape[-2:]
    yf = y.astype(f32)
    out = yf * lax.rsqrt(jnp.mean(jnp.square(yf), -1, keepdims=True) + eps) * g.reshape(H, d)
    return out.reshape(*y.shape[:-2], H * d)


def rotary(x, positions):
    half = x.shape[-1] // 2
    inv = ROPE_BASE ** (-jnp.arange(half, dtype=f32) / half)
    ang = positions.astype(f32)[..., None] * inv
    cos, sin = jnp.cos(ang)[:, :, None, :], jnp.sin(ang)[:, :, None, :]
    x1, x2 = x[..., :half], x[..., half:]
    return jnp.concatenate([x1 * cos - x2 * sin, x1 * sin + x2 * cos], -1).astype(x.dtype)


def retention(q, k, v):
    B_, S_, H, d = q.shape
    C = RET_CHUNK
    N = S_ // C
    log_g = jnp.log1p(-jnp.exp2(-5.0 - jnp.arange(H, dtype=f32)))
    idx = jnp.arange(C, dtype=f32)
    diff = idx[:, None] - idx[None, :]
    inner_decay = jnp.where(diff >= 0, jnp.exp(log_g[:, None, None] * jnp.maximum(diff, 0.0)), 0.0)
    qc = q.reshape(B_, N, C, H, d)
    kc = k.reshape(B_, N, C, H, d)
    vc = v.reshape(B_, N, C, H, d)
    scores = jnp.einsum('bnihd,bnjhd->bnhij', qc, kc) * inner_decay
    o_inner = jnp.einsum('bnhij,bnjhe->bnihe', scores, vc)
    k_w = jnp.exp(log_g[None, :] * (C - 1 - idx)[:, None])
    kv = jnp.einsum('bnjhd,jh,bnjhe->nbhde', kc, k_w, vc).astype(f32)
    g_chunk = jnp.exp(log_g * C)

    def step(state, kv_n):
        return state * g_chunk[None, :, None, None] + kv_n, state

    _, states = lax.scan(step, jnp.zeros((B_, H, d, d), f32), kv)
    q_w = jnp.exp(log_g[None, :] * (idx + 1.0)[:, None])
    o_cross = jnp.einsum('bnihd,nbhde,ih->bnihe', qc, states, q_w)
    return (o_inner + o_cross).reshape(B_, S_, H, d)


def wkv7_scan(r, w, k, v, a, b):
    B_, _, H, N = r.shape
    xs = tuple(jnp.moveaxis(t.astype(f32), 1, 0) for t in (r, w, k, v, a, b))

    def step(state, inp):
        r_t, w_t, k_t, v_t, a_t, b_t = inp
        sa = jnp.einsum('bhvk,bhk->bhv', state, a_t)
        state = state * w_t[:, :, None, :] + sa[..., None] * b_t[:, :, None, :] + v_t[..., None] * k_t[:, :, None, :]
        return state, jnp.einsum('bhvk,bhk->bhv', state, r_t)

    _, y = lax.scan(step, jnp.zeros((B_, H, N, N), f32), xs)
    return jnp.moveaxis(y, 0, 1)


def rwkv7_time_mix(pb, v_first, vres, mu, w0, w2, a0, a2, g2, k_k, k_a, r_k, lnx_g, lnx_b):
    B_, S_, _ = pb.shape
    prev = jnp.pad(pb[:, :-1], ((0, 0), (1, 0), (0, 0)))
    pb = pb + (prev - pb) * mu
    r, k, v, wd, ad, gd = jnp.split(pb, RWKV_SPLITS, axis=-1)
    w_log = -jax.nn.softplus(-(w0 + jnp.tanh(wd) @ w2).astype(f32)) - 0.5
    decay = jnp.exp(-jnp.exp(w_log))
    a = jax.nn.sigmoid((a0 + ad @ a2).astype(f32))
    gate = jax.nn.sigmoid(gd) @ g2
    if vres is None:
        v_first = v
    else:
        v0, v1, v2 = vres
        v = v + (v_first - v) * jax.nn.sigmoid(v0 + (v @ v1) @ v2)
    heads = lambda t: t.reshape(B_, S_, RWKV_HEADS, RWKV_HEAD_DIM)
    kk = heads((k * k_k).astype(f32))
    kk = kk / jnp.maximum(jnp.sqrt(jnp.sum(jnp.square(kk), -1, keepdims=True)), 1e-12)
    k = k * (1.0 + (a - 1.0) * k_a)
    rh, kh, vh, ah = heads(r), heads(k), heads(v), heads(a)
    y = wkv7_scan(rh, heads(decay), kh, vh, -kk, kk * ah)
    y = head_layer_norm(y, lnx_g, lnx_b, RWKV_GN_EPS)
    bonus = (jnp.sum(rh * kh * r_k, -1, keepdims=True) * vh).reshape(B_, S_, MIX_B)
    return (y + bonus) * gate, v_first


def even_mixer(h, positions, v_first, vres, win, wo, ret_gn_g, ret_gn_b,
               mu, w0, w2, a0, a2, g2, k_k, k_a, r_k, lnx_g, lnx_b):
    B_, S_, _ = h.shape
    p = h @ win
    q, k, v, g = jnp.split(p[..., :RET_COLS], 4, axis=-1)
    heads = lambda t: t.reshape(B_, S_, RET_HEADS, RET_HEAD_DIM)
    q = rotary(heads(q), positions)
    k = rotary(heads(k), positions) * RET_HEAD_DIM ** -0.5
    ya = retention(q, k, heads(v))
    out_a = jax.nn.silu(g.astype(f32)) * head_layer_norm(ya, ret_gn_g, ret_gn_b, LN_EPS)
    out_b, v_first = rwkv7_time_mix(p[..., RET_COLS:], v_first, vres, mu, w0, w2, a0, a2, g2,
                                    k_k, k_a, r_k, lnx_g, lnx_b)
    y = jnp.concatenate([out_a, out_b.astype(f32)], -1).astype(h.dtype) @ wo
    return y, v_first


def gla_chunked(q, k, v, log_a):
    B_, S_, H, dk = q.shape
    dv = v.shape[-1]
    C = GLA_CHUNK
    N = S_ // C
    to_chunks = lambda t: t.astype(f32).reshape(B_, N, C, H, t.shape[-1]).transpose(1, 0, 3, 2, 4)
    causal = jnp.tril(jnp.ones((C, C), bool))

    def step(state, inp):
        qc, kc, vc, lac = inp
        bcum = jnp.cumsum(lac, axis=2)
        o_inter = jnp.einsum('bhid,bhde->bhie', qc * jnp.exp(bcum), state)
        rel = jnp.where(causal[None, None, :, :, None], bcum[:, :, :, None, :] - bcum[:, :, None, :, :], -jnp.inf)
        attn = jnp.einsum('bhid,bhjd,bhijd->bhij', qc, kc, jnp.exp(rel))
        o_intra = jnp.einsum('bhij,bhje->bhie', attn, vc)
        b_last = bcum[:, :, -1:, :]
        state = state * jnp.exp(b_last)[:, :, 0, :, None] + jnp.einsum('bhjd,bhje->bhde', kc * jnp.exp(b_last - bcum), vc)
        return state, o_inter + o_intra

    _, o = lax.scan(step, jnp.zeros((B_, H, dk, dv), f32),
                    (to_chunks(q), to_chunks(k), to_chunks(v), to_chunks(log_a)))
    return o.transpose(1, 0, 3, 2, 4).reshape(B_, S_, H, dv)


def gla_mixer(h, win, wo, a2, ab, norm_g):
    B_, S_, _ = h.shape
    q, k, v, r, ad = jnp.split(h @ win, GLA_SPLITS, axis=-1)
    log_a = jax.nn.log_sigmoid((ad @ a2 + ab).astype(f32)) / GLA_GATE_NORM
    hk = lambda t: t.reshape(B_, S_, GLA_HEADS, GLA_HEAD_DK)
    o = gla_chunked(hk(q) * GLA_HEAD_DK ** -0.5, hk(k), v.reshape(B_, S_, GLA_HEADS, GLA_HEAD_DV), hk(log_a))
    o = head_rms_norm(o, norm_g, LN_EPS)
    return (jax.nn.silu(r.astype(f32)) * o).astype(h.dtype) @ wo


def hier_moe(h, wg, bg, we, be, w1, w3, w2):
    B_, S_, D = h.shape
    T = B_ * S_
    xt = h.reshape(T, D)
    g_prob = jax.nn.softmax((xt @ wg + bg).astype(f32), -1)
    p_top, g_top = lax.top_k(g_prob, 1)
    e_logits = (xt @ we + be).astype(f32).reshape(T, MOE_GROUPS, MOE_PER_GROUP)
    e_sel = jnp.take_along_axis(e_logits, g_top[:, :, None], axis=1)[:, 0]
    e_val, e_top = lax.top_k(e_sel, MOE_TOPK)
    gate = p_top * jax.nn.softmax(e_val, -1)
    expert = g_top * MOE_PER_GROUP + e_top
    A = T * MOE_TOPK
    e_flat = expert.reshape(A)
    tok_flat = jnp.repeat(jnp.arange(T, dtype=jnp.int32), MOE_TOPK)
    order = jnp.argsort(e_flat)
    e_s, tok_s, w_s = e_flat[order], tok_flat[order], gate.reshape(A)[order]
    counts = jnp.bincount(e_flat, length=N_EXPERTS)
    padded = (counts + MOE_BLOCK - 1) // MOE_BLOCK * MOE_BLOCK
    pad_end = jnp.cumsum(padded)
    pad_start = pad_end - padded
    start = jnp.cumsum(counts) - counts
    slot = pad_start[e_s] + jnp.arange(A, dtype=jnp.int32) - start[e_s]
    NB = -(-A // MOE_BLOCK) + N_EXPERTS
    buf_tok = jnp.full((NB * MOE_BLOCK,), T, jnp.int32).at[slot].set(tok_s)
    blk_expert = jnp.minimum(jnp.searchsorted(pad_end, jnp.arange(NB, dtype=jnp.int32) * MOE_BLOCK, side='right'),
                             N_EXPERTS - 1)
    x_pad = jnp.concatenate([xt, jnp.zeros((1, D), xt.dtype)], 0)

    def expert_block(args):
        toks, e = args
        xb = x_pad[toks]
        return (jax.nn.silu(xb @ w1[e]) * (xb @ w3[e])) @ w2[e]

    y_buf = lax.map(expert_block, (buf_tok.reshape(NB, MOE_BLOCK), blk_expert)).reshape(NB * MOE_BLOCK, D)
    y = jax.ops.segment_sum(y_buf[slot] * w_s[:, None].astype(y_buf.dtype), tok_s, num_segments=T)
    return y.reshape(B_, S_, D)


def setup_inputs(seed: int = 0) -> dict:
    key = jax.random.key(seed)
    ks = iter(jax.random.split(key, 48))

    def nrm(shape, scale):
        return scale * jax.random.normal(next(ks), shape, f32)

    def gain(shape):
        return 1.0 + nrm(shape, 0.02)

    D = D_MODEL
    x = nrm((BATCH, SEQ, D), 1.0)
    c = nrm((BATCH, D), 1.0)
    positions = jnp.arange(SEQ, dtype=jnp.int32)[None, :] + jax.random.randint(next(ks), (BATCH, 1), 0, 4096, jnp.int32)
    ada_w = nrm((DEPTH, D, 6 * D), 0.1 * D ** -0.5)
    ada_b = nrm((DEPTH, 6 * D), 0.01)
    ln_g = gain((DEPTH, 2, D))
    ln_b = nrm((DEPTH, 2, D), 0.02)
    ev_win = nrm((N_EVEN, D, EVEN_PROJ), D ** -0.5)
    ev_wo = nrm((N_EVEN, D, D), DEEPNORM_BETA * D ** -0.5)
    ret_gn_g = gain((N_EVEN, MIX_A))
    ret_gn_b = nrm((N_EVEN, MIX_A), 0.02)
    rw_mu = jax.random.uniform(next(ks), (N_EVEN, RWKV_COLS), f32)
    ratio = jnp.arange(MIX_B, dtype=f32) / (MIX_B - 1)
    expo = 0.85 + (jnp.arange(N_EVEN, dtype=f32) / max(N_EVEN - 1, 1)) ** 0.5
    rw_w0 = -6.5 + 5.0 * ratio[None, :] ** expo[:, None] + nrm((N_EVEN, MIX_B), 0.1)
    rw_w2 = nrm((N_EVEN, RWKV_LORA_W, MIX_B), 0.5 * RWKV_LORA_W ** -0.5)
    rw_a0 = nrm((N_EVEN, MIX_B), 0.1)
    rw_a2 = nrm((N_EVEN, RWKV_LORA_A, MIX_B), RWKV_LORA_A ** -0.5)
    rw_g2 = nrm((N_EVEN, RWKV_LORA_G, MIX_B), RWKV_LORA_G ** -0.5)
    rw_kk = 0.85 + nrm((N_EVEN, MIX_B), 0.05)
    rw_ka = 1.0 + nrm((N_EVEN, MIX_B), 0.05)
    rw_rk = nrm((N_EVEN, RWKV_HEADS, RWKV_HEAD_DIM), 0.1)
    rw_lnx_g = gain((N_EVEN, MIX_B))
    rw_lnx_b = nrm((N_EVEN, MIX_B), 0.02)
    rw_v0 = nrm((N_EVEN - 1, MIX_B), 0.1)
    rw_v1 = nrm((N_EVEN - 1, MIX_B, RWKV_LORA_V), MIX_B ** -0.5)
    rw_v2 = nrm((N_EVEN - 1, RWKV_LORA_V, MIX_B), RWKV_LORA_V ** -0.5)
    od_win = nrm((N_ODD, D, ODD_PROJ), D ** -0.5)
    od_wo = nrm((N_ODD, GLA_DV, D), DEEPNORM_BETA * GLA_DV ** -0.5)
    gla_a2 = nrm((N_ODD, GLA_LORA, GLA_DK), GLA_LORA ** -0.5)
    gla_ab = nrm((N_ODD, GLA_DK), 0.5)
    gla_norm_g = gain((N_ODD, GLA_DV))
    moe_wg = nrm((DEPTH, D, MOE_GROUPS), D ** -0.5)
    moe_bg = nrm((DEPTH, MOE_GROUPS), 0.01)
    moe_we = nrm((DEPTH, D, N_EXPERTS), D ** -0.5)
    moe_be = nrm((DEPTH, N_EXPERTS), 0.01)
    moe_w1 = nrm((DEPTH, N_EXPERTS, D, EXPERT_FF), D ** -0.5)
    moe_w3 = nrm((DEPTH, N_EXPERTS, D, EXPERT_FF), D ** -0.5)
    moe_w2 = nrm((DEPTH, N_EXPERTS, EXPERT_FF, D), DEEPNORM_BETA * EXPERT_FF ** -0.5)
    return {"x": x, "c": c, "positions": positions, "ada_w": ada_w, "ada_b": ada_b,
            "ln_g": ln_g, "ln_b": ln_b, "ev_win": ev_win, "ev_wo": ev_wo,
            "ret_gn_g": ret_gn_g, "ret_gn_b": ret_gn_b, "rw_mu": rw_mu, "rw_w0": rw_w0,
            "rw_w2": rw_w2, "rw_a0": rw_a0, "rw_a2": rw_a2, "rw_g2": rw_g2, "rw_kk": rw_kk,
            "rw_ka": rw_ka, "rw_rk": rw_rk, "rw_lnx_g": rw_lnx_g, "rw_lnx_b": rw_lnx_b,
            "rw_v0": rw_v0, "rw_v1": rw_v1, "rw_v2": rw_v2, "od_win": od_win, "od_wo": od_wo,
            "gla_a2": gla_a2, "gla_ab": gla_ab, "gla_norm_g": gla_norm_g,
            "moe_wg": moe_wg, "moe_bg": moe_bg, "moe_we": moe_we, "moe_be": moe_be,
            "moe_w1": moe_w1, "moe_w3": moe_w3, "moe_w2": moe_w2}


def reference(x, c, positions, ada_w, ada_b, ln_g, ln_b, ev_win, ev_wo, ret_gn_g, ret_gn_b,
              rw_mu, rw_w0, rw_w2, rw_a0, rw_a2, rw_g2, rw_kk, rw_ka, rw_rk, rw_lnx_g, rw_lnx_b,
              rw_v0, rw_v1, rw_v2, od_win, od_wo, gla_a2, gla_ab, gla_norm_g,
              moe_wg, moe_bg, moe_we, moe_be, moe_w1, moe_w3, moe_w2):
    c_act = jax.nn.silu(c)
    v_first = None
    for layer in range(DEPTH):
        mod = (c_act @ ada_w[layer] + ada_b[layer])[:, None, :]
        sh_m, sc_m, gt_m, sh_f, sc_f, gt_f = jnp.split(mod, 6, axis=-1)
        h = x * (1.0 + sc_m) + sh_m
        j = layer // 2
        if layer % 2 == 0:
            vres = None if j == 0 else (rw_v0[j - 1], rw_v1[j - 1], rw_v2[j - 1])
            y, v_first = even_mixer(h, positions, v_first, vres, ev_win[j], ev_wo[j], ret_gn_g[j], ret_gn_b[j],
                                    rw_mu[j], rw_w0[j], rw_w2[j], rw_a0[j], rw_a2[j], rw_g2[j],
                                    rw_kk[j], rw_ka[j], rw_rk[j], rw_lnx_g[j], rw_lnx_b[j])
        else:
            y = gla_mixer(h, od_win[j], od_wo[j], gla_a2[j], gla_ab[j], gla_norm_g[j])
        x = layer_norm(DEEPNORM_ALPHA * x + (1.0 + gt_m) * y, ln_g[layer, 0], ln_b[layer, 0])
        h = x * (1.0 + sc_f) + sh_f
        y = hier_moe(h, moe_wg[layer], moe_bg[layer], moe_we[layer], moe_be[layer],
                     moe_w1[layer], moe_w3[layer], moe_w2[layer])
        x = layer_norm(DEEPNORM_ALPHA * x + (1.0 + gt_f) * y, ln_g[layer, 1], ln_b[layer, 1])
    return x
```

```python
import functools

import jax
import jax.numpy as jnp
import numpy as np
from jax import lax
from jax.experimental import pallas as pl
from jax.experimental.pallas import tpu as pltpu

D_MODEL = 2048
DEPTH = 4
MIX_A = D_MODEL // 2
RET_HEAD_DIM = 256
RET_HEADS = MIX_A // RET_HEAD_DIM
RET_CHUNK = 128
RET_COLS = 4 * MIX_A
ROPE_BASE = 10000.0
MIX_B = D_MODEL - MIX_A
RWKV_HEAD_DIM = 64
RWKV_HEADS = MIX_B // RWKV_HEAD_DIM
RWKV_LORA_W = 64
RWKV_LORA_A = 64
RWKV_LORA_G = 160
RWKV_COLS = 3 * MIX_B + RWKV_LORA_W + RWKV_LORA_A + RWKV_LORA_G
RWKV_SPLITS = (MIX_B, 2 * MIX_B, 3 * MIX_B, 3 * MIX_B + RWKV_LORA_W, 3 * MIX_B + RWKV_LORA_W + RWKV_LORA_A)
RWKV_GN_EPS = 64e-5
EVEN_PROJ = RET_COLS + RWKV_COLS
GLA_HEADS = 4
GLA_DK = D_MODEL // 2
GLA_DV = D_MODEL
GLA_HEAD_DK = GLA_DK // GLA_HEADS
GLA_HEAD_DV = GLA_DV // GLA_HEADS
GLA_LORA = 16
GLA_GATE_NORM = 16.0
GLA_CHUNK = 64
ODD_PROJ = 2 * GLA_DK + 2 * GLA_DV + GLA_LORA
GLA_SPLITS = (GLA_DK, 2 * GLA_DK, 2 * GLA_DK + GLA_DV, 2 * GLA_DK + 2 * GLA_DV)
MOE_GROUPS = 4
MOE_PER_GROUP = 8
N_EXPERTS = MOE_GROUPS * MOE_PER_GROUP
MOE_TOPK = 2
EXPERT_FF = D_MODEL // 4
MOE_BLOCK = 128
DEEPNORM_ALPHA = (2 * DEPTH) ** 0.25
LN_EPS = 1e-5

f32 = jnp.float32
bf16 = jnp.bfloat16

VMEM_LIMIT_BYTES = 48 * 1024 * 1024


def _matmul_kernel(x_ref, w_ref, o_ref):
    o_ref[...] = jnp.dot(x_ref[...], w_ref[...], preferred_element_type=f32)


def matmul(x, w, tm=1024, tn=512):
    M, K = x.shape
    N = w.shape[1]
    n_pad = -N % tn
    wb = jnp.pad(w.astype(bf16), ((0, 0), (0, n_pad)))
    Np = N + n_pad
    tm = min(tm, M)
    out = pl.pallas_call(
        _matmul_kernel,
        grid=(M // tm, Np // tn),
        in_specs=[pl.BlockSpec((tm, K), lambda i, j: (i, 0)),
                  pl.BlockSpec((K, tn), lambda i, j: (0, j))],
        out_specs=pl.BlockSpec((tm, tn), lambda i, j: (i, j)),
        out_shape=jax.ShapeDtypeStruct((M, Np), f32),
        compiler_params=pltpu.CompilerParams(
            dimension_semantics=("parallel", "parallel"), vmem_limit_bytes=VMEM_LIMIT_BYTES),
        name="dense_matmul",
    )(x.astype(bf16), wb)
    return out[:, :N] if n_pad else out


WKV_CHUNK = 64
WKV_PAIR = 2 * RWKV_HEAD_DIM
WKV_PAIRS = MIX_B // WKV_PAIR


def _dotb(a, b):
    return jnp.dot(a.astype(bf16), b.astype(bf16), preferred_element_type=f32)


def _wkv_kernel(r_ref, lw_ref, k_ref, v_ref, a_ref, b_ref, y_ref, h_ref):
    C = WKV_CHUNK
    P2 = WKV_PAIR

    @pl.when(pl.program_id(1) == 0)
    def _():
        h_ref[...] = jnp.zeros_like(h_ref)

    row = lax.broadcasted_iota(jnp.int32, (P2, P2), 0)
    col = lax.broadcasted_iota(jnp.int32, (P2, P2), 1)
    same_head = (row >= C) == (col >= RWKV_HEAD_DIM)
    t_idx = jnp.bitwise_and(row, C - 1)
    s_idx = jnp.bitwise_and(col, C - 1)
    strict_lower = t_idx > s_idx
    incl_lower = t_idx >= s_idx
    eye = row == col
    tri = (lax.broadcasted_iota(jnp.int32, (C, C), 0) >= lax.broadcasted_iota(jnp.int32, (C, C), 1)).astype(f32)
    zeros_pp = jnp.zeros((P2, P2), f32)

    def bd(x):
        return jnp.where(same_head, jnp.concatenate([x, x], axis=0), 0.0)

    for p in range(WKV_PAIRS):
        sl = slice(p * P2, (p + 1) * P2)
        r = r_ref[0, :, sl]
        lw = lw_ref[0, :, sl]
        k = k_ref[0, :, sl]
        v = v_ref[0, :, sl]
        a = a_ref[0, :, sl]
        b = b_ref[0, :, sl]
        cum = jnp.dot(tri, lw, preferred_element_type=f32, precision=lax.Precision.HIGHEST)
        tot = cum[C - 1:C, :]
        w_incl = jnp.exp(cum)
        w_excl = jnp.exp(cum - lw)
        w_inv = jnp.exp(-cum)
        w_rest = jnp.exp(tot - cum)
        at = bd(a * w_excl)
        rt = bd(r * w_incl)
        bt = bd(b * w_inv)
        kt = bd(k * w_inv)
        bh = bd(b * w_rest)
        kh = bd(k * w_rest)
        vb = bd(v)
        scores = lax.dot_general(
            jnp.concatenate([at, rt], axis=0).astype(bf16), jnp.concatenate([bt, kt], axis=0).astype(bf16),
            (((1,), (1,)), ((), ())), preferred_element_type=f32)
        a_ab = jnp.where(strict_lower, scores[:P2, :P2], 0.0)
        a_ak = jnp.where(strict_lower, scores[:P2, P2:], 0.0)
        a_rb = jnp.where(incl_lower, scores[P2:, :P2], 0.0)
        a_rk = jnp.where(incl_lower, scores[P2:, P2:], 0.0)
        x = jnp.concatenate([at, _dotb(a_ak, vb)], axis=1)
        pw = a_ab
        x = x + _dotb(pw, x)
        for _ in range(int(np.log2(C)) - 1):
            pw = _dotb(pw, pw)
            x = x + _dotb(pw, x)
        rhs = jnp.concatenate([x, jnp.concatenate([zeros_pp, vb], axis=1)], axis=0)
        gz = _dotb(jnp.concatenate([a_rb, a_rk], axis=1), rhs)
        g = rt + gz[:, :P2]
        z = gz[:, P2:]
        mn = _dotb(jnp.concatenate([bh, kh], axis=0).T, rhs)
        m = mn[:, :P2] + jnp.where(eye, jnp.broadcast_to(jnp.exp(tot), (P2, P2)), 0.0)
        n = mn[:, P2:]
        yh = _dotb(jnp.concatenate([g, m], axis=0), h_ref[p])
        y = yh[:P2] + z
        h_ref[p] = yh[P2:] + n
        y_ref[0, :, sl] = y[:C] + y[C:]


def wkv7(r, lw, k, v, a, b):
    B, S, _ = r.shape
    spec = pl.BlockSpec((1, WKV_CHUNK, MIX_B), lambda bi, ci: (bi, ci, 0))
    return pl.pallas_call(
        _wkv_kernel,
        grid=(B, S // WKV_CHUNK),
        in_specs=[spec] * 6,
        out_specs=spec,
        out_shape=jax.ShapeDtypeStruct((B, S, MIX_B), f32),
        scratch_shapes=[pltpu.VMEM((WKV_PAIRS, WKV_PAIR, WKV_PAIR), f32)],
        compiler_params=pltpu.CompilerParams(
            dimension_semantics=("parallel", "arbitrary"), vmem_limit_bytes=VMEM_LIMIT_BYTES),
        name="wkv7_chunked",
    )(r, lw, k, v, a, b)


def layer_norm(x, g, b):
    mu = jnp.mean(x, -1, keepdims=True)
    var = jnp.mean(jnp.square(x - mu), -1, keepdims=True)
    return (x - mu) * lax.rsqrt(var + LN_EPS) * g + b


def head_layer_norm(y, g, b, eps):
    H, d = y.shape[-2:]
    mu = jnp.mean(y, -1, keepdims=True)
    var = jnp.mean(jnp.square(y - mu), -1, keepdims=True)
    out = (y - mu) * lax.rsqrt(var + eps) * g.reshape(H, d) + b.reshape(H, d)
    return out.reshape(*y.shape[:-2], H * d)


def head_rms_norm(y, g, eps):
    H, d = y.shape[-2:]
    out = y * lax.rsqrt(jnp.mean(jnp.square(y), -1, keepdims=True) + eps) * g.reshape(H, d)
    return out.reshape(*y.shape[:-2], H * d)


def rotary(x, positions):
    half = x.shape[-1] // 2
    inv = ROPE_BASE ** (-jnp.arange(half, dtype=f32) / half)
    ang = positions.astype(f32)[..., None] * inv
    cos, sin = jnp.cos(ang)[:, :, None, :], jnp.sin(ang)[:, :, None, :]
    x1, x2 = x[..., :half], x[..., half:]
    return jnp.concatenate([x1 * cos - x2 * sin, x1 * sin + x2 * cos], -1)


def retention(q, k, v):
    B_, S_, H, d = q.shape
    C = RET_CHUNK
    N = S_ // C
    log_g = jnp.log1p(-jnp.exp2(-5.0 - jnp.arange(H, dtype=f32)))
    idx = jnp.arange(C, dtype=f32)
    diff = idx[:, None] - idx[None, :]
    inner_decay = jnp.where(diff >= 0, jnp.exp(log_g[:, None, None] * jnp.maximum(diff, 0.0)), 0.0)
    qc = q.reshape(B_, N, C, H, d)
    kc = k.reshape(B_, N, C, H, d)
    vc = v.reshape(B_, N, C, H, d)
    scores = jnp.einsum('bnihd,bnjhd->bnhij', qc, kc) * inner_decay
    o_inner = jnp.einsum('bnhij,bnjhe->bnihe', scores, vc)
    k_w = jnp.exp(log_g[None, :] * (C - 1 - idx)[:, None])
    kv = jnp.einsum('bnjhd,jh,bnjhe->nbhde', kc, k_w, vc).astype(f32)
    g_chunk = jnp.exp(log_g * C)

    def step(state, kv_n):
        return state * g_chunk[None, :, None, None] + kv_n, state

    _, states = lax.scan(step, jnp.zeros((B_, H, d, d), f32), kv)
    q_w = jnp.exp(log_g[None, :] * (idx + 1.0)[:, None])
    o_cross = jnp.einsum('bnihd,nbhde,ih->bnihe', qc, states, q_w)
    return (o_inner + o_cross).reshape(B_, S_, H, d)


def rwkv7_time_mix(pb, v_first, vres, mu, w0, w2, a0, a2, g2, k_k, k_a, r_k, lnx_g, lnx_b):
    B_, S_, _ = pb.shape
    prev = jnp.pad(pb[:, :-1], ((0, 0), (1, 0), (0, 0)))
    pb = pb + (prev - pb) * mu
    r, k, v, wd, ad, gd = jnp.split(pb, RWKV_SPLITS, axis=-1)
    w_log = -jax.nn.softplus(-(w0 + jnp.tanh(wd) @ w2)) - 0.5
    log_decay = -jnp.exp(w_log)
    a = jax.nn.sigmoid(a0 + ad @ a2)
    gate = jax.nn.sigmoid(gd) @ g2
    if vres is None:
        v_first = v
    else:
        v0, v1, v2 = vres
        v = v + (v_first - v) * jax.nn.sigmoid(v0 + (v @ v1) @ v2)
    heads = lambda t: t.reshape(B_, S_, RWKV_HEADS, RWKV_HEAD_DIM)
    kk = heads(k * k_k)
    kk = kk / jnp.maximum(jnp.sqrt(jnp.sum(jnp.square(kk), -1, keepdims=True)), 1e-12)
    kk = kk.reshape(B_, S_, MIX_B)
    k = k * (1.0 + (a - 1.0) * k_a)
    y = wkv7(r, log_decay, k, v, -kk, kk * a)
    rh, kh, vh = heads(r), heads(k), heads(v)
    y = head_layer_norm(heads(y), lnx_g, lnx_b, RWKV_GN_EPS)
    bonus = (jnp.sum(rh * kh * r_k, -1, keepdims=True) * vh).reshape(B_, S_, MIX_B)
    return (y + bonus) * gate, v_first


def even_mixer(h, positions, v_first, vres, win, wo, ret_gn_g, ret_gn_b,
               mu, w0, w2, a0, a2, g2, k_k, k_a, r_k, lnx_g, lnx_b):
    B_, S_, D = h.shape
    p = matmul(h.reshape(B_ * S_, D), win).reshape(B_, S_, EVEN_PROJ)
    q, k, v, g = jnp.split(p[..., :RET_COLS], 4, axis=-1)
    heads = lambda t: t.reshape(B_, S_, RET_HEADS, RET_HEAD_DIM)
    q = rotary(heads(q), positions)
    k = rotary(heads(k), positions) * RET_HEAD_DIM ** -0.5
    ya = retention(q, k, heads(v))
    out_a = jax.nn.silu(g) * head_layer_norm(ya, ret_gn_g, ret_gn_b, LN_EPS)
    out_b, v_first = rwkv7_time_mix(p[..., RET_COLS:], v_first, vres, mu, w0, w2, a0, a2, g2,
                                    k_k, k_a, r_k, lnx_g, lnx_b)
    y = matmul(jnp.concatenate([out_a, out_b], -1).reshape(B_ * S_, D), wo).reshape(B_, S_, D)
    return y, v_first


def gla_chunked(q, k, v, log_a):
    B_, S_, H, dk = q.shape
    dv = v.shape[-1]
    C = GLA_CHUNK
    N = S_ // C
    to_chunks = lambda t: t.reshape(B_, N, C, H, t.shape[-1]).transpose(1, 0, 3, 2, 4)
    causal = jnp.tril(jnp.ones((C, C), bool))

    def step(state, inp):
        qc, kc, vc, lac = inp
        bcum = jnp.cumsum(lac, axis=2)
        o_inter = jnp.einsum('bhid,bhde->bhie', qc * jnp.exp(bcum), state)
        rel = jnp.where(causal[None, None, :, :, None], bcum[:, :, :, None, :] - bcum[:, :, None, :, :], -jnp.inf)
        attn = jnp.einsum('bhid,bhjd,bhijd->bhij', qc, kc, jnp.exp(rel))
        o_intra = jnp.einsum('bhij,bhje->bhie', attn, vc)
        b_last = bcum[:, :, -1:, :]
        state = state * jnp.exp(b_last)[:, :, 0, :, None] + jnp.einsum('bhjd,bhje->bhde', kc * jnp.exp(b_last - bcum), vc)
        return state, o_inter + o_intra

    _, o = lax.scan(step, jnp.zeros((B_, H, dk, dv), f32),
                    (to_chunks(q), to_chunks(k), to_chunks(v), to_chunks(log_a)))
    return o.transpose(1, 0, 3, 2, 4).reshape(B_, S_, H, dv)


def gla_mixer(h, win, wo, a2, ab, norm_g):
    B_, S_, D = h.shape
    p = matmul(h.reshape(B_ * S_, D), win).reshape(B_, S_, ODD_PROJ)
    q, k, v, r, ad = jnp.split(p, GLA_SPLITS, axis=-1)
    log_a = jax.nn.log_sigmoid(ad @ a2 + ab) / GLA_GATE_NORM
    hk = lambda t: t.reshape(B_, S_, GLA_HEADS, GLA_HEAD_DK)
    o = gla_chunked(hk(q) * GLA_HEAD_DK ** -0.5, hk(k), v.reshape(B_, S_, GLA_HEADS, GLA_HEAD_DV), hk(log_a))
    o = head_rms_norm(o, norm_g, LN_EPS)
    return matmul((jax.nn.silu(r) * o).reshape(B_ * S_, GLA_DV), wo).reshape(B_, S_, D)


def hier_moe(h, wg, bg, we, be, w1, w3, w2):
    B_, S_, D = h.shape
    T = B_ * S_
    xt = h.reshape(T, D)
    g_prob = jax.nn.softmax(xt @ wg + bg, -1)
    p_top, g_top = lax.top_k(g_prob, 1)
    e_logits = (xt @ we + be).reshape(T, MOE_GROUPS, MOE_PER_GROUP)
    e_sel = jnp.take_along_axis(e_logits, g_top[:, :, None], axis=1)[:, 0]
    e_val, e_top = lax.top_k(e_sel, MOE_TOPK)
    gate = p_top * jax.nn.softmax(e_val, -1)
    expert = g_top * MOE_PER_GROUP + e_top
    A = T * MOE_TOPK
    e_flat = expert.reshape(A)
    tok_flat = jnp.repeat(jnp.arange(T, dtype=jnp.int32), MOE_TOPK)
    order = jnp.argsort(e_flat)
    e_s, tok_s, w_s = e_flat[order], tok_flat[order], gate.reshape(A)[order]
    counts = jnp.bincount(e_flat, length=N_EXPERTS)
    padded = (counts + MOE_BLOCK - 1) // MOE_BLOCK * MOE_BLOCK
    pad_end = jnp.cumsum(padded)
    pad_start = pad_end - padded
    start = jnp.cumsum(counts) - counts
    slot = pad_start[e_s] + jnp.arange(A, dtype=jnp.int32) - start[e_s]
    NB = -(-A // MOE_BLOCK) + N_EXPERTS
    buf_tok = jnp.full((NB * MOE_BLOCK,), T, jnp.int32).at[slot].set(tok_s)
    blk_expert = jnp.minimum(jnp.searchsorted(pad_end, jnp.arange(NB, dtype=jnp.int32) * MOE_BLOCK, side='right'),
                             N_EXPERTS - 1)
    x_pad = jnp.concatenate([xt, jnp.zeros((1, D), xt.dtype)], 0)

    def expert_block(args):
        toks, e = args
        xb = x_pad[toks]
        return (jax.nn.silu(xb @ w1[e]) * (xb @ w3[e])) @ w2[e]

    y_buf = lax.map(expert_block, (buf_tok.reshape(NB, MOE_BLOCK), blk_expert)).reshape(NB * MOE_BLOCK, D)
    y = jax.ops.segment_sum(y_buf[slot] * w_s[:, None], tok_s, num_segments=T)
    return y.reshape(B_, S_, D)


def kernel(x, c, positions, ada_w, ada_b, ln_g, ln_b, ev_win, ev_wo, ret_gn_g, ret_gn_b, rw_mu, rw_w0, rw_w2, rw_a0, rw_a2, rw_g2, rw_kk, rw_ka, rw_rk, rw_lnx_g, rw_lnx_b, rw_v0, rw_v1, rw_v2, od_win, od_wo, gla_a2, gla_ab, gla_norm_g, moe_wg, moe_bg, moe_we, moe_be, moe_w1, moe_w3, moe_w2):
    c_act = jax.nn.silu(c)
    v_first = None
    for layer in range(DEPTH):
        mod = (c_act @ ada_w[layer] + ada_b[layer])[:, None, :]
        sh_m, sc_m, gt_m, sh_f, sc_f, gt_f = jnp.split(mod, 6, axis=-1)
        h = x * (1.0 + sc_m) + sh_m
        j = layer // 2
        if layer % 2 == 0:
            vres = None if j == 0 else (rw_v0[j - 1], rw_v1[j - 1], rw_v2[j - 1])
            y, v_first = even_mixer(h, positions, v_first, vres, ev_win[j], ev_wo[j], ret_gn_g[j], ret_gn_b[j],
                                    rw_mu[j], rw_w0[j], rw_w2[j], rw_a0[j], rw_a2[j], rw_g2[j],
                                    rw_kk[j], rw_ka[j], rw_rk[j], rw_lnx_g[j], rw_lnx_b[j])
        else:
            y = gla_mixer(h, od_win[j], od_wo[j], gla_a2[j], gla_ab[j], gla_norm_g[j])
        x = layer_norm(DEEPNORM_ALPHA * x + (1.0 + gt_m) * y, ln_g[layer, 0], ln_b[layer, 0])
        h = x * (1.0 + sc_f) + sh_f
        y = hier_moe(h, moe_wg[layer], moe_bg[layer], moe_we[layer], moe_be[layer],
                     moe_w1[layer], moe_w3[layer], moe_w2[layer])
        x = layer_norm(DEEPNORM_ALPHA * x + (1.0 + gt_f) * y, ln_g[layer, 1], ln_b[layer, 1])
    return x
```

```python
import functools

import jax
import jax.numpy as jnp
import numpy as np
from jax import lax
from jax.experimental import pallas as pl
from jax.experimental.pallas import tpu as pltpu

D_MODEL = 2048
DEPTH = 4
MIX_A = D_MODEL // 2
RET_HEAD_DIM = 256
RET_HEADS = MIX_A // RET_HEAD_DIM
RET_CHUNK = 128
RET_COLS = 4 * MIX_A
ROPE_BASE = 10000.0
MIX_B = D_MODEL - MIX_A
RWKV_HEAD_DIM = 64
RWKV_HEADS = MIX_B // RWKV_HEAD_DIM
RWKV_LORA_W = 64
RWKV_LORA_A = 64
RWKV_LORA_G = 160
RWKV_COLS = 3 * MIX_B + RWKV_LORA_W + RWKV_LORA_A + RWKV_LORA_G
RWKV_SPLITS = (MIX_B, 2 * MIX_B, 3 * MIX_B, 3 * MIX_B + RWKV_LORA_W, 3 * MIX_B + RWKV_LORA_W + RWKV_LORA_A)
RWKV_GN_EPS = 64e-5
EVEN_PROJ = RET_COLS + RWKV_COLS
EVEN_PROJ_PAD = -(-EVEN_PROJ // 512) * 512
GLA_HEADS = 4
GLA_DK = D_MODEL // 2
GLA_DV = D_MODEL
GLA_HEAD_DK = GLA_DK // GLA_HEADS
GLA_HEAD_DV = GLA_DV // GLA_HEADS
GLA_LORA = 16
GLA_GATE_NORM = 16.0
GLA_CHUNK = 64
ODD_PROJ = 2 * GLA_DK + 2 * GLA_DV + GLA_LORA
GLA_SPLITS = (GLA_DK, 2 * GLA_DK, 2 * GLA_DK + GLA_DV, 2 * GLA_DK + 2 * GLA_DV)
MOE_GROUPS = 4
MOE_PER_GROUP = 8
N_EXPERTS = MOE_GROUPS * MOE_PER_GROUP
MOE_TOPK = 2
EXPERT_FF = D_MODEL // 4
MOE_BLOCK = 128
DEEPNORM_ALPHA = (2 * DEPTH) ** 0.25
LN_EPS = 1e-5

f32 = jnp.float32
bf16 = jnp.bfloat16

VMEM_LIMIT_BYTES = 48 * 1024 * 1024


def _matmul_kernel(x_ref, w_ref, o_ref):
    o_ref[...] = jnp.dot(x_ref[...], w_ref[...], preferred_element_type=f32)


def matmul(x, w, tm=1024, tn=512):
    M, K = x.shape
    Np = w.shape[1]
    assert Np % tn == 0 and M % min(tm, M) == 0
    wb = w.astype(bf16)
    tm = min(tm, M)
    return pl.pallas_call(
        _matmul_kernel,
        grid=(M // tm, Np // tn),
        in_specs=[pl.BlockSpec((tm, K), lambda i, j: (i, 0)),
                  pl.BlockSpec((K, tn), lambda i, j: (0, j))],
        out_specs=pl.BlockSpec((tm, tn), lambda i, j: (i, j)),
        out_shape=jax.ShapeDtypeStruct((M, Np), f32),
        compiler_params=pltpu.CompilerParams(
            dimension_semantics=("parallel", "parallel"), vmem_limit_bytes=VMEM_LIMIT_BYTES),
        name="dense_matmul",
    )(x.astype(bf16), wb)


WKV_CHUNK = 64
WKV_PAIR = 2 * RWKV_HEAD_DIM
WKV_PAIRS = MIX_B // WKV_PAIR


def _dotb(a, b):
    return jnp.dot(a.astype(bf16), b.astype(bf16), preferred_element_type=f32)


def _wkv_kernel(r_ref, lw_ref, k_ref, v_ref, a_ref, b_ref, y_ref, h_ref):
    C = WKV_CHUNK
    P2 = WKV_PAIR

    @pl.when(pl.program_id(1) == 0)
    def _():
        h_ref[...] = jnp.zeros_like(h_ref)

    row = lax.broadcasted_iota(jnp.int32, (P2, P2), 0)
    col = lax.broadcasted_iota(jnp.int32, (P2, P2), 1)
    same_head = (row >= C) == (col >= RWKV_HEAD_DIM)
    t_idx = jnp.bitwise_and(row, C - 1)
    s_idx = jnp.bitwise_and(col, C - 1)
    strict_lower = t_idx > s_idx
    incl_lower = t_idx >= s_idx
    eye = row == col
    tri = (lax.broadcasted_iota(jnp.int32, (C, C), 0) >= lax.broadcasted_iota(jnp.int32, (C, C), 1)).astype(f32)
    zeros_pp = jnp.zeros((P2, P2), f32)

    def bd(x):
        return jnp.where(same_head, jnp.concatenate([x, x], axis=0), 0.0)

    for p in range(WKV_PAIRS):
        sl = slice(p * P2, (p + 1) * P2)
        r = r_ref[0, :, sl]
        lw = lw_ref[0, :, sl]
        k = k_ref[0, :, sl]
        v = v_ref[0, :, sl]
        a = a_ref[0, :, sl]
        b = b_ref[0, :, sl]
        cum = jnp.dot(tri, lw, preferred_element_type=f32, precision=lax.Precision.HIGHEST)
        tot = cum[C - 1:C, :]
        w_incl = jnp.exp(cum)
        w_excl = jnp.exp(cum - lw)
        w_inv = jnp.exp(-cum)
        w_rest = jnp.exp(tot - cum)
        at = bd(a * w_excl)
        rt = bd(r * w_incl)
        bt = bd(b * w_inv)
        kt = bd(k * w_inv)
        bh = bd(b * w_rest)
        kh = bd(k * w_rest)
        vb = bd(v)
        scores = lax.dot_general(
            jnp.concatenate([at, rt], axis=0).astype(bf16), jnp.concatenate([bt, kt], axis=0).astype(bf16),
            (((1,), (1,)), ((), ())), preferred_element_type=f32)
        a_ab = jnp.where(strict_lower, scores[:P2, :P2], 0.0)
        a_ak = jnp.where(strict_lower, scores[:P2, P2:], 0.0)
        a_rb = jnp.where(incl_lower, scores[P2:, :P2], 0.0)
        a_rk = jnp.where(incl_lower, scores[P2:, P2:], 0.0)
        x = jnp.concatenate([at, _dotb(a_ak, vb)], axis=1)
        pw = a_ab
        x = x + _dotb(pw, x)
        for _ in range(int(np.log2(C)) - 1):
            pw = _dotb(pw, pw)
            x = x + _dotb(pw, x)
        rhs = jnp.concatenate([x, jnp.concatenate([zeros_pp, vb], axis=1)], axis=0)
        gz = _dotb(jnp.concatenate([a_rb, a_rk], axis=1), rhs)
        g = rt + gz[:, :P2]
        z = gz[:, P2:]
        mn = _dotb(jnp.concatenate([bh, kh], axis=0).T, rhs)
        m = mn[:, :P2] + jnp.where(eye, jnp.broadcast_to(jnp.exp(tot), (P2, P2)), 0.0)
        n = mn[:, P2:]
        yh = _dotb(jnp.concatenate([g, m], axis=0), h_ref[p])
        y = yh[:P2] + z
        h_ref[p] = yh[P2:] + n
        y_ref[0, :, sl] = y[:C] + y[C:]


def wkv7(r, lw, k, v, a, b):
    B, S, _ = r.shape
    spec = pl.BlockSpec((1, WKV_CHUNK, MIX_B), lambda bi, ci: (bi, ci, 0))
    return pl.pallas_call(
        _wkv_kernel,
        grid=(B, S // WKV_CHUNK),
        in_specs=[spec] * 6,
        out_specs=spec,
        out_shape=jax.ShapeDtypeStruct((B, S, MIX_B), f32),
        scratch_shapes=[pltpu.VMEM((WKV_PAIRS, WKV_PAIR, WKV_PAIR), f32)],
        compiler_params=pltpu.CompilerParams(
            dimension_semantics=("parallel", "arbitrary"), vmem_limit_bytes=VMEM_LIMIT_BYTES),
        name="wkv7_chunked",
    )(r, lw, k, v, a, b)


def layer_norm(x, g, b):
    mu = jnp.mean(x, -1, keepdims=True)
    var = jnp.mean(jnp.square(x - mu), -1, keepdims=True)
    return (x - mu) * lax.rsqrt(var + LN_EPS) * g + b


def head_layer_norm(y, g, b, eps):
    H, d = y.shape[-2:]
    mu = jnp.mean(y, -1, keepdims=True)
    var = jnp.mean(jnp.square(y - mu), -1, keepdims=True)
    out = (y - mu) * lax.rsqrt(var + eps) * g.reshape(H, d) + b.reshape(H, d)
    return out.reshape(*y.shape[:-2], H * d)


def head_rms_norm(y, g, eps):
    H, d = y.shape[-2:]
    out = y * lax.rsqrt(jnp.mean(jnp.square(y), -1, keepdims=True) + eps) * g.reshape(H, d)
    return out.reshape(*y.shape[:-2], H * d)


def rotary(x, positions):
    half = x.shape[-1] // 2
    inv = ROPE_BASE ** (-jnp.arange(half, dtype=f32) / half)
    ang = positions.astype(f32)[..., None] * inv
    cos, sin = jnp.cos(ang)[:, :, None, :], jnp.sin(ang)[:, :, None, :]
    x1, x2 = x[..., :half], x[..., half:]
    return jnp.concatenate([x1 * cos - x2 * sin, x1 * sin + x2 * cos], -1)


RET_LOG_GAMMA = tuple(float(np.log1p(-np.exp2(-5.0 - h))) for h in range(RET_HEADS))
ROPE_HALF = RET_HEAD_DIM // 2


def _ret_kernel(q_ref, k_ref, v_ref, g_ref, pos_ref, inv_ref, gng_ref, gnb_ref, o_ref, s_ref):
    C = RET_CHUNK
    d = RET_HEAD_DIM

    @pl.when(pl.program_id(1) == 0)
    def _():
        s_ref[...] = jnp.zeros_like(s_ref)

    ang = pos_ref[...] * inv_ref[...]
    cos = jnp.cos(ang)
    sin = jnp.sin(ang)

    def rot(x):
        x1, x2 = x[:, :ROPE_HALF], x[:, ROPE_HALF:]
        return jnp.concatenate([x1 * cos - x2 * sin, x1 * sin + x2 * cos], axis=-1)

    ri = lax.broadcasted_iota(jnp.int32, (C, C), 0)
    ci = lax.broadcasted_iota(jnp.int32, (C, C), 1)
    diff = (ri - ci).astype(f32)
    row = lax.broadcasted_iota(jnp.int32, (C, d), 0).astype(f32)
    nt = (((1,), (1,)), ((), ()))

    for h in range(RET_HEADS):
        hs = slice(h * d, (h + 1) * d)
        lg = RET_LOG_GAMMA[h]
        q = rot(q_ref[:, hs])
        k = rot(k_ref[:, hs]) * (d ** -0.5)
        v = v_ref[:, hs].astype(bf16)
        decay = jnp.where(diff >= 0, jnp.exp(lg * jnp.maximum(diff, 0.0)), 0.0)
        scores = lax.dot_general(q.astype(bf16), k.astype(bf16), nt, preferred_element_type=f32) * decay
        state = s_ref[h]
        y = jnp.dot(scores.astype(bf16), v, preferred_element_type=f32)
        y = y + jnp.dot((q * jnp.exp(lg * (row + 1.0))).astype(bf16), state.astype(bf16), preferred_element_type=f32)
        kw = (k * jnp.exp(lg * (C - 1.0 - row))).astype(bf16)
        s_ref[h] = state * float(np.exp(lg * C)) + jnp.dot(kw.T, v, preferred_element_type=f32)
        mu = jnp.mean(y, axis=-1, keepdims=True)
        yc = y - mu
        var = jnp.mean(yc * yc, axis=-1, keepdims=True)
        yn = yc * lax.rsqrt(var + LN_EPS) * gng_ref[:, hs] + gnb_ref[:, hs]
        g = g_ref[:, hs]
        o_ref[:, hs] = (g * jax.nn.sigmoid(g) * yn).astype(o_ref.dtype)


def retention_core(p, positions, gn_g, gn_b, B, S):
    C = RET_CHUNK
    n = S // C
    inv = (ROPE_BASE ** (-jnp.arange(ROPE_HALF, dtype=f32) / ROPE_HALF)).reshape(1, ROPE_HALF)
    pos = positions.astype(f32).reshape(B * S, 1)
    row = lambda bi, ci: bi * n + ci
    const = lambda bi, ci: (0, 0)
    col = lambda j: pl.BlockSpec((C, MIX_A), lambda bi, ci: (row(bi, ci), j))
    return pl.pallas_call(
        _ret_kernel,
        grid=(B, n),
        in_specs=[col(0), col(1), col(2), col(3),
                  pl.BlockSpec((C, 1), lambda bi, ci: (row(bi, ci), 0)),
                  pl.BlockSpec((1, ROPE_HALF), const),
                  pl.BlockSpec((1, MIX_A), const),
                  pl.BlockSpec((1, MIX_A), const)],
        out_specs=pl.BlockSpec((C, MIX_A), lambda bi, ci: (row(bi, ci), 0)),
        out_shape=jax.ShapeDtypeStruct((B * S, MIX_A), bf16),
        scratch_shapes=[pltpu.VMEM((RET_HEADS, RET_HEAD_DIM, RET_HEAD_DIM), f32)],
        compiler_params=pltpu.CompilerParams(
            dimension_semantics=("parallel", "arbitrary"), vmem_limit_bytes=VMEM_LIMIT_BYTES),
        name="retention_chunked",
    )(p, p, p, p, pos, inv, gn_g.reshape(1, MIX_A), gn_b.reshape(1, MIX_A))


def rwkv7_time_mix(pb, v_first, vres, mu, w0, w2, a0, a2, g2, k_k, k_a, r_k, lnx_g, lnx_b):
    B_, S_, _ = pb.shape
    prev = jnp.pad(pb[:, :-1], ((0, 0), (1, 0), (0, 0)))
    pb = pb + (prev - pb) * mu
    r, k, v, wd, ad, gd = jnp.split(pb, RWKV_SPLITS, axis=-1)
    w_log = -jax.nn.softplus(-(w0 + jnp.tanh(wd) @ w2)) - 0.5
    log_decay = -jnp.exp(w_log)
    a = jax.nn.sigmoid(a0 + ad @ a2)
    gate = jax.nn.sigmoid(gd) @ g2
    if vres is None:
        v_first = v
    else:
        v0, v1, v2 = vres
        v = v + (v_first - v) * jax.nn.sigmoid(v0 + (v @ v1) @ v2)
    heads = lambda t: t.reshape(B_, S_, RWKV_HEADS, RWKV_HEAD_DIM)
    kk = heads(k * k_k)
    kk = kk / jnp.maximum(jnp.sqrt(jnp.sum(jnp.square(kk), -1, keepdims=True)), 1e-12)
    kk = kk.reshape(B_, S_, MIX_B)
    k = k * (1.0 + (a - 1.0) * k_a)
    y = wkv7(r, log_decay, k, v, -kk, kk * a)
    rh, kh, vh = heads(r), heads(k), heads(v)
    y = head_layer_norm(heads(y), lnx_g, lnx_b, RWKV_GN_EPS)
    bonus = (jnp.sum(rh * kh * r_k, -1, keepdims=True) * vh).reshape(B_, S_, MIX_B)
    return (y + bonus) * gate, v_first


def even_mixer(h, positions, v_first, vres, win, wo, ret_gn_g, ret_gn_b,
               mu, w0, w2, a0, a2, g2, k_k, k_a, r_k, lnx_g, lnx_b):
    B_, S_, D = h.shape
    p = matmul(h.reshape(B_ * S_, D), jnp.pad(win, ((0, 0), (0, EVEN_PROJ_PAD - EVEN_PROJ))))
    out_a = retention_core(p, positions, ret_gn_g, ret_gn_b, B_, S_)
    out_b, v_first = rwkv7_time_mix(p[:, RET_COLS:EVEN_PROJ].reshape(B_, S_, RWKV_COLS), v_first, vres, mu, w0, w2, a0, a2, g2,
                                    k_k, k_a, r_k, lnx_g, lnx_b)
    y = matmul(jnp.concatenate([out_a, out_b.reshape(B_ * S_, MIX_B).astype(bf16)], -1), wo).reshape(B_, S_, D)
    return y, v_first


GLA_LEVELS = int(np.log2(GLA_CHUNK))
GLA_DECAY_ROWS = (2 + GLA_LEVELS) * GLA_CHUNK
GLA_AD_PAD = 128
ODD_PROJ_PAD = 2 * GLA_DK + 2 * GLA_DV + GLA_AD_PAD


def _gla_decay_operator():
    C = GLA_CHUNK
    i = np.arange(C)[:, None]
    j = np.arange(C)[None, :]
    blocks = [(j <= i), (j > i)]
    for lvl in range(GLA_LEVELS):
        s = C >> (lvl + 1)
        mid = (i // (2 * s)) * (2 * s) + s
        right = i >= mid
        blocks.append(np.where(right, (j >= mid) & (j <= i), (j > i) & (j < mid)))
    return np.concatenate(blocks, 0).astype(np.float32)


def _gla_kernel(q_ref, k_ref, v_ref, r_ref, ad_ref, a2_ref, ab_ref, g_ref, lop_ref, o_ref, s_ref):
    C = GLA_CHUNK

    @pl.when(pl.program_id(1) == 0)
    def _():
        s_ref[...] = jnp.zeros_like(s_ref)

    x = jnp.dot(ad_ref[...].astype(bf16), a2_ref[...], preferred_element_type=f32) + ab_ref[...]
    la = (jnp.minimum(x, 0.0) - jnp.log1p(jnp.exp(-jnp.abs(x)))) * (1.0 / GLA_GATE_NORM)
    hi = la.astype(bf16)
    r1 = la - hi.astype(f32)
    mid = r1.astype(bf16)
    lo = (r1 - mid.astype(f32)).astype(bf16)
    lop = lop_ref[...]
    dec = (jnp.dot(lop, hi, preferred_element_type=f32) + jnp.dot(lop, mid, preferred_element_type=f32)
           + jnp.dot(lop, lo, preferred_element_type=f32))

    ri = lax.broadcasted_iota(jnp.int32, (C, C), 0)
    ci = lax.broadcasted_iota(jnp.int32, (C, C), 1)
    level_masks = []
    for lvl in range(GLA_LEVELS):
        s = C >> (lvl + 1)
        same_block = (ri // (2 * s)) == (ci // (2 * s))
        level_masks.append(same_block & ((ri % (2 * s)) >= s) & ((ci % (2 * s)) < s))
    eye = ri == ci
    nt = (((1,), (1,)), ((), ()))

    for h in range(GLA_HEADS):
        ks = slice(h * GLA_HEAD_DK, (h + 1) * GLA_HEAD_DK)
        vs = slice(h * GLA_HEAD_DV, (h + 1) * GLA_HEAD_DV)
        q = q_ref[:, ks] * (GLA_HEAD_DK ** -0.5)
        k = k_ref[:, ks]
        v = v_ref[:, vs].astype(bf16)
        e = jnp.exp(dec[:, ks])
        state = s_ref[h]
        o = lax.dot_general((q * e[:C]).astype(bf16), state.astype(bf16), nt, preferred_element_type=f32)
        attn = jnp.where(eye, lax.dot_general(q.astype(bf16), k.astype(bf16), nt, preferred_element_type=f32), 0.0)
        for lvl in range(GLA_LEVELS):
            el = e[(2 + lvl) * C:(3 + lvl) * C]
            sc = lax.dot_general((q * el).astype(bf16), (k * el).astype(bf16), nt, preferred_element_type=f32)
            attn = attn + jnp.where(level_masks[lvl], sc, 0.0)
        o = o + jnp.dot(attn.astype(bf16), v, preferred_element_type=f32)
        kd = (k * e[C:2 * C]).astype(bf16)
        s_ref[h] = state * e[C - 1:C] + jnp.dot(v.T, kd, preferred_element_type=f32)
        o = o * lax.rsqrt(jnp.mean(o * o, axis=-1, keepdims=True) + LN_EPS) * g_ref[:, vs]
        rr = r_ref[:, vs]
        o_ref[:, vs] = (rr * jax.nn.sigmoid(rr) * o).astype(o_ref.dtype)


def gla_core(p, a2, ab, norm_g, B, S):
    C = GLA_CHUNK
    n = S // C
    a2p = jnp.pad(a2, ((0, GLA_AD_PAD - GLA_LORA), (0, 0))).astype(bf16)
    lop = jnp.asarray(_gla_decay_operator(), dtype=bf16)
    row = lambda bi, ci: bi * n + ci
    const = lambda bi, ci: (0, 0)
    return pl.pallas_call(
        _gla_kernel,
        grid=(B, n),
        in_specs=[pl.BlockSpec((C, GLA_DK), lambda bi, ci: (row(bi, ci), 0)),
                  pl.BlockSpec((C, GLA_DK), lambda bi, ci: (row(bi, ci), 1)),
                  pl.BlockSpec((C, GLA_DV), lambda bi, ci: (row(bi, ci), 1)),
                  pl.BlockSpec((C, GLA_DV), lambda bi, ci: (row(bi, ci), 2)),
                  pl.BlockSpec((C, GLA_AD_PAD), lambda bi, ci: (row(bi, ci), (2 * GLA_DK + 2 * GLA_DV) // GLA_AD_PAD)),
                  pl.BlockSpec((GLA_AD_PAD, GLA_DK), const),
                  pl.BlockSpec((1, GLA_DK), const),
                  pl.BlockSpec((1, GLA_DV), const),
                  pl.BlockSpec((GLA_DECAY_ROWS, C), const)],
        out_specs=pl.BlockSpec((C, GLA_DV), lambda bi, ci: (row(bi, ci), 0)),
        out_shape=jax.ShapeDtypeStruct((B * S, GLA_DV), bf16),
        scratch_shapes=[pltpu.VMEM((GLA_HEADS, GLA_HEAD_DV, GLA_HEAD_DK), f32)],
        compiler_params=pltpu.CompilerParams(
            dimension_semantics=("parallel", "arbitrary"), vmem_limit_bytes=VMEM_LIMIT_BYTES),
        name="gla_chunked",
    )(p, p, p, p, p, a2p, ab.reshape(1, GLA_DK), norm_g.reshape(1, GLA_DV), lop)


def gla_mixer(h, win, wo, a2, ab, norm_g):
    B_, S_, D = h.shape
    winp = jnp.pad(win, ((0, 0), (0, ODD_PROJ_PAD - ODD_PROJ)))
    p = matmul(h.reshape(B_ * S_, D), winp, tn=896)
    o = gla_core(p, a2, ab, norm_g, B_, S_)
    return matmul(o, wo).reshape(B_, S_, D)


def hier_moe(h, wg, bg, we, be, w1, w3, w2):
    B_, S_, D = h.shape
    T = B_ * S_
    xt = h.reshape(T, D)
    g_prob = jax.nn.softmax(xt @ wg + bg, -1)
    p_top, g_top = lax.top_k(g_prob, 1)
    e_logits = (xt @ we + be).reshape(T, MOE_GROUPS, MOE_PER_GROUP)
    e_sel = jnp.take_along_axis(e_logits, g_top[:, :, None], axis=1)[:, 0]
    e_val, e_top = lax.top_k(e_sel, MOE_TOPK)
    gate = p_top * jax.nn.softmax(e_val, -1)
    expert = g_top * MOE_PER_GROUP + e_top
    A = T * MOE_TOPK
    e_flat = expert.reshape(A)
    tok_flat = jnp.repeat(jnp.arange(T, dtype=jnp.int32), MOE_TOPK)
    order = jnp.argsort(e_flat)
    e_s, tok_s, w_s = e_flat[order], tok_flat[order], gate.reshape(A)[order]
    counts = jnp.bincount(e_flat, length=N_EXPERTS)
    padded = (counts + MOE_BLOCK - 1) // MOE_BLOCK * MOE_BLOCK
    pad_end = jnp.cumsum(padded)
    pad_start = pad_end - padded
    start = jnp.cumsum(counts) - counts
    slot = pad_start[e_s] + jnp.arange(A, dtype=jnp.int32) - start[e_s]
    NB = -(-A // MOE_BLOCK) + N_EXPERTS
    buf_tok = jnp.full((NB * MOE_BLOCK,), T, jnp.int32).at[slot].set(tok_s)
    blk_expert = jnp.minimum(jnp.searchsorted(pad_end, jnp.arange(NB, dtype=jnp.int32) * MOE_BLOCK, side='right'),
                             N_EXPERTS - 1)
    x_pad = jnp.concatenate([xt, jnp.zeros((1, D), xt.dtype)], 0)

    def expert_block(args):
        toks, e = args
        xb = x_pad[toks]
        return (jax.nn.silu(xb @ w1[e]) * (xb @ w3[e])) @ w2[e]

    y_buf = lax.map(expert_block, (buf_tok.reshape(NB, MOE_BLOCK), blk_expert)).reshape(NB * MOE_BLOCK, D)
    y = jax.ops.segment_sum(y_buf[slot] * w_s[:, None], tok_s, num_segments=T)
    return y.reshape(B_, S_, D)


def kernel(x, c, positions, ada_w, ada_b, ln_g, ln_b, ev_win, ev_wo, ret_gn_g, ret_gn_b, rw_mu, rw_w0, rw_w2, rw_a0, rw_a2, rw_g2, rw_kk, rw_ka, rw_rk, rw_lnx_g, rw_lnx_b, rw_v0, rw_v1, rw_v2, od_win, od_wo, gla_a2, gla_ab, gla_norm_g, moe_wg, moe_bg, moe_we, moe_be, moe_w1, moe_w3, moe_w2):
    c_act = jax.nn.silu(c)
    v_first = None
    for layer in range(DEPTH):
        mod = (c_act @ ada_w[layer] + ada_b[layer])[:, None, :]
        sh_m, sc_m, gt_m, sh_f, sc_f, gt_f = jnp.split(mod, 6, axis=-1)
        h = x * (1.0 + sc_m) + sh_m
        j = layer // 2
        if layer % 2 == 0:
            vres = None if j == 0 else (rw_v0[j - 1], rw_v1[j - 1], rw_v2[j - 1])
            y, v_first = even_mixer(h, positions, v_first, vres, ev_win[j], ev_wo[j], ret_gn_g[j], ret_gn_b[j],
                                    rw_mu[j], rw_w0[j], rw_w2[j], rw_a0[j], rw_a2[j], rw_g2[j],
                                    rw_kk[j], rw_ka[j], rw_rk[j], rw_lnx_g[j], rw_lnx_b[j])
        else:
            y = gla_mixer(h, od_win[j], od_wo[j], gla_a2[j], gla_ab[j], gla_norm_g[j])
        x = layer_norm(DEEPNORM_ALPHA * x + (1.0 + gt_m) * y, ln_g[layer, 0], ln_b[layer, 0])
        h = x * (1.0 + sc_f) + sh_f
        y = hier_moe(h, moe_wg[layer], moe_bg[layer], moe_we[layer], moe_be[layer],
                     moe_w1[layer], moe_w3[layer], moe_w2[layer])
        x = layer_norm(DEEPNORM_ALPHA * x + (1.0 + gt_f) * y, ln_g[layer, 1], ln_b[layer, 1])
    return x
```

```python
import functools

import jax
import jax.numpy as jnp
import numpy as np
from jax import lax
from jax.experimental import pallas as pl
from jax.experimental.pallas import tpu as pltpu

D_MODEL = 2048
DEPTH = 4
MIX_A = D_MODEL // 2
RET_HEAD_DIM = 256
RET_HEADS = MIX_A // RET_HEAD_DIM
RET_CHUNK = 128
RET_COLS = 4 * MIX_A
ROPE_BASE = 10000.0
MIX_B = D_MODEL - MIX_A
RWKV_HEAD_DIM = 64
RWKV_HEADS = MIX_B // RWKV_HEAD_DIM
RWKV_LORA_W = 64
RWKV_LORA_A = 64
RWKV_LORA_G = 160
RWKV_COLS = 3 * MIX_B + RWKV_LORA_W + RWKV_LORA_A + RWKV_LORA_G
RWKV_SPLITS = (MIX_B, 2 * MIX_B, 3 * MIX_B, 3 * MIX_B + RWKV_LORA_W, 3 * MIX_B + RWKV_LORA_W + RWKV_LORA_A)
RWKV_GN_EPS = 64e-5
EVEN_PROJ = RET_COLS + RWKV_COLS
EVEN_PROJ_PAD = -(-EVEN_PROJ // 512) * 512
GLA_HEADS = 4
GLA_DK = D_MODEL // 2
GLA_DV = D_MODEL
GLA_HEAD_DK = GLA_DK // GLA_HEADS
GLA_HEAD_DV = GLA_DV // GLA_HEADS
GLA_LORA = 16
GLA_GATE_NORM = 16.0
GLA_CHUNK = 64
ODD_PROJ = 2 * GLA_DK + 2 * GLA_DV + GLA_LORA
GLA_SPLITS = (GLA_DK, 2 * GLA_DK, 2 * GLA_DK + GLA_DV, 2 * GLA_DK + 2 * GLA_DV)
MOE_GROUPS = 4
MOE_PER_GROUP = 8
N_EXPERTS = MOE_GROUPS * MOE_PER_GROUP
MOE_TOPK = 2
EXPERT_FF = D_MODEL // 4
MOE_BLOCK = 128
DEEPNORM_ALPHA = (2 * DEPTH) ** 0.25
LN_EPS = 1e-5

f32 = jnp.float32
bf16 = jnp.bfloat16

VMEM_LIMIT_BYTES = 48 * 1024 * 1024


def _matmul_kernel(x_ref, w_ref, o_ref):
    o_ref[...] = jnp.dot(x_ref[...], w_ref[...], preferred_element_type=f32)


def matmul(x, w, tm=1024, tn=512):
    M, K = x.shape
    Np = w.shape[1]
    assert Np % tn == 0 and M % min(tm, M) == 0
    wb = w.astype(bf16)
    tm = min(tm, M)
    return pl.pallas_call(
        _matmul_kernel,
        grid=(M // tm, Np // tn),
        in_specs=[pl.BlockSpec((tm, K), lambda i, j: (i, 0)),
                  pl.BlockSpec((K, tn), lambda i, j: (0, j))],
        out_specs=pl.BlockSpec((tm, tn), lambda i, j: (i, j)),
        out_shape=jax.ShapeDtypeStruct((M, Np), f32),
        compiler_params=pltpu.CompilerParams(
            dimension_semantics=("parallel", "parallel"), vmem_limit_bytes=VMEM_LIMIT_BYTES),
        name="dense_matmul",
    )(x.astype(bf16), wb)


WKV_CHUNK = 64
WKV_PAIR = 2 * RWKV_HEAD_DIM
WKV_PAIRS = MIX_B // WKV_PAIR


def _dotb(a, b):
    return jnp.dot(a.astype(bf16), b.astype(bf16), preferred_element_type=f32)


def _wkv_kernel(r_ref, lw_ref, k_ref, v_ref, a_ref, b_ref, y_ref, h_ref):
    C = WKV_CHUNK
    P2 = WKV_PAIR

    @pl.when(pl.program_id(1) == 0)
    def _():
        h_ref[...] = jnp.zeros_like(h_ref)

    row = lax.broadcasted_iota(jnp.int32, (P2, P2), 0)
    col = lax.broadcasted_iota(jnp.int32, (P2, P2), 1)
    same_head = (row >= C) == (col >= RWKV_HEAD_DIM)
    t_idx = jnp.bitwise_and(row, C - 1)
    s_idx = jnp.bitwise_and(col, C - 1)
    strict_lower = t_idx > s_idx
    incl_lower = t_idx >= s_idx
    eye = row == col
    tri = (lax.broadcasted_iota(jnp.int32, (C, C), 0) >= lax.broadcasted_iota(jnp.int32, (C, C), 1)).astype(f32)
    zeros_pp = jnp.zeros((P2, P2), f32)

    def bd(x):
        return jnp.where(same_head, jnp.concatenate([x, x], axis=0), 0.0)

    for p in range(WKV_PAIRS):
        sl = slice(p * P2, (p + 1) * P2)
        r = r_ref[0, :, sl]
        lw = lw_ref[0, :, sl]
        k = k_ref[0, :, sl]
        v = v_ref[0, :, sl]
        a = a_ref[0, :, sl]
        b = b_ref[0, :, sl]
        cum = jnp.dot(tri, lw, preferred_element_type=f32, precision=lax.Precision.HIGHEST)
        tot = cum[C - 1:C, :]
        w_incl = jnp.exp(cum)
        w_excl = jnp.exp(cum - lw)
        w_inv = jnp.exp(-cum)
        w_rest = jnp.exp(tot - cum)
        at = bd(a * w_excl)
        rt = bd(r * w_incl)
        bt = bd(b * w_inv)
        kt = bd(k * w_inv)
        bh = bd(b * w_rest)
        kh = bd(k * w_rest)
        vb = bd(v)
        scores = lax.dot_general(
            jnp.concatenate([at, rt], axis=0).astype(bf16), jnp.concatenate([bt, kt], axis=0).astype(bf16),
            (((1,), (1,)), ((), ())), preferred_element_type=f32)
        a_ab = jnp.where(strict_lower, scores[:P2, :P2], 0.0)
        a_ak = jnp.where(strict_lower, scores[:P2, P2:], 0.0)
        a_rb = jnp.where(incl_lower, scores[P2:, :P2], 0.0)
        a_rk = jnp.where(incl_lower, scores[P2:, P2:], 0.0)
        x = jnp.concatenate([at, _dotb(a_ak, vb)], axis=1)
        pw = a_ab
        x = x + _dotb(pw, x)
        for _ in range(int(np.log2(C)) - 1):
            pw = _dotb(pw, pw)
            x = x + _dotb(pw, x)
        rhs = jnp.concatenate([x, jnp.concatenate([zeros_pp, vb], axis=1)], axis=0)
        gz = _dotb(jnp.concatenate([a_rb, a_rk], axis=1), rhs)
        g = rt + gz[:, :P2]
        z = gz[:, P2:]
        mn = _dotb(jnp.concatenate([bh, kh], axis=0).T, rhs)
        m = mn[:, :P2] + jnp.where(eye, jnp.broadcast_to(jnp.exp(tot), (P2, P2)), 0.0)
        n = mn[:, P2:]
        yh = _dotb(jnp.concatenate([g, m], axis=0), h_ref[p])
        y = yh[:P2] + z
        h_ref[p] = yh[P2:] + n
        y_ref[0, :, sl] = y[:C] + y[C:]


def wkv7(r, lw, k, v, a, b):
    B, S, _ = r.shape
    spec = pl.BlockSpec((1, WKV_CHUNK, MIX_B), lambda bi, ci: (bi, ci, 0))
    return pl.pallas_call(
        _wkv_kernel,
        grid=(B, S // WKV_CHUNK),
        in_specs=[spec] * 6,
        out_specs=spec,
        out_shape=jax.ShapeDtypeStruct((B, S, MIX_B), f32),
        scratch_shapes=[pltpu.VMEM((WKV_PAIRS, WKV_PAIR, WKV_PAIR), f32)],
        compiler_params=pltpu.CompilerParams(
            dimension_semantics=("parallel", "arbitrary"), vmem_limit_bytes=VMEM_LIMIT_BYTES),
        name="wkv7_chunked",
    )(r, lw, k, v, a, b)


def layer_norm(x, g, b):
    mu = jnp.mean(x, -1, keepdims=True)
    var = jnp.mean(jnp.square(x - mu), -1, keepdims=True)
    return (x - mu) * lax.rsqrt(var + LN_EPS) * g + b


def head_layer_norm(y, g, b, eps):
    H, d = y.shape[-2:]
    mu = jnp.mean(y, -1, keepdims=True)
    var = jnp.mean(jnp.square(y - mu), -1, keepdims=True)
    out = (y - mu) * lax.rsqrt(var + eps) * g.reshape(H, d) + b.reshape(H, d)
    return out.reshape(*y.shape[:-2], H * d)


RET_LOG_GAMMA = tuple(float(np.log1p(-np.exp2(-5.0 - h))) for h in range(RET_HEADS))
ROPE_HALF = RET_HEAD_DIM // 2


def _ret_kernel(q_ref, k_ref, v_ref, g_ref, pos_ref, inv_ref, gng_ref, gnb_ref, o_ref, s_ref):
    C = RET_CHUNK
    d = RET_HEAD_DIM

    @pl.when(pl.program_id(1) == 0)
    def _():
        s_ref[...] = jnp.zeros_like(s_ref)

    ang = pos_ref[...] * inv_ref[...]
    cos = jnp.cos(ang)
    sin = jnp.sin(ang)

    def rot(x):
        x1, x2 = x[:, :ROPE_HALF], x[:, ROPE_HALF:]
        return jnp.concatenate([x1 * cos - x2 * sin, x1 * sin + x2 * cos], axis=-1)

    ri = lax.broadcasted_iota(jnp.int32, (C, C), 0)
    ci = lax.broadcasted_iota(jnp.int32, (C, C), 1)
    diff = (ri - ci).astype(f32)
    row = lax.broadcasted_iota(jnp.int32, (C, d), 0).astype(f32)
    nt = (((1,), (1,)), ((), ()))

    for h in range(RET_HEADS):
        hs = slice(h * d, (h + 1) * d)
        lg = RET_LOG_GAMMA[h]
        q = rot(q_ref[:, hs])
        k = rot(k_ref[:, hs]) * (d ** -0.5)
        v = v_ref[:, hs].astype(bf16)
        decay = jnp.where(diff >= 0, jnp.exp(lg * jnp.maximum(diff, 0.0)), 0.0)
        scores = lax.dot_general(q.astype(bf16), k.astype(bf16), nt, preferred_element_type=f32) * decay
        state = s_ref[h]
        y = jnp.dot(scores.astype(bf16), v, preferred_element_type=f32)
        y = y + jnp.dot((q * jnp.exp(lg * (row + 1.0))).astype(bf16), state.astype(bf16), preferred_element_type=f32)
        kw = (k * jnp.exp(lg * (C - 1.0 - row))).astype(bf16)
        s_ref[h] = state * float(np.exp(lg * C)) + jnp.dot(kw.T, v, preferred_element_type=f32)
        mu = jnp.mean(y, axis=-1, keepdims=True)
        yc = y - mu
        var = jnp.mean(yc * yc, axis=-1, keepdims=True)
        yn = yc * lax.rsqrt(var + LN_EPS) * gng_ref[:, hs] + gnb_ref[:, hs]
        g = g_ref[:, hs]
        o_ref[:, hs] = (g * jax.nn.sigmoid(g) * yn).astype(o_ref.dtype)


def retention_core(p, positions, gn_g, gn_b, B, S):
    C = RET_CHUNK
    n = S // C
    inv = (ROPE_BASE ** (-jnp.arange(ROPE_HALF, dtype=f32) / ROPE_HALF)).reshape(1, ROPE_HALF)
    pos = positions.astype(f32).reshape(B * S, 1)
    row = lambda bi, ci: bi * n + ci
    const = lambda bi, ci: (0, 0)
    col = lambda j: pl.BlockSpec((C, MIX_A), lambda bi, ci: (row(bi, ci), j))
    return pl.pallas_call(
        _ret_kernel,
        grid=(B, n),
        in_specs=[col(0), col(1), col(2), col(3),
                  pl.BlockSpec((C, 1), lambda bi, ci: (row(bi, ci), 0)),
                  pl.BlockSpec((1, ROPE_HALF), const),
                  pl.BlockSpec((1, MIX_A), const),
                  pl.BlockSpec((1, MIX_A), const)],
        out_specs=pl.BlockSpec((C, MIX_A), lambda bi, ci: (row(bi, ci), 0)),
        out_shape=jax.ShapeDtypeStruct((B * S, MIX_A), bf16),
        scratch_shapes=[pltpu.VMEM((RET_HEADS, RET_HEAD_DIM, RET_HEAD_DIM), f32)],
        compiler_params=pltpu.CompilerParams(
            dimension_semantics=("parallel", "arbitrary"), vmem_limit_bytes=VMEM_LIMIT_BYTES),
        name="retention_chunked",
    )(p, p, p, p, pos, inv, gn_g.reshape(1, MIX_A), gn_b.reshape(1, MIX_A))


def rwkv7_time_mix(pb, v_first, vres, mu, w0, w2, a0, a2, g2, k_k, k_a, r_k, lnx_g, lnx_b):
    B_, S_, _ = pb.shape
    prev = jnp.pad(pb[:, :-1], ((0, 0), (1, 0), (0, 0)))
    pb = pb + (prev - pb) * mu
    r, k, v, wd, ad, gd = jnp.split(pb, RWKV_SPLITS, axis=-1)
    w_log = -jax.nn.softplus(-(w0 + jnp.tanh(wd) @ w2)) - 0.5
    log_decay = -jnp.exp(w_log)
    a = jax.nn.sigmoid(a0 + ad @ a2)
    gate = jax.nn.sigmoid(gd) @ g2
    if vres is None:
        v_first = v
    else:
        v0, v1, v2 = vres
        v = v + (v_first - v) * jax.nn.sigmoid(v0 + (v @ v1) @ v2)
    heads = lambda t: t.reshape(B_, S_, RWKV_HEADS, RWKV_HEAD_DIM)
    kk = heads(k * k_k)
    kk = kk / jnp.maximum(jnp.sqrt(jnp.sum(jnp.square(kk), -1, keepdims=True)), 1e-12)
    kk = kk.reshape(B_, S_, MIX_B)
    k = k * (1.0 + (a - 1.0) * k_a)
    y = wkv7(r, log_decay, k, v, -kk, kk * a)
    rh, kh, vh = heads(r), heads(k), heads(v)
    y = head_layer_norm(heads(y), lnx_g, lnx_b, RWKV_GN_EPS)
    bonus = (jnp.sum(rh * kh * r_k, -1, keepdims=True) * vh).reshape(B_, S_, MIX_B)
    return (y + bonus) * gate, v_first


def even_mixer(h, positions, v_first, vres, win, wo, ret_gn_g, ret_gn_b,
               mu, w0, w2, a0, a2, g2, k_k, k_a, r_k, lnx_g, lnx_b):
    B_, S_, D = h.shape
    p = matmul(h.reshape(B_ * S_, D), jnp.pad(win, ((0, 0), (0, EVEN_PROJ_PAD - EVEN_PROJ))))
    out_a = retention_core(p, positions, ret_gn_g, ret_gn_b, B_, S_)
    out_b, v_first = rwkv7_time_mix(p[:, RET_COLS:EVEN_PROJ].reshape(B_, S_, RWKV_COLS), v_first, vres, mu, w0, w2, a0, a2, g2,
                                    k_k, k_a, r_k, lnx_g, lnx_b)
    y = matmul(jnp.concatenate([out_a, out_b.reshape(B_ * S_, MIX_B).astype(bf16)], -1), wo).reshape(B_, S_, D)
    return y, v_first


GLA_LEVELS = int(np.log2(GLA_CHUNK))
GLA_DECAY_ROWS = (2 + GLA_LEVELS) * GLA_CHUNK
GLA_AD_PAD = 128
ODD_PROJ_PAD = 2 * GLA_DK + 2 * GLA_DV + GLA_AD_PAD


def _gla_decay_operator():
    C = GLA_CHUNK
    i = np.arange(C)[:, None]
    j = np.arange(C)[None, :]
    blocks = [(j <= i), (j > i)]
    for lvl in range(GLA_LEVELS):
        s = C >> (lvl + 1)
        mid = (i // (2 * s)) * (2 * s) + s
        right = i >= mid
        blocks.append(np.where(right, (j >= mid) & (j <= i), (j > i) & (j < mid)))
    return np.concatenate(blocks, 0).astype(np.float32)


def _gla_kernel(q_ref, k_ref, v_ref, r_ref, ad_ref, a2_ref, ab_ref, g_ref, lop_ref, o_ref, s_ref):
    C = GLA_CHUNK

    @pl.when(pl.program_id(1) == 0)
    def _():
        s_ref[...] = jnp.zeros_like(s_ref)

    x = jnp.dot(ad_ref[...].astype(bf16), a2_ref[...], preferred_element_type=f32) + ab_ref[...]
    la = (jnp.minimum(x, 0.0) - jnp.log1p(jnp.exp(-jnp.abs(x)))) * (1.0 / GLA_GATE_NORM)
    hi = la.astype(bf16)
    r1 = la - hi.astype(f32)
    mid = r1.astype(bf16)
    lo = (r1 - mid.astype(f32)).astype(bf16)
    lop = lop_ref[...]
    dec = (jnp.dot(lop, hi, preferred_element_type=f32) + jnp.dot(lop, mid, preferred_element_type=f32)
           + jnp.dot(lop, lo, preferred_element_type=f32))

    ri = lax.broadcasted_iota(jnp.int32, (C, C), 0)
    ci = lax.broadcasted_iota(jnp.int32, (C, C), 1)
    level_masks = []
    for lvl in range(GLA_LEVELS):
        s = C >> (lvl + 1)
        same_block = (ri // (2 * s)) == (ci // (2 * s))
        level_masks.append(same_block & ((ri % (2 * s)) >= s) & ((ci % (2 * s)) < s))
    eye = ri == ci
    nt = (((1,), (1,)), ((), ()))

    for h in range(GLA_HEADS):
        ks = slice(h * GLA_HEAD_DK, (h + 1) * GLA_HEAD_DK)
        vs = slice(h * GLA_HEAD_DV, (h + 1) * GLA_HEAD_DV)
        q = q_ref[:, ks] * (GLA_HEAD_DK ** -0.5)
        k = k_ref[:, ks]
        v = v_ref[:, vs].astype(bf16)
        e = jnp.exp(dec[:, ks])
        state = s_ref[h]
        o = lax.dot_general((q * e[:C]).astype(bf16), state.astype(bf16), nt, preferred_element_type=f32)
        attn = jnp.where(eye, lax.dot_general(q.astype(bf16), k.astype(bf16), nt, preferred_element_type=f32), 0.0)
        for lvl in range(GLA_LEVELS):
            el = e[(2 + lvl) * C:(3 + lvl) * C]
            sc = lax.dot_general((q * el).astype(bf16), (k * el).astype(bf16), nt, preferred_element_type=f32)
            attn = attn + jnp.where(level_masks[lvl], sc, 0.0)
        o = o + jnp.dot(attn.astype(bf16), v, preferred_element_type=f32)
        kd = (k * e[C:2 * C]).astype(bf16)
        s_ref[h] = state * e[C - 1:C] + jnp.dot(v.T, kd, preferred_element_type=f32)
        o = o * lax.rsqrt(jnp.mean(o * o, axis=-1, keepdims=True) + LN_EPS) * g_ref[:, vs]
        rr = r_ref[:, vs]
        o_ref[:, vs] = (rr * jax.nn.sigmoid(rr) * o).astype(o_ref.dtype)


def gla_core(p, a2, ab, norm_g, B, S):
    C = GLA_CHUNK
    n = S // C
    a2p = jnp.pad(a2, ((0, GLA_AD_PAD - GLA_LORA), (0, 0))).astype(bf16)
    lop = jnp.asarray(_gla_decay_operator(), dtype=bf16)
    row = lambda bi, ci: bi * n + ci
    const = lambda bi, ci: (0, 0)
    return pl.pallas_call(
        _gla_kernel,
        grid=(B, n),
        in_specs=[pl.BlockSpec((C, GLA_DK), lambda bi, ci: (row(bi, ci), 0)),
                  pl.BlockSpec((C, GLA_DK), lambda bi, ci: (row(bi, ci), 1)),
                  pl.BlockSpec((C, GLA_DV), lambda bi, ci: (row(bi, ci), 1)),
                  pl.BlockSpec((C, GLA_DV), lambda bi, ci: (row(bi, ci), 2)),
                  pl.BlockSpec((C, GLA_AD_PAD), lambda bi, ci: (row(bi, ci), (2 * GLA_DK + 2 * GLA_DV) // GLA_AD_PAD)),
                  pl.BlockSpec((GLA_AD_PAD, GLA_DK), const),
                  pl.BlockSpec((1, GLA_DK), const),
                  pl.BlockSpec((1, GLA_DV), const),
                  pl.BlockSpec((GLA_DECAY_ROWS, C), const)],
        out_specs=pl.BlockSpec((C, GLA_DV), lambda bi, ci: (row(bi, ci), 0)),
        out_shape=jax.ShapeDtypeStruct((B * S, GLA_DV), bf16),
        scratch_shapes=[pltpu.VMEM((GLA_HEADS, GLA_HEAD_DV, GLA_HEAD_DK), f32)],
        compiler_params=pltpu.CompilerParams(
            dimension_semantics=("parallel", "arbitrary"), vmem_limit_bytes=VMEM_LIMIT_BYTES),
        name="gla_chunked",
    )(p, p, p, p, p, a2p, ab.reshape(1, GLA_DK), norm_g.reshape(1, GLA_DV), lop)


def gla_mixer(h, win, wo, a2, ab, norm_g):
    B_, S_, D = h.shape
    winp = jnp.pad(win, ((0, 0), (0, ODD_PROJ_PAD - ODD_PROJ)))
    p = matmul(h.reshape(B_ * S_, D), winp, tn=896)
    o = gla_core(p, a2, ab, norm_g, B_, S_)
    return matmul(o, wo).reshape(B_, S_, D)


MOE_ROWS = 256
MOE_VMEM_LIMIT_BYTES = 56 * 1024 * 1024
COMBINE_ROWS = 256


def _moe_ffn_kernel(tok_ref, be_ref, nused_ref, x_hbm, w1_ref, w3_ref, w2_ref, gate_ref, y_ref, xbuf, sem):
    i = pl.program_id(0)
    n_used = nused_ref[0]
    slot = lax.rem(i, 2)

    def start_gather(blk, s):
        def body(r, carry):
            t = tok_ref[blk * MOE_ROWS + r]
            pltpu.make_async_copy(x_hbm.at[pl.ds(t, 1)], xbuf.at[s, pl.ds(r, 1)], sem.at[s]).start()
            return carry
        lax.fori_loop(0, MOE_ROWS, body, 0, unroll=8)

    @pl.when((i == 0) & (n_used > 0))
    def _():
        start_gather(0, 0)

    @pl.when(i + 1 < n_used)
    def _():
        start_gather(i + 1, 1 - slot)

    @pl.when(i < n_used)
    def _():
        pltpu.make_async_copy(x_hbm.at[pl.ds(0, MOE_ROWS)], xbuf.at[slot], sem.at[slot]).wait()
        xb = xbuf[slot].astype(bf16)
        a = jnp.dot(xb, w1_ref[0].astype(bf16), preferred_element_type=f32)
        b = jnp.dot(xb, w3_ref[0].astype(bf16), preferred_element_type=f32)
        hid = (a * jax.nn.sigmoid(a) * b).astype(bf16)
        y_ref[...] = jnp.dot(hid, w2_ref[0].astype(bf16), preferred_element_type=f32) * gate_ref[...]

    @pl.when(i >= n_used)
    def _():
        y_ref[...] = jnp.zeros_like(y_ref)


def moe_ffn(x_pad, buf_tok, blk_expert, n_used, gate_rows, w1, w3, w2):
    D = x_pad.shape[1]
    nblk = blk_expert.shape[0]
    wspec = lambda shape: pl.BlockSpec((1,) + shape, lambda i, tok, be, nu: (be[i], 0, 0))
    return pl.pallas_call(
        _moe_ffn_kernel,
        grid_spec=pltpu.PrefetchScalarGridSpec(
            num_scalar_prefetch=3,
            grid=(nblk,),
            in_specs=[pl.BlockSpec(memory_space=pl.ANY),
                      wspec((D, EXPERT_FF)), wspec((D, EXPERT_FF)), wspec((EXPERT_FF, D)),
                      pl.BlockSpec((MOE_ROWS, 1), lambda i, tok, be, nu: (i, 0))],
            out_specs=pl.BlockSpec((MOE_ROWS, D), lambda i, tok, be, nu: (i, 0)),
            scratch_shapes=[pltpu.VMEM((2, MOE_ROWS, D), f32), pltpu.SemaphoreType.DMA((2,))]),
        out_shape=jax.ShapeDtypeStruct((nblk * MOE_ROWS, D), f32),
        compiler_params=pltpu.CompilerParams(
            dimension_semantics=("arbitrary",), vmem_limit_bytes=MOE_VMEM_LIMIT_BYTES),
        name="moe_expert_ffn",
    )(buf_tok, blk_expert, n_used, x_pad, w1, w3, w2, gate_rows)


def _moe_combine_kernel(s0_ref, s1_ref, y_hbm, x_ref, gt_ref, g_ref, b_ref, o_ref, buf, sem):
    i = pl.program_id(0)
    n = pl.num_programs(0)
    slot = lax.rem(i, 2)

    def start_gather(blk, s):
        def body(r, carry):
            t = blk * COMBINE_ROWS + r
            pltpu.make_async_copy(y_hbm.at[pl.ds(s0_ref[t], 1)], buf.at[s, 0, pl.ds(r, 1)], sem.at[s]).start()
            pltpu.make_async_copy(y_hbm.at[pl.ds(s1_ref[t], 1)], buf.at[s, 1, pl.ds(r, 1)], sem.at[s]).start()
            return carry
        lax.fori_loop(0, COMBINE_ROWS, body, 0, unroll=8)

    @pl.when(i == 0)
    def _():
        start_gather(0, 0)

    @pl.when(i + 1 < n)
    def _():
        start_gather(i + 1, 1 - slot)

    for c in range(MOE_TOPK):
        pltpu.make_async_copy(y_hbm.at[pl.ds(0, COMBINE_ROWS)], buf.at[slot, c], sem.at[slot]).wait()
    y = buf[slot, 0] + buf[slot, 1]
    z = DEEPNORM_ALPHA * x_ref[...] + (1.0 + gt_ref[0]) * y
    mu = jnp.mean(z, axis=-1, keepdims=True)
    zc = z - mu
    var = jnp.mean(zc * zc, axis=-1, keepdims=True)
    o_ref[...] = zc * lax.rsqrt(var + LN_EPS) * g_ref[...] + b_ref[...]


def moe_combine_norm(y_rows, slot0, slot1, x, gt, ln_g, ln_b, B, S):
    T, D = x.shape
    tiles_per_batch = S // COMBINE_ROWS
    const = lambda i, s0, s1: (0, 0)
    return pl.pallas_call(
        _moe_combine_kernel,
        grid_spec=pltpu.PrefetchScalarGridSpec(
            num_scalar_prefetch=2,
            grid=(T // COMBINE_ROWS,),
            in_specs=[pl.BlockSpec(memory_space=pl.ANY),
                      pl.BlockSpec((COMBINE_ROWS, D), lambda i, s0, s1: (i, 0)),
                      pl.BlockSpec((1, 1, D), lambda i, s0, s1: (i // tiles_per_batch, 0, 0)),
                      pl.BlockSpec((1, D), const), pl.BlockSpec((1, D), const)],
            out_specs=pl.BlockSpec((COMBINE_ROWS, D), lambda i, s0, s1: (i, 0)),
            scratch_shapes=[pltpu.VMEM((2, MOE_TOPK, COMBINE_ROWS, D), f32), pltpu.SemaphoreType.DMA((2,))]),
        out_shape=jax.ShapeDtypeStruct((T, D), f32),
        compiler_params=pltpu.CompilerParams(
            dimension_semantics=("arbitrary",), vmem_limit_bytes=VMEM_LIMIT_BYTES),
        name="moe_combine_norm",
    )(slot0, slot1, y_rows, x, gt, ln_g.reshape(1, D), ln_b.reshape(1, D))


def moe_route(xt, wg, bg, we, be):
    T = xt.shape[0]
    g_prob = jax.nn.softmax(xt @ wg + bg, -1)
    p_top, g_top = lax.top_k(g_prob, 1)
    e_logits = (xt @ we + be).reshape(T, MOE_GROUPS, MOE_PER_GROUP)
    e_sel = jnp.take_along_axis(e_logits, g_top[:, :, None], axis=1)[:, 0]
    e_val, e_top = lax.top_k(e_sel, MOE_TOPK)
    gate = p_top * jax.nn.softmax(e_val, -1)
    expert = g_top * MOE_PER_GROUP + e_top
    return gate, expert


def moe_layout(gate, expert):
    T = gate.shape[0]
    A = T * MOE_TOPK
    e_flat = expert.reshape(A)
    order = jnp.argsort(e_flat)
    e_s = e_flat[order]
    counts = jnp.bincount(e_flat, length=N_EXPERTS)
    padded = (counts + MOE_ROWS - 1) // MOE_ROWS * MOE_ROWS
    pad_end = jnp.cumsum(padded)
    pad_start = pad_end - padded
    start = jnp.cumsum(counts) - counts
    slot_sorted = (pad_start[e_s] + jnp.arange(A, dtype=jnp.int32) - start[e_s]).astype(jnp.int32)
    slot = jnp.zeros((A,), jnp.int32).at[order].set(slot_sorted)
    tok_flat = jnp.arange(A, dtype=jnp.int32) // MOE_TOPK
    nblk = A // MOE_ROWS + N_EXPERTS
    rows = nblk * MOE_ROWS
    buf_tok = jnp.full((rows,), T, jnp.int32).at[slot].set(tok_flat)
    gate_rows = jnp.zeros((rows,), f32).at[slot].set(gate.reshape(A))
    n_used = (pad_end[-1] // MOE_ROWS).astype(jnp.int32)
    blk = jnp.arange(nblk, dtype=jnp.int32)
    blk_expert = jnp.searchsorted(pad_end, jnp.minimum(blk, n_used - 1) * MOE_ROWS, side='right').astype(jnp.int32)
    blk_expert = jnp.minimum(blk_expert, N_EXPERTS - 1)
    slot2 = slot.reshape(T, MOE_TOPK)
    return buf_tok, blk_expert, n_used.reshape(1), gate_rows.reshape(rows, 1), slot2[:, 0], slot2[:, 1]


def moe_block(x, h, gt, ln_g, ln_b, wg, bg, we, be, w1, w3, w2, B, S):
    T, D = h.shape
    gate, expert = moe_route(h, wg, bg, we, be)
    buf_tok, blk_expert, n_used, gate_rows, slot0, slot1 = moe_layout(gate, expert)
    x_pad = jnp.concatenate([h, jnp.zeros((1, D), h.dtype)], 0)
    y_rows = moe_ffn(x_pad, buf_tok, blk_expert, n_used, gate_rows, w1, w3, w2)
    return moe_combine_norm(y_rows, slot0, slot1, x, gt, ln_g, ln_b, B, S)


def kernel(x, c, positions, ada_w, ada_b, ln_g, ln_b, ev_win, ev_wo, ret_gn_g, ret_gn_b, rw_mu, rw_w0, rw_w2, rw_a0, rw_a2, rw_g2, rw_kk, rw_ka, rw_rk, rw_lnx_g, rw_lnx_b, rw_v0, rw_v1, rw_v2, od_win, od_wo, gla_a2, gla_ab, gla_norm_g, moe_wg, moe_bg, moe_we, moe_be, moe_w1, moe_w3, moe_w2):
    c_act = jax.nn.silu(c)
    v_first = None
    for layer in range(DEPTH):
        mod = (c_act @ ada_w[layer] + ada_b[layer])[:, None, :]
        sh_m, sc_m, gt_m, sh_f, sc_f, gt_f = jnp.split(mod, 6, axis=-1)
        h = x * (1.0 + sc_m) + sh_m
        j = layer // 2
        if layer % 2 == 0:
            vres = None if j == 0 else (rw_v0[j - 1], rw_v1[j - 1], rw_v2[j - 1])
            y, v_first = even_mixer(h, positions, v_first, vres, ev_win[j], ev_wo[j], ret_gn_g[j], ret_gn_b[j],
                                    rw_mu[j], rw_w0[j], rw_w2[j], rw_a0[j], rw_a2[j], rw_g2[j],
                                    rw_kk[j], rw_ka[j], rw_rk[j], rw_lnx_g[j], rw_lnx_b[j])
        else:
            y = gla_mixer(h, od_win[j], od_wo[j], gla_a2[j], gla_ab[j], gla_norm_g[j])
        x = layer_norm(DEEPNORM_ALPHA * x + (1.0 + gt_m) * y, ln_g[layer, 0], ln_b[layer, 0])
        h = x * (1.0 + sc_f) + sh_f
        B_, S_, D = x.shape
        x = moe_block(x.reshape(B_ * S_, D), h.reshape(B_ * S_, D), gt_f, ln_g[layer, 1], ln_b[layer, 1],
                      moe_wg[layer], moe_bg[layer], moe_we[layer], moe_be[layer],
                      moe_w1[layer], moe_w3[layer], moe_w2[layer], B_, S_).reshape(B_, S_, D)
    return x
```

```python
import functools

import jax
import jax.numpy as jnp
import numpy as np
from jax import lax
from jax.experimental import pallas as pl
from jax.experimental.pallas import tpu as pltpu

D_MODEL = 2048
DEPTH = 4
MIX_A = D_MODEL // 2
RET_HEAD_DIM = 256
RET_HEADS = MIX_A // RET_HEAD_DIM
RET_CHUNK = 128
RET_COLS = 4 * MIX_A
ROPE_BASE = 10000.0
MIX_B = D_MODEL - MIX_A
RWKV_HEAD_DIM = 64
RWKV_HEADS = MIX_B // RWKV_HEAD_DIM
RWKV_LORA_W = 64
RWKV_LORA_A = 64
RWKV_LORA_G = 160
RWKV_COLS = 3 * MIX_B + RWKV_LORA_W + RWKV_LORA_A + RWKV_LORA_G
RWKV_SPLITS = (MIX_B, 2 * MIX_B, 3 * MIX_B, 3 * MIX_B + RWKV_LORA_W, 3 * MIX_B + RWKV_LORA_W + RWKV_LORA_A)
RWKV_GN_EPS = 64e-5
GLA_HEADS = 4
GLA_DK = D_MODEL // 2
GLA_DV = D_MODEL
GLA_HEAD_DK = GLA_DK // GLA_HEADS
GLA_HEAD_DV = GLA_DV // GLA_HEADS
GLA_LORA = 16
GLA_GATE_NORM = 16.0
GLA_CHUNK = 64
ODD_PROJ = 2 * GLA_DK + 2 * GLA_DV + GLA_LORA
MOE_GROUPS = 4
MOE_PER_GROUP = 8
N_EXPERTS = MOE_GROUPS * MOE_PER_GROUP
MOE_TOPK = 2
EXPERT_FF = D_MODEL // 4
DEEPNORM_ALPHA = (2 * DEPTH) ** 0.25
LN_EPS = 1e-5

f32 = jnp.float32
bf16 = jnp.bfloat16

VMEM_LIMIT_BYTES = 48 * 1024 * 1024


def _dotb(a, b):
    return jnp.dot(a.astype(bf16), b.astype(bf16), preferred_element_type=f32)


def _split_dot(x, ones_bf16):
    hi = x.astype(bf16)
    lo = (x - hi.astype(f32)).astype(bf16)
    return jnp.dot(hi, ones_bf16, preferred_element_type=f32) + jnp.dot(lo, ones_bf16, preferred_element_type=f32)


def _softplus(z):
    return jnp.maximum(z, 0.0) + jnp.log1p(jnp.exp(-jnp.abs(z)))


ADA_ROWS = 8
ADA_TN = 1024


def _ada_kernel(c_ref, w_ref, b_ref, o_ref):
    c = c_ref[...]
    act = (c * jax.nn.sigmoid(c)).astype(bf16)
    o_ref[0] = jnp.dot(act, w_ref[0].astype(bf16), preferred_element_type=f32) + b_ref[0]


def ada_mod(c, ada_w, ada_b):
    B, D = c.shape
    depth, _, N = ada_w.shape
    c_pad = jnp.pad(c, ((0, ADA_ROWS - B), (0, 0)))
    out = pl.pallas_call(
        _ada_kernel,
        grid=(depth, N // ADA_TN),
        in_specs=[pl.BlockSpec((ADA_ROWS, D), lambda l, j: (0, 0)),
                  pl.BlockSpec((1, D, ADA_TN), lambda l, j: (l, 0, j)),
                  pl.BlockSpec((1, 1, ADA_TN), lambda l, j: (l, 0, j))],
        out_specs=pl.BlockSpec((1, ADA_ROWS, ADA_TN), lambda l, j: (l, 0, j)),
        out_shape=jax.ShapeDtypeStruct((depth, ADA_ROWS, N), f32),
        compiler_params=pltpu.CompilerParams(
            dimension_semantics=("parallel", "parallel"), vmem_limit_bytes=VMEM_LIMIT_BYTES),
        name="ada_mod",
    )(c_pad, ada_w, ada_b.reshape(depth, 1, N))
    return out[:, :B]


def _matmul_kernel(x_ref, w_ref, o_ref):
    o_ref[...] = jnp.dot(x_ref[...], w_ref[...], preferred_element_type=f32)


def matmul(x, w, tm=1024, tn=512):
    M, K = x.shape
    Np = w.shape[1]
    assert Np % tn == 0 and M % min(tm, M) == 0
    wb = w.astype(bf16)
    tm = min(tm, M)
    return pl.pallas_call(
        _matmul_kernel,
        grid=(M // tm, Np // tn),
        in_specs=[pl.BlockSpec((tm, K), lambda i, j: (i, 0)),
                  pl.BlockSpec((K, tn), lambda i, j: (0, j))],
        out_specs=pl.BlockSpec((tm, tn), lambda i, j: (i, j)),
        out_shape=jax.ShapeDtypeStruct((M, Np), f32),
        compiler_params=pltpu.CompilerParams(
            dimension_semantics=("parallel", "parallel"), vmem_limit_bytes=VMEM_LIMIT_BYTES),
        name="dense_matmul",
    )(x.astype(bf16), wb)


WKV_CHUNK = 64
WKV_PAIR = 2 * RWKV_HEAD_DIM
WKV_PAIRS = MIX_B // WKV_PAIR
RWKV_WA_PAD = 128
RWKV_G_PAD = 256
RWKV_V_PAD = 128
EV_R0 = RET_COLS
EV_WA0 = EV_R0 + 3 * MIX_B
EV_G0 = EV_WA0 + 2 * RWKV_WA_PAD
EVEN_COLS = EV_G0 + RWKV_G_PAD


def _rwkv_kernel(has_vres, *refs):
    if has_vres:
        (r_ref, k_ref, v_ref, wa_ref, gd_ref, vf_ref, mu_ref, muwa_ref, mug_ref, w2_ref, a2_ref, g2_ref,
         vec_ref, v1_ref, v2_ref, o_ref, h_ref, prev_ref, prevwa_ref, prevg_ref) = refs
    else:
        (r_ref, k_ref, v_ref, wa_ref, gd_ref, mu_ref, muwa_ref, mug_ref, w2_ref, a2_ref, g2_ref,
         vec_ref, o_ref, vf_ref, h_ref, prev_ref, prevwa_ref, prevg_ref) = refs
    C = WKV_CHUNK
    P2 = WKV_PAIR

    @pl.when(pl.program_id(1) == 0)
    def _():
        h_ref[...] = jnp.zeros_like(h_ref)
        prev_ref[...] = jnp.zeros_like(prev_ref)
        prevwa_ref[...] = jnp.zeros_like(prevwa_ref)
        prevg_ref[...] = jnp.zeros_like(prevg_ref)

    def shift(x, carry_ref, j, mu):
        first = lax.broadcasted_iota(jnp.int32, x.shape, 0) == 0
        prev = jnp.where(first, jnp.broadcast_to(carry_ref[j, 7:8, :], x.shape), pltpu.roll(x, 1, 0))
        carry_ref[j] = x[C - 8:, :]
        return x + (prev - x) * mu

    r_all = shift(r_ref[...], prev_ref, 0, mu_ref[0:1, :])
    k_all = shift(k_ref[...], prev_ref, 1, mu_ref[1:2, :])
    v_all = shift(v_ref[...], prev_ref, 2, mu_ref[2:3, :])
    wa = shift(wa_ref[...], prevwa_ref, 0, muwa_ref[...])
    gd = shift(gd_ref[...], prevg_ref, 0, mug_ref[...])
    w0, a0, k_k, k_a, r_k, lnx_g, lnx_b = (vec_ref[i:i + 1, :] for i in range(7))

    w_log = -_softplus(-(w0 + _dotb(jnp.tanh(wa), w2_ref[...]))) - 0.5
    lw_all = -jnp.exp(w_log)
    a_all = jax.nn.sigmoid(a0 + _dotb(wa, a2_ref[...]))
    gate_all = _dotb(jax.nn.sigmoid(gd), g2_ref[...])
    if has_vres:
        v0 = vec_ref[7:8, :]
        mix = jax.nn.sigmoid(v0 + _dotb(_dotb(v_all, v1_ref[...]), v2_ref[...]))
        v_all = v_all + (vf_ref[...] - v_all) * mix
    else:
        vf_ref[...] = v_all
    kkraw_all = k_all * k_k
    k_all = k_all * (1.0 + (a_all - 1.0) * k_a)

    row = lax.broadcasted_iota(jnp.int32, (P2, P2), 0)
    col = lax.broadcasted_iota(jnp.int32, (P2, P2), 1)
    same_head = (row >= C) == (col >= RWKV_HEAD_DIM)
    head_ones = jnp.where((row >= RWKV_HEAD_DIM) == (col >= RWKV_HEAD_DIM), 1.0, 0.0).astype(bf16)
    t_idx = jnp.bitwise_and(row, C - 1)
    s_idx = jnp.bitwise_and(col, C - 1)
    strict_lower = t_idx > s_idx
    incl_lower = t_idx >= s_idx
    eye = row == col
    tri = (lax.broadcasted_iota(jnp.int32, (C, C), 0) >= lax.broadcasted_iota(jnp.int32, (C, C), 1)).astype(bf16)
    zeros_pp = jnp.zeros((P2, P2), f32)
    nt = (((1,), (1,)), ((), ()))
    pairs = range(WKV_PAIRS)
    sl = [slice(p * P2, (p + 1) * P2) for p in pairs]

    def bd(x):
        return jnp.where(same_head, jnp.concatenate([x, x], axis=0), 0.0)

    kk = []
    for p in pairs:
        kr = kkraw_all[:, sl[p]]
        ss = _split_dot(kr * kr, head_ones)
        kk.append(kr / jnp.maximum(jnp.sqrt(ss), 1e-12))
    lw = [lw_all[:, sl[p]] for p in pairs]
    cum = []
    for p in pairs:
        hi = lw[p].astype(bf16)
        r1 = lw[p] - hi.astype(f32)
        mid = r1.astype(bf16)
        lo = (r1 - mid.astype(f32)).astype(bf16)
        cum.append(jnp.dot(tri, hi, preferred_element_type=f32) + jnp.dot(tri, mid, preferred_element_type=f32)
                   + jnp.dot(tri, lo, preferred_element_type=f32))
    tot = [cum[p][C - 1:C, :] for p in pairs]
    w_incl = [jnp.exp(cum[p]) for p in pairs]
    w_excl = [jnp.exp(cum[p] - lw[p]) for p in pairs]
    w_inv = [jnp.exp(-cum[p]) for p in pairs]
    w_rest = [jnp.exp(tot[p] - cum[p]) for p in pairs]
    r_p = [r_all[:, sl[p]] for p in pairs]
    k_p = [k_all[:, sl[p]] for p in pairs]
    v_p = [v_all[:, sl[p]] for p in pairs]
    a_p = [a_all[:, sl[p]] for p in pairs]
    at = [bd(-kk[p] * w_excl[p]) for p in pairs]
    rt = [bd(r_p[p] * w_incl[p]) for p in pairs]
    bt = [bd(kk[p] * a_p[p] * w_inv[p]) for p in pairs]
    kt = [bd(k_p[p] * w_inv[p]) for p in pairs]
    bh = [bd(kk[p] * a_p[p] * w_rest[p]) for p in pairs]
    kh = [bd(k_p[p] * w_rest[p]) for p in pairs]
    vb = [bd(v_p[p]) for p in pairs]

    scores = [lax.dot_general(jnp.concatenate([at[p], rt[p]], axis=0).astype(bf16),
                              jnp.concatenate([bt[p], kt[p]], axis=0).astype(bf16),
                              nt, preferred_element_type=f32) for p in pairs]
    a_ab = [jnp.where(strict_lower, scores[p][:P2, :P2], 0.0) for p in pairs]
    a_ak = [jnp.where(strict_lower, scores[p][:P2, P2:], 0.0) for p in pairs]
    a_rb = [jnp.where(incl_lower, scores[p][P2:, :P2], 0.0) for p in pairs]
    a_rk = [jnp.where(incl_lower, scores[p][P2:, P2:], 0.0) for p in pairs]
    x = [jnp.concatenate([at[p], _dotb(a_ak[p], vb[p])], axis=1) for p in pairs]
    pw = a_ab
    x = [x[p] + _dotb(pw[p], x[p]) for p in pairs]
    for _ in range(int(np.log2(C)) - 1):
        pw = [_dotb(pw[p], pw[p]) for p in pairs]
        x = [x[p] + _dotb(pw[p], x[p]) for p in pairs]
    rhs = [jnp.concatenate([x[p], jnp.concatenate([zeros_pp, vb[p]], axis=1)], axis=0) for p in pairs]
    gz = [_dotb(jnp.concatenate([a_rb[p], a_rk[p]], axis=1), rhs[p]) for p in pairs]
    mn = [_dotb(jnp.concatenate([bh[p], kh[p]], axis=0).T, rhs[p]) for p in pairs]
    yh = []
    for p in pairs:
        g = rt[p] + gz[p][:, :P2]
        m = mn[p][:, :P2] + jnp.where(eye, jnp.broadcast_to(jnp.exp(tot[p]), (P2, P2)), 0.0)
        yh.append(_dotb(jnp.concatenate([g, m], axis=0), h_ref[p]))
    for p in pairs:
        y2 = yh[p][:P2] + gz[p][:, P2:]
        h_ref[p] = yh[p][P2:] + mn[p][:, P2:]
        y = y2[:C] + y2[C:]
        mean = _split_dot(y, head_ones) * (1.0 / RWKV_HEAD_DIM)
        yc = y - mean
        var = _split_dot(yc * yc, head_ones) * (1.0 / RWKV_HEAD_DIM)
        yn = yc * lax.rsqrt(var + RWKV_GN_EPS) * lnx_g[:, sl[p]] + lnx_b[:, sl[p]]
        bonus = _split_dot(r_p[p] * k_p[p] * r_k[:, sl[p]], head_ones) * v_p[p]
        o_ref[:, sl[p]] = ((yn + bonus) * gate_all[:, sl[p]]).astype(o_ref.dtype)


def rwkv_core(p, v_first, vres, mu, w0, w2, a0, a2, g2, k_k, k_a, r_k, lnx_g, lnx_b, B, S):
    C = WKV_CHUNK
    n = S // C
    has_vres = vres is not None
    r_mu, k_mu, v_mu, wd_mu, ad_mu, gd_mu = jnp.split(mu, RWKV_SPLITS)
    mu3 = jnp.stack([r_mu, k_mu, v_mu])
    mu_wa = jnp.concatenate([wd_mu, ad_mu]).reshape(1, RWKV_WA_PAD)
    mu_g = jnp.pad(gd_mu, (0, RWKV_G_PAD - RWKV_LORA_G)).reshape(1, RWKV_G_PAD)
    w2p = jnp.pad(w2, ((0, RWKV_WA_PAD - RWKV_LORA_W), (0, 0))).astype(bf16)
    a2p = jnp.pad(a2, ((RWKV_LORA_W, 0), (0, 0))).astype(bf16)
    g2p = jnp.pad(g2, ((0, RWKV_G_PAD - RWKV_LORA_G), (0, 0))).astype(bf16)
    vecs = [w0, a0, k_k, k_a, r_k.reshape(MIX_B), lnx_g, lnx_b]
    if has_vres:
        vecs.append(vres[0])
    vec = jnp.stack(vecs)
    nv = vec.shape[0]
    row = lambda bi, ci: bi * n + ci
    const = lambda bi, ci: (0, 0)
    tile = lambda w, j: pl.BlockSpec((C, w), lambda bi, ci: (row(bi, ci), j))
    in_specs = [tile(MIX_B, EV_R0 // MIX_B), tile(MIX_B, EV_R0 // MIX_B + 1), tile(MIX_B, EV_R0 // MIX_B + 2),
                tile(RWKV_WA_PAD, EV_WA0 // RWKV_WA_PAD), tile(RWKV_G_PAD, EV_G0 // RWKV_G_PAD)]
    args = [p, p, p, p, p]
    if has_vres:
        in_specs.append(tile(MIX_B, 0))
        args.append(v_first)
    in_specs += [pl.BlockSpec((3, MIX_B), const), pl.BlockSpec((1, RWKV_WA_PAD), const),
                 pl.BlockSpec((1, RWKV_G_PAD), const), pl.BlockSpec((RWKV_WA_PAD, MIX_B), const),
                 pl.BlockSpec((RWKV_WA_PAD, MIX_B), const), pl.BlockSpec((RWKV_G_PAD, MIX_B), const),
                 pl.BlockSpec((nv, MIX_B), const)]
    args += [mu3, mu_wa, mu_g, w2p, a2p, g2p, vec]
    out_specs = [tile(MIX_B, 0)]
    out_shape = [jax.ShapeDtypeStruct((B * S, MIX_B), bf16)]
    if has_vres:
        v1p = jnp.pad(vres[1], ((0, 0), (0, RWKV_V_PAD - vres[1].shape[1]))).astype(bf16)
        v2p = jnp.pad(vres[2], ((0, RWKV_V_PAD - vres[2].shape[0]), (0, 0))).astype(bf16)
        in_specs += [pl.BlockSpec((MIX_B, RWKV_V_PAD), const), pl.BlockSpec((RWKV_V_PAD, MIX_B), const)]
        args += [v1p, v2p]
    else:
        out_specs.append(tile(MIX_B, 0))
        out_shape.append(jax.ShapeDtypeStruct((B * S, MIX_B), f32))
    outs = pl.pallas_call(
        functools.partial(_rwkv_kernel, has_vres),
        grid=(B, n),
        in_specs=in_specs,
        out_specs=out_specs,
        out_shape=out_shape,
        scratch_shapes=[pltpu.VMEM((WKV_PAIRS, WKV_PAIR, WKV_PAIR), f32),
                        pltpu.VMEM((3, 8, MIX_B), f32), pltpu.VMEM((1, 8, RWKV_WA_PAD), f32),
                        pltpu.VMEM((1, 8, RWKV_G_PAD), f32)],
        compiler_params=pltpu.CompilerParams(
            dimension_semantics=("parallel", "arbitrary"), vmem_limit_bytes=VMEM_LIMIT_BYTES),
        name="rwkv7_time_mix",
    )(*args)
    return (outs[0], v_first) if has_vres else (outs[0], outs[1])


def even_win_layout(win):
    ret, rw = win[:, :RET_COLS], win[:, RET_COLS:]
    rkv = rw[:, :RWKV_SPLITS[2]]
    wd, ad, gd = rw[:, RWKV_SPLITS[2]:RWKV_SPLITS[3]], rw[:, RWKV_SPLITS[3]:RWKV_SPLITS[4]], rw[:, RWKV_SPLITS[4]:]
    z = lambda n: jnp.zeros((win.shape[0], n), win.dtype)
    return jnp.concatenate([ret, rkv, wd, ad, z(RWKV_WA_PAD), gd, z(RWKV_G_PAD - RWKV_LORA_G)], axis=1)


RET_LOG_GAMMA = tuple(float(np.log1p(-np.exp2(-5.0 - h))) for h in range(RET_HEADS))
ROPE_HALF = RET_HEAD_DIM // 2


def _ret_kernel(q_ref, k_ref, v_ref, g_ref, pos_ref, inv_ref, gng_ref, gnb_ref, o_ref, s_ref):
    C = RET_CHUNK
    d = RET_HEAD_DIM

    @pl.when(pl.program_id(1) == 0)
    def _():
        s_ref[...] = jnp.zeros_like(s_ref)

    ang = pos_ref[...] * inv_ref[...]
    cos = jnp.cos(ang)
    sin = jnp.sin(ang)

    def rot(x):
        x1, x2 = x[:, :ROPE_HALF], x[:, ROPE_HALF:]
        return jnp.concatenate([x1 * cos - x2 * sin, x1 * sin + x2 * cos], axis=-1)

    ri = lax.broadcasted_iota(jnp.int32, (C, C), 0)
    ci = lax.broadcasted_iota(jnp.int32, (C, C), 1)
    diff = (ri - ci).astype(f32)
    row = lax.broadcasted_iota(jnp.int32, (C, d), 0).astype(f32)
    nt = (((1,), (1,)), ((), ()))

    for h in range(RET_HEADS):
        hs = slice(h * d, (h + 1) * d)
        lg = RET_LOG_GAMMA[h]
        q = rot(q_ref[:, hs])
        k = rot(k_ref[:, hs]) * (d ** -0.5)
        v = v_ref[:, hs].astype(bf16)
        decay = jnp.where(diff >= 0, jnp.exp(lg * jnp.maximum(diff, 0.0)), 0.0)
        scores = lax.dot_general(q.astype(bf16), k.astype(bf16), nt, preferred_element_type=f32) * decay
        state = s_ref[h]
        y = jnp.dot(scores.astype(bf16), v, preferred_element_type=f32)
        y = y + jnp.dot((q * jnp.exp(lg * (row + 1.0))).astype(bf16), state.astype(bf16), preferred_element_type=f32)
        kw = (k * jnp.exp(lg * (C - 1.0 - row))).astype(bf16)
        s_ref[h] = state * float(np.exp(lg * C)) + jnp.dot(kw.T, v, preferred_element_type=f32)
        mu = jnp.mean(y, axis=-1, keepdims=True)
        yc = y - mu
        var = jnp.mean(yc * yc, axis=-1, keepdims=True)
        yn = yc * lax.rsqrt(var + LN_EPS) * gng_ref[:, hs] + gnb_ref[:, hs]
        g = g_ref[:, hs]
        o_ref[:, hs] = (g * jax.nn.sigmoid(g) * yn).astype(o_ref.dtype)


def retention_core(p, positions, gn_g, gn_b, B, S):
    C = RET_CHUNK
    n = S // C
    inv = (ROPE_BASE ** (-jnp.arange(ROPE_HALF, dtype=f32) / ROPE_HALF)).reshape(1, ROPE_HALF)
    pos = positions.astype(f32).reshape(B * S, 1)
    row = lambda bi, ci: bi * n + ci
    const = lambda bi, ci: (0, 0)
    col = lambda j: pl.BlockSpec((C, MIX_A), lambda bi, ci: (row(bi, ci), j))
    return pl.pallas_call(
        _ret_kernel,
        grid=(B, n),
        in_specs=[col(0), col(1), col(2), col(3),
                  pl.BlockSpec((C, 1), lambda bi, ci: (row(bi, ci), 0)),
                  pl.BlockSpec((1, ROPE_HALF), const),
                  pl.BlockSpec((1, MIX_A), const),
                  pl.BlockSpec((1, MIX_A), const)],
        out_specs=pl.BlockSpec((C, MIX_A), lambda bi, ci: (row(bi, ci), 0)),
        out_shape=jax.ShapeDtypeStruct((B * S, MIX_A), bf16),
        scratch_shapes=[pltpu.VMEM((RET_HEADS, RET_HEAD_DIM, RET_HEAD_DIM), f32)],
        compiler_params=pltpu.CompilerParams(
            dimension_semantics=("parallel", "arbitrary"), vmem_limit_bytes=VMEM_LIMIT_BYTES),
        name="retention_chunked",
    )(p, p, p, p, pos, inv, gn_g.reshape(1, MIX_A), gn_b.reshape(1, MIX_A))


GLA_LEVELS = int(np.log2(GLA_CHUNK))
GLA_DECAY_ROWS = (2 + GLA_LEVELS) * GLA_CHUNK
GLA_AD_PAD = 128
ODD_PROJ_PAD = 2 * GLA_DK + 2 * GLA_DV + GLA_AD_PAD


def _gla_decay_operator():
    C = GLA_CHUNK
    i = np.arange(C)[:, None]
    j = np.arange(C)[None, :]
    blocks = [(j <= i), (j > i)]
    for lvl in range(GLA_LEVELS):
        s = C >> (lvl + 1)
        mid = (i // (2 * s)) * (2 * s) + s
        right = i >= mid
        blocks.append(np.where(right, (j >= mid) & (j <= i), (j > i) & (j < mid)))
    return np.concatenate(blocks, 0).astype(np.float32)


def _gla_kernel(q_ref, k_ref, v_ref, r_ref, ad_ref, a2_ref, ab_ref, g_ref, lop_ref, o_ref, s_ref):
    C = GLA_CHUNK

    @pl.when(pl.program_id(1) == 0)
    def _():
        s_ref[...] = jnp.zeros_like(s_ref)

    x = jnp.dot(ad_ref[...].astype(bf16), a2_ref[...], preferred_element_type=f32) + ab_ref[...]
    la = (jnp.minimum(x, 0.0) - jnp.log1p(jnp.exp(-jnp.abs(x)))) * (1.0 / GLA_GATE_NORM)
    hi = la.astype(bf16)
    r1 = la - hi.astype(f32)
    mid = r1.astype(bf16)
    lo = (r1 - mid.astype(f32)).astype(bf16)
    lop = lop_ref[...]
    dec = (jnp.dot(lop, hi, preferred_element_type=f32) + jnp.dot(lop, mid, preferred_element_type=f32)
           + jnp.dot(lop, lo, preferred_element_type=f32))

    ri = lax.broadcasted_iota(jnp.int32, (C, C), 0)
    ci = lax.broadcasted_iota(jnp.int32, (C, C), 1)
    level_masks = []
    for lvl in range(GLA_LEVELS):
        s = C >> (lvl + 1)
        same_block = (ri // (2 * s)) == (ci // (2 * s))
        level_masks.append(same_block & ((ri % (2 * s)) >= s) & ((ci % (2 * s)) < s))
    eye = ri == ci
    nt = (((1,), (1,)), ((), ()))

    for h in range(GLA_HEADS):
        ks = slice(h * GLA_HEAD_DK, (h + 1) * GLA_HEAD_DK)
        vs = slice(h * GLA_HEAD_DV, (h + 1) * GLA_HEAD_DV)
        q = q_ref[:, ks] * (GLA_HEAD_DK ** -0.5)
        k = k_ref[:, ks]
        v = v_ref[:, vs].astype(bf16)
        e = jnp.exp(dec[:, ks])
        state = s_ref[h]
        o = lax.dot_general((q * e[:C]).astype(bf16), state.astype(bf16), nt, preferred_element_type=f32)
        attn = jnp.where(eye, lax.dot_general(q.astype(bf16), k.astype(bf16), nt, preferred_element_type=f32), 0.0)
        for lvl in range(GLA_LEVELS):
            el = e[(2 + lvl) * C:(3 + lvl) * C]
            sc = lax.dot_general((q * el).astype(bf16), (k * el).astype(bf16), nt, preferred_element_type=f32)
            attn = attn + jnp.where(level_masks[lvl], sc, 0.0)
        o = o + jnp.dot(attn.astype(bf16), v, preferred_element_type=f32)
        kd = (k * e[C:2 * C]).astype(bf16)
        s_ref[h] = state * e[C - 1:C] + jnp.dot(v.T, kd, preferred_element_type=f32)
        o = o * lax.rsqrt(jnp.mean(o * o, axis=-1, keepdims=True) + LN_EPS) * g_ref[:, vs]
        rr = r_ref[:, vs]
        o_ref[:, vs] = (rr * jax.nn.sigmoid(rr) * o).astype(o_ref.dtype)


def gla_core(p, a2, ab, norm_g, B, S):
    C = GLA_CHUNK
    n = S // C
    a2p = jnp.pad(a2, ((0, GLA_AD_PAD - GLA_LORA), (0, 0))).astype(bf16)
    lop = jnp.asarray(_gla_decay_operator(), dtype=bf16)
    row = lambda bi, ci: bi * n + ci
    const = lambda bi, ci: (0, 0)
    return pl.pallas_call(
        _gla_kernel,
        grid=(B, n),
        in_specs=[pl.BlockSpec((C, GLA_DK), lambda bi, ci: (row(bi, ci), 0)),
                  pl.BlockSpec((C, GLA_DK), lambda bi, ci: (row(bi, ci), 1)),
                  pl.BlockSpec((C, GLA_DV), lambda bi, ci: (row(bi, ci), 1)),
                  pl.BlockSpec((C, GLA_DV), lambda bi, ci: (row(bi, ci), 2)),
                  pl.BlockSpec((C, GLA_AD_PAD), lambda bi, ci: (row(bi, ci), (2 * GLA_DK + 2 * GLA_DV) // GLA_AD_PAD)),
                  pl.BlockSpec((GLA_AD_PAD, GLA_DK), const),
                  pl.BlockSpec((1, GLA_DK), const),
                  pl.BlockSpec((1, GLA_DV), const),
                  pl.BlockSpec((GLA_DECAY_ROWS, C), const)],
        out_specs=pl.BlockSpec((C, GLA_DV), lambda bi, ci: (row(bi, ci), 0)),
        out_shape=jax.ShapeDtypeStruct((B * S, GLA_DV), bf16),
        scratch_shapes=[pltpu.VMEM((GLA_HEADS, GLA_HEAD_DV, GLA_HEAD_DK), f32)],
        compiler_params=pltpu.CompilerParams(
            dimension_semantics=("parallel", "arbitrary"), vmem_limit_bytes=VMEM_LIMIT_BYTES),
        name="gla_chunked",
    )(p, p, p, p, p, a2p, ab.reshape(1, GLA_DK), norm_g.reshape(1, GLA_DV), lop)


PROJ_TM = 256
ROUTER_PAD = 128


def _proj_norm_kernel(n_in, *refs):
    a_refs = refs[:n_in]
    w_ref, x_ref, mod_ref, g_ref, b_ref, wr_ref, br_ref, xo_ref, ho_ref, lo_ref = refs[n_in:]
    acc = None
    k0 = 0
    for a_ref in a_refs:
        kw = a_ref.shape[1]
        part = jnp.dot(a_ref[...], w_ref[k0:k0 + kw, :], preferred_element_type=f32)
        acc = part if acc is None else acc + part
        k0 += kw
    gt, sc, sh = mod_ref[0, 0:1, :], mod_ref[0, 1:2, :], mod_ref[0, 2:3, :]
    z = DEEPNORM_ALPHA * x_ref[...] + (1.0 + gt) * acc
    mu = jnp.mean(z, axis=-1, keepdims=True)
    zc = z - mu
    var = jnp.mean(zc * zc, axis=-1, keepdims=True)
    xn = zc * lax.rsqrt(var + LN_EPS) * g_ref[...] + b_ref[...]
    xo_ref[...] = xn
    hf = xn * (1.0 + sc) + sh
    ho_ref[...] = hf
    lo_ref[...] = jnp.dot(hf.astype(bf16), wr_ref[...], preferred_element_type=f32) + br_ref[...]


def proj_norm(acts, wo, x, mod, ln_g, ln_b, w_router, b_router, S):
    T, D = x.shape
    tiles_per_batch = S // PROJ_TM
    const = lambda i: (0, 0)
    rowblk = lambda w: pl.BlockSpec((PROJ_TM, w), lambda i: (i, 0))
    in_specs = [rowblk(a.shape[1]) for a in acts]
    in_specs += [pl.BlockSpec(wo.shape, const), rowblk(D),
                 pl.BlockSpec((1, 3, D), lambda i: (i // tiles_per_batch, 0, 0)),
                 pl.BlockSpec((1, D), const), pl.BlockSpec((1, D), const),
                 pl.BlockSpec((D, ROUTER_PAD), const), pl.BlockSpec((1, ROUTER_PAD), const)]
    return pl.pallas_call(
        functools.partial(_proj_norm_kernel, len(acts)),
        grid=(T // PROJ_TM,),
        in_specs=in_specs,
        out_specs=[rowblk(D), rowblk(D), rowblk(ROUTER_PAD)],
        out_shape=[jax.ShapeDtypeStruct((T, D), f32), jax.ShapeDtypeStruct((T, D), f32),
                   jax.ShapeDtypeStruct((T, ROUTER_PAD), f32)],
        compiler_params=pltpu.CompilerParams(
            dimension_semantics=("parallel",), vmem_limit_bytes=VMEM_LIMIT_BYTES),
        name="proj_norm",
    )(*acts, wo.astype(bf16), x, mod, ln_g.reshape(1, D), ln_b.reshape(1, D), w_router, b_router)


MOE_ROWS = 256
MOE_VMEM_LIMIT_BYTES = 56 * 1024 * 1024
COMBINE_ROWS = 256


def _moe_ffn_kernel(tok_ref, be_ref, nused_ref, x_hbm, w1_ref, w3_ref, w2_ref, gate_ref, y_ref, xbuf, sem):
    i = pl.program_id(0)
    n_used = nused_ref[0]
    slot = lax.rem(i, 2)

    def start_gather(blk, s):
        def body(r, carry):
            t = tok_ref[blk * MOE_ROWS + r]
            pltpu.make_async_copy(x_hbm.at[pl.ds(t, 1)], xbuf.at[s, pl.ds(r, 1)], sem.at[s]).start()
            return carry
        lax.fori_loop(0, MOE_ROWS, body, 0, unroll=8)

    @pl.when((i == 0) & (n_used > 0))
    def _():
        start_gather(0, 0)

    @pl.when(i + 1 < n_used)
    def _():
        start_gather(i + 1, 1 - slot)

    @pl.when(i < n_used)
    def _():
        pltpu.make_async_copy(x_hbm.at[pl.ds(0, MOE_ROWS)], xbuf.at[slot], sem.at[slot]).wait()
        xb = xbuf[slot].astype(bf16)
        a = jnp.dot(xb, w1_ref[0].astype(bf16), preferred_element_type=f32)
        b = jnp.dot(xb, w3_ref[0].astype(bf16), preferred_element_type=f32)
        hid = (a * jax.nn.sigmoid(a) * b).astype(bf16)
        y_ref[...] = jnp.dot(hid, w2_ref[0].astype(bf16), preferred_element_type=f32) * gate_ref[...]

    @pl.when(i >= n_used)
    def _():
        y_ref[...] = jnp.zeros_like(y_ref)


def moe_ffn(x, buf_tok, blk_expert, n_used, gate_rows, w1, w3, w2):
    D = x.shape[1]
    nblk = blk_expert.shape[0]
    wspec = lambda shape: pl.BlockSpec((1,) + shape, lambda i, tok, be, nu: (be[i], 0, 0))
    return pl.pallas_call(
        _moe_ffn_kernel,
        grid_spec=pltpu.PrefetchScalarGridSpec(
            num_scalar_prefetch=3,
            grid=(nblk,),
            in_specs=[pl.BlockSpec(memory_space=pl.ANY),
                      wspec((D, EXPERT_FF)), wspec((D, EXPERT_FF)), wspec((EXPERT_FF, D)),
                      pl.BlockSpec((MOE_ROWS, 1), lambda i, tok, be, nu: (i, 0))],
            out_specs=pl.BlockSpec((MOE_ROWS, D), lambda i, tok, be, nu: (i, 0)),
            scratch_shapes=[pltpu.VMEM((2, MOE_ROWS, D), f32), pltpu.SemaphoreType.DMA((2,))]),
        out_shape=jax.ShapeDtypeStruct((nblk * MOE_ROWS, D), f32),
        compiler_params=pltpu.CompilerParams(
            dimension_semantics=("arbitrary",), vmem_limit_bytes=MOE_VMEM_LIMIT_BYTES),
        name="moe_expert_ffn",
    )(buf_tok, blk_expert, n_used, x, w1, w3, w2, gate_rows)


def _moe_combine_kernel(emit_h, s0_ref, s1_ref, y_hbm, x_ref, mod_ref, g_ref, b_ref, *rest):
    if emit_h:
        o_ref, h_ref, buf, sem = rest
    else:
        o_ref, buf, sem = rest
    i = pl.program_id(0)
    n = pl.num_programs(0)
    slot = lax.rem(i, 2)

    def start_gather(blk, s):
        def body(r, carry):
            t = blk * COMBINE_ROWS + r
            pltpu.make_async_copy(y_hbm.at[pl.ds(s0_ref[t], 1)], buf.at[s, 0, pl.ds(r, 1)], sem.at[s]).start()
            pltpu.make_async_copy(y_hbm.at[pl.ds(s1_ref[t], 1)], buf.at[s, 1, pl.ds(r, 1)], sem.at[s]).start()
            return carry
        lax.fori_loop(0, COMBINE_ROWS, body, 0, unroll=8)

    @pl.when(i == 0)
    def _():
        start_gather(0, 0)

    @pl.when(i + 1 < n)
    def _():
        start_gather(i + 1, 1 - slot)

    for c in range(MOE_TOPK):
        pltpu.make_async_copy(y_hbm.at[pl.ds(0, COMBINE_ROWS)], buf.at[slot, c], sem.at[slot]).wait()
    y = buf[slot, 0] + buf[slot, 1]
    z = DEEPNORM_ALPHA * x_ref[...] + (1.0 + mod_ref[0, 0:1, :]) * y
    mu = jnp.mean(z, axis=-1, keepdims=True)
    zc = z - mu
    var = jnp.mean(zc * zc, axis=-1, keepdims=True)
    xn = zc * lax.rsqrt(var + LN_EPS) * g_ref[...] + b_ref[...]
    o_ref[...] = xn
    if emit_h:
        h_ref[...] = (xn * (1.0 + mod_ref[0, 1:2, :]) + mod_ref[0, 2:3, :]).astype(h_ref.dtype)


def moe_combine_norm(y_rows, slot0, slot1, x, mod, ln_g, ln_b, S, emit_h):
    T, D = x.shape
    tiles_per_batch = S // COMBINE_ROWS
    const = lambda i, s0, s1: (0, 0)
    rowblk = pl.BlockSpec((COMBINE_ROWS, D), lambda i, s0, s1: (i, 0))
    out_specs = [rowblk, rowblk] if emit_h else [rowblk]
    out_shape = [jax.ShapeDtypeStruct((T, D), f32)] + ([jax.ShapeDtypeStruct((T, D), bf16)] if emit_h else [])
    return pl.pallas_call(
        functools.partial(_moe_combine_kernel, emit_h),
        grid_spec=pltpu.PrefetchScalarGridSpec(
            num_scalar_prefetch=2,
            grid=(T // COMBINE_ROWS,),
            in_specs=[pl.BlockSpec(memory_space=pl.ANY), rowblk,
                      pl.BlockSpec((1, 3, D), lambda i, s0, s1: (i // tiles_per_batch, 0, 0)),
                      pl.BlockSpec((1, D), const), pl.BlockSpec((1, D), const)],
            out_specs=out_specs,
            scratch_shapes=[pltpu.VMEM((2, MOE_TOPK, COMBINE_ROWS, D), f32), pltpu.SemaphoreType.DMA((2,))]),
        out_shape=out_shape,
        compiler_params=pltpu.CompilerParams(
            dimension_semantics=("arbitrary",), vmem_limit_bytes=VMEM_LIMIT_BYTES),
        name="moe_combine_norm",
    )(slot0, slot1, y_rows, x, mod, ln_g.reshape(1, D), ln_b.reshape(1, D))


def router_params(wg, bg, we, be):
    D = wg.shape[0]
    pad = ROUTER_PAD - MOE_GROUPS - N_EXPERTS
    w = jnp.concatenate([wg, we, jnp.zeros((D, pad), wg.dtype)], 1).astype(bf16)
    b = jnp.concatenate([bg, be, jnp.zeros((pad,), bg.dtype)]).reshape(1, ROUTER_PAD)
    return w, b


def moe_route(logits):
    T = logits.shape[0]
    g_prob = jax.nn.softmax(logits[:, :MOE_GROUPS], -1)
    p_top, g_top = lax.top_k(g_prob, 1)
    e_logits = logits[:, MOE_GROUPS:MOE_GROUPS + N_EXPERTS].reshape(T, MOE_GROUPS, MOE_PER_GROUP)
    e_sel = jnp.take_along_axis(e_logits, g_top[:, :, None], axis=1)[:, 0]
    e_val, e_top = lax.top_k(e_sel, MOE_TOPK)
    gate = p_top * jax.nn.softmax(e_val, -1)
    expert = g_top * MOE_PER_GROUP + e_top
    return gate, expert


def moe_layout(gate, expert):
    T = gate.shape[0]
    A = T * MOE_TOPK
    e_flat = expert.reshape(A)
    order = jnp.argsort(e_flat)
    e_s = e_flat[order]
    counts = jnp.bincount(e_flat, length=N_EXPERTS)
    padded = (counts + MOE_ROWS - 1) // MOE_ROWS * MOE_ROWS
    pad_end = jnp.cumsum(padded)
    pad_start = pad_end - padded
    start = jnp.cumsum(counts) - counts
    slot_sorted = (pad_start[e_s] + jnp.arange(A, dtype=jnp.int32) - start[e_s]).astype(jnp.int32)
    slot = jnp.zeros((A,), jnp.int32).at[order].set(slot_sorted)
    tok_flat = jnp.arange(A, dtype=jnp.int32) // MOE_TOPK
    nblk = A // MOE_ROWS + N_EXPERTS
    rows = nblk * MOE_ROWS
    buf_tok = jnp.zeros((rows,), jnp.int32).at[slot].set(tok_flat)
    gate_rows = jnp.zeros((rows,), f32).at[slot].set(gate.reshape(A))
    n_used = (pad_end[-1] // MOE_ROWS).astype(jnp.int32)
    blk = jnp.arange(nblk, dtype=jnp.int32)
    blk_expert = jnp.searchsorted(pad_end, jnp.minimum(blk, n_used - 1) * MOE_ROWS, side='right').astype(jnp.int32)
    blk_expert = jnp.minimum(blk_expert, N_EXPERTS - 1)
    slot2 = slot.reshape(T, MOE_TOPK)
    return buf_tok, blk_expert, n_used.reshape(1), gate_rows.reshape(rows, 1), slot2[:, 0], slot2[:, 1]


def kernel(x, c, positions, ada_w, ada_b, ln_g, ln_b, ev_win, ev_wo, ret_gn_g, ret_gn_b, rw_mu, rw_w0, rw_w2, rw_a0, rw_a2, rw_g2, rw_kk, rw_ka, rw_rk, rw_lnx_g, rw_lnx_b, rw_v0, rw_v1, rw_v2, od_win, od_wo, gla_a2, gla_ab, gla_norm_g, moe_wg, moe_bg, moe_we, moe_be, moe_w1, moe_w3, moe_w2):
    B, S, D = x.shape
    T = B * S
    depth = ada_w.shape[0]
    mods = ada_mod(c, ada_w, ada_b).reshape(depth, B, 6, D)
    h = (x * (1.0 + mods[0, :, 1])[:, None, :] + mods[0, :, 0][:, None, :]).astype(bf16).reshape(T, D)
    x = x.reshape(T, D)
    v_first = None
    for layer in range(depth):
        j = layer // 2
        mixer_mod = jnp.stack([mods[layer, :, 2], mods[layer, :, 4], mods[layer, :, 3]], axis=1)
        if layer % 2 == 0:
            p = matmul(h, even_win_layout(ev_win[j]))
            out_a = retention_core(p, positions, ret_gn_g[j], ret_gn_b[j], B, S)
            vres = None if j == 0 else (rw_v0[j - 1], rw_v1[j - 1], rw_v2[j - 1])
            out_b, v_first = rwkv_core(p, v_first, vres, rw_mu[j], rw_w0[j], rw_w2[j], rw_a0[j], rw_a2[j], rw_g2[j],
                                       rw_kk[j], rw_ka[j], rw_rk[j], rw_lnx_g[j], rw_lnx_b[j], B, S)
            acts, wo = [out_a, out_b], ev_wo[j]
        else:
            p = matmul(h, jnp.pad(od_win[j], ((0, 0), (0, ODD_PROJ_PAD - ODD_PROJ))), tn=896)
            acts, wo = [gla_core(p, gla_a2[j], gla_ab[j], gla_norm_g[j], B, S)], od_wo[j]
        w_router, b_router = router_params(moe_wg[layer], moe_bg[layer], moe_we[layer], moe_be[layer])
        x, hf, logits = proj_norm(acts, wo, x, mixer_mod, ln_g[layer, 0], ln_b[layer, 0], w_router, b_router, S)
        gate, expert = moe_route(logits)
        buf_tok, blk_expert, n_used, gate_rows, slot0, slot1 = moe_layout(gate, expert)
        y_rows = moe_ffn(hf, buf_tok, blk_expert, n_used, gate_rows, moe_w1[layer], moe_w3[layer], moe_w2[layer])
        last = layer == depth - 1
        nxt = layer if last else layer + 1
        moe_mod = jnp.stack([mods[layer, :, 5], mods[nxt, :, 1], mods[nxt, :, 0]], axis=1)
        outs = moe_combine_norm(y_rows, slot0, slot1, x, moe_mod, ln_g[layer, 1], ln_b[layer, 1], S, not last)
        x = outs[0]
        if not last:
            h = outs[1]
    return x.reshape(B, S, D)
```

```python
import functools

import jax
import jax.numpy as jnp
import numpy as np
from jax import lax
from jax.experimental import pallas as pl
from jax.experimental.pallas import tpu as pltpu

D_MODEL = 2048
DEPTH = 4
MIX_A = D_MODEL // 2
RET_HEAD_DIM = 256
RET_HEADS = MIX_A // RET_HEAD_DIM
RET_CHUNK = 128
RET_COLS = 4 * MIX_A
ROPE_BASE = 10000.0
MIX_B = D_MODEL - MIX_A
RWKV_HEAD_DIM = 64
RWKV_HEADS = MIX_B // RWKV_HEAD_DIM
RWKV_LORA_W = 64
RWKV_LORA_A = 64
RWKV_LORA_G = 160
RWKV_COLS = 3 * MIX_B + RWKV_LORA_W + RWKV_LORA_A + RWKV_LORA_G
RWKV_SPLITS = (MIX_B, 2 * MIX_B, 3 * MIX_B, 3 * MIX_B + RWKV_LORA_W, 3 * MIX_B + RWKV_LORA_W + RWKV_LORA_A)
RWKV_GN_EPS = 64e-5
GLA_HEADS = 4
GLA_DK = D_MODEL // 2
GLA_DV = D_MODEL
GLA_HEAD_DK = GLA_DK // GLA_HEADS
GLA_HEAD_DV = GLA_DV // GLA_HEADS
GLA_LORA = 16
GLA_GATE_NORM = 16.0
GLA_CHUNK = 64
ODD_PROJ = 2 * GLA_DK + 2 * GLA_DV + GLA_LORA
MOE_GROUPS = 4
MOE_PER_GROUP = 8
N_EXPERTS = MOE_GROUPS * MOE_PER_GROUP
MOE_TOPK = 2
EXPERT_FF = D_MODEL // 4
DEEPNORM_ALPHA = (2 * DEPTH) ** 0.25
LN_EPS = 1e-5

f32 = jnp.float32
bf16 = jnp.bfloat16

VMEM_LIMIT_BYTES = 48 * 1024 * 1024


def _dotb(a, b):
    return jnp.dot(a.astype(bf16), b.astype(bf16), preferred_element_type=f32)


def _split_dot(x, ones_bf16):
    hi = x.astype(bf16)
    lo = (x - hi.astype(f32)).astype(bf16)
    return jnp.dot(hi, ones_bf16, preferred_element_type=f32) + jnp.dot(lo, ones_bf16, preferred_element_type=f32)


def _softplus(z):
    return jnp.maximum(z, 0.0) + jnp.log1p(jnp.exp(-jnp.abs(z)))


ADA_ROWS = 8
ADA_TN = 1024


def _ada_kernel(c_ref, w_ref, b_ref, o_ref):
    c = c_ref[...]
    act = (c * jax.nn.sigmoid(c)).astype(bf16)
    o_ref[0] = jnp.dot(act, w_ref[0].astype(bf16), preferred_element_type=f32) + b_ref[0]


def ada_mod(c, ada_w, ada_b):
    B, D = c.shape
    depth, _, N = ada_w.shape
    c_pad = jnp.pad(c, ((0, ADA_ROWS - B), (0, 0)))
    out = pl.pallas_call(
        _ada_kernel,
        grid=(depth, N // ADA_TN),
        in_specs=[pl.BlockSpec((ADA_ROWS, D), lambda l, j: (0, 0)),
                  pl.BlockSpec((1, D, ADA_TN), lambda l, j: (l, 0, j)),
                  pl.BlockSpec((1, 1, ADA_TN), lambda l, j: (l, 0, j))],
        out_specs=pl.BlockSpec((1, ADA_ROWS, ADA_TN), lambda l, j: (l, 0, j)),
        out_shape=jax.ShapeDtypeStruct((depth, ADA_ROWS, N), f32),
        compiler_params=pltpu.CompilerParams(
            dimension_semantics=("parallel", "parallel"), vmem_limit_bytes=VMEM_LIMIT_BYTES),
        name="ada_mod",
    )(c_pad, ada_w, ada_b.reshape(depth, 1, N))
    return out[:, :B]


def _matmul_kernel(x_ref, w_ref, o_ref):
    o_ref[...] = jnp.dot(x_ref[...], w_ref[...], preferred_element_type=f32)


def matmul(x, w, tm=1024, tn=512):
    M, K = x.shape
    Np = w.shape[1]
    assert Np % tn == 0 and M % min(tm, M) == 0
    wb = w.astype(bf16)
    tm = min(tm, M)
    return pl.pallas_call(
        _matmul_kernel,
        grid=(M // tm, Np // tn),
        in_specs=[pl.BlockSpec((tm, K), lambda i, j: (i, 0)),
                  pl.BlockSpec((K, tn), lambda i, j: (0, j))],
        out_specs=pl.BlockSpec((tm, tn), lambda i, j: (i, j)),
        out_shape=jax.ShapeDtypeStruct((M, Np), f32),
        compiler_params=pltpu.CompilerParams(
            dimension_semantics=("parallel", "parallel"), vmem_limit_bytes=VMEM_LIMIT_BYTES),
        name="dense_matmul",
    )(x.astype(bf16), wb)


WKV_CHUNK = 64
WKV_PAIR = 2 * RWKV_HEAD_DIM
WKV_PAIRS = MIX_B // WKV_PAIR
RWKV_WA_PAD = 128
RWKV_G_PAD = 256
RWKV_V_PAD = 128
EV_R0 = RET_COLS
EV_WA0 = EV_R0 + 3 * MIX_B
EV_G0 = EV_WA0 + 2 * RWKV_WA_PAD
EVEN_COLS = EV_G0 + RWKV_G_PAD


def _rwkv_kernel(has_vres, *refs):
    if has_vres:
        (r_ref, k_ref, v_ref, wa_ref, gd_ref, vf_ref, mu_ref, muwa_ref, mug_ref, w2_ref, a2_ref, g2_ref,
         vec_ref, v1_ref, v2_ref, o_ref, h_ref, prev_ref, prevwa_ref, prevg_ref) = refs
    else:
        (r_ref, k_ref, v_ref, wa_ref, gd_ref, mu_ref, muwa_ref, mug_ref, w2_ref, a2_ref, g2_ref,
         vec_ref, o_ref, vf_ref, h_ref, prev_ref, prevwa_ref, prevg_ref) = refs
    C = WKV_CHUNK
    P2 = WKV_PAIR

    @pl.when(pl.program_id(1) == 0)
    def _():
        h_ref[...] = jnp.zeros_like(h_ref)
        prev_ref[...] = jnp.zeros_like(prev_ref)
        prevwa_ref[...] = jnp.zeros_like(prevwa_ref)
        prevg_ref[...] = jnp.zeros_like(prevg_ref)

    def shift(x, carry_ref, j, mu):
        first = lax.broadcasted_iota(jnp.int32, x.shape, 0) == 0
        prev = jnp.where(first, jnp.broadcast_to(carry_ref[j, 7:8, :], x.shape), pltpu.roll(x, 1, 0))
        carry_ref[j] = x[C - 8:, :]
        return x + (prev - x) * mu

    r_all = shift(r_ref[...], prev_ref, 0, mu_ref[0:1, :])
    k_all = shift(k_ref[...], prev_ref, 1, mu_ref[1:2, :])
    v_all = shift(v_ref[...], prev_ref, 2, mu_ref[2:3, :])
    wa = shift(wa_ref[...], prevwa_ref, 0, muwa_ref[...])
    gd = shift(gd_ref[...], prevg_ref, 0, mug_ref[...])
    w0, a0, k_k, k_a, r_k, lnx_g, lnx_b = (vec_ref[i:i + 1, :] for i in range(7))

    w_log = -_softplus(-(w0 + _dotb(jnp.tanh(wa), w2_ref[...]))) - 0.5
    lw_all = -jnp.exp(w_log)
    a_all = jax.nn.sigmoid(a0 + _dotb(wa, a2_ref[...]))
    gate_all = _dotb(jax.nn.sigmoid(gd), g2_ref[...])
    if has_vres:
        v0 = vec_ref[7:8, :]
        mix = jax.nn.sigmoid(v0 + _dotb(_dotb(v_all, v1_ref[...]), v2_ref[...]))
        v_all = v_all + (vf_ref[...] - v_all) * mix
    else:
        vf_ref[...] = v_all
    kkraw_all = k_all * k_k
    k_all = k_all * (1.0 + (a_all - 1.0) * k_a)

    row = lax.broadcasted_iota(jnp.int32, (P2, P2), 0)
    col = lax.broadcasted_iota(jnp.int32, (P2, P2), 1)
    same_head = (row >= C) == (col >= RWKV_HEAD_DIM)
    head_ones = jnp.where((row >= RWKV_HEAD_DIM) == (col >= RWKV_HEAD_DIM), 1.0, 0.0).astype(bf16)
    t_idx = jnp.bitwise_and(row, C - 1)
    s_idx = jnp.bitwise_and(col, C - 1)
    strict_lower = t_idx > s_idx
    incl_lower = t_idx >= s_idx
    eye = row == col
    tri = (lax.broadcasted_iota(jnp.int32, (C, C), 0) >= lax.broadcasted_iota(jnp.int32, (C, C), 1)).astype(bf16)
    zeros_pp = jnp.zeros((P2, P2), f32)
    nt = (((1,), (1,)), ((), ()))
    pairs = range(WKV_PAIRS)
    sl = [slice(p * P2, (p + 1) * P2) for p in pairs]

    def bd(x):
        return jnp.where(same_head, jnp.concatenate([x, x], axis=0), 0.0)

    kk = []
    for p in pairs:
        kr = kkraw_all[:, sl[p]]
        ss = _split_dot(kr * kr, head_ones)
        kk.append(kr / jnp.maximum(jnp.sqrt(ss), 1e-12))
    lw = [lw_all[:, sl[p]] for p in pairs]
    cum = []
    for p in pairs:
        hi = lw[p].astype(bf16)
        r1 = lw[p] - hi.astype(f32)
        mid = r1.astype(bf16)
        lo = (r1 - mid.astype(f32)).astype(bf16)
        cum.append(jnp.dot(tri, hi, preferred_element_type=f32) + jnp.dot(tri, mid, preferred_element_type=f32)
                   + jnp.dot(tri, lo, preferred_element_type=f32))
    tot = [cum[p][C - 1:C, :] for p in pairs]
    w_incl = [jnp.exp(cum[p]) for p in pairs]
    w_excl = [jnp.exp(cum[p] - lw[p]) for p in pairs]
    w_inv = [jnp.exp(-cum[p]) for p in pairs]
    w_rest = [jnp.exp(tot[p] - cum[p]) for p in pairs]
    r_p = [r_all[:, sl[p]] for p in pairs]
    k_p = [k_all[:, sl[p]] for p in pairs]
    v_p = [v_all[:, sl[p]] for p in pairs]
    a_p = [a_all[:, sl[p]] for p in pairs]
    at = [bd(-kk[p] * w_excl[p]) for p in pairs]
    rt = [bd(r_p[p] * w_incl[p]) for p in pairs]
    bt = [bd(kk[p] * a_p[p] * w_inv[p]) for p in pairs]
    kt = [bd(k_p[p] * w_inv[p]) for p in pairs]
    bh = [bd(kk[p] * a_p[p] * w_rest[p]) for p in pairs]
    kh = [bd(k_p[p] * w_rest[p]) for p in pairs]
    vb = [bd(v_p[p]) for p in pairs]

    scores = [lax.dot_general(jnp.concatenate([at[p], rt[p]], axis=0).astype(bf16),
                              jnp.concatenate([bt[p], kt[p]], axis=0).astype(bf16),
                              nt, preferred_element_type=f32) for p in pairs]
    a_ab = [jnp.where(strict_lower, scores[p][:P2, :P2], 0.0) for p in pairs]
    a_ak = [jnp.where(strict_lower, scores[p][:P2, P2:], 0.0) for p in pairs]
    a_rb = [jnp.where(incl_lower, scores[p][P2:, :P2], 0.0) for p in pairs]
    a_rk = [jnp.where(incl_lower, scores[p][P2:, P2:], 0.0) for p in pairs]
    x = [jnp.concatenate([at[p], _dotb(a_ak[p], vb[p])], axis=1) for p in pairs]
    pw = a_ab
    x = [x[p] + _dotb(pw[p], x[p]) for p in pairs]
    for _ in range(int(np.log2(C)) - 1):
        pw = [_dotb(pw[p], pw[p]) for p in pairs]
        x = [x[p] + _dotb(pw[p], x[p]) for p in pairs]
    rhs = [jnp.concatenate([x[p], jnp.concatenate([zeros_pp, vb[p]], axis=1)], axis=0) for p in pairs]
    gz = [_dotb(jnp.concatenate([a_rb[p], a_rk[p]], axis=1), rhs[p]) for p in pairs]
    mn = [_dotb(jnp.concatenate([bh[p], kh[p]], axis=0).T, rhs[p]) for p in pairs]
    yh = []
    for p in pairs:
        g = rt[p] + gz[p][:, :P2]
        m = mn[p][:, :P2] + jnp.where(eye, jnp.broadcast_to(jnp.exp(tot[p]), (P2, P2)), 0.0)
        yh.append(_dotb(jnp.concatenate([g, m], axis=0), h_ref[p]))
    for p in pairs:
        y2 = yh[p][:P2] + gz[p][:, P2:]
        h_ref[p] = yh[p][P2:] + mn[p][:, P2:]
        y = y2[:C] + y2[C:]
        mean = _split_dot(y, head_ones) * (1.0 / RWKV_HEAD_DIM)
        yc = y - mean
        var = _split_dot(yc * yc, head_ones) * (1.0 / RWKV_HEAD_DIM)
        yn = yc * lax.rsqrt(var + RWKV_GN_EPS) * lnx_g[:, sl[p]] + lnx_b[:, sl[p]]
        bonus = _split_dot(r_p[p] * k_p[p] * r_k[:, sl[p]], head_ones) * v_p[p]
        o_ref[:, sl[p]] = ((yn + bonus) * gate_all[:, sl[p]]).astype(o_ref.dtype)


def rwkv_core(p, v_first, vres, mu, w0, w2, a0, a2, g2, k_k, k_a, r_k, lnx_g, lnx_b, B, S):
    C = WKV_CHUNK
    n = S // C
    has_vres = vres is not None
    r_mu, k_mu, v_mu, wd_mu, ad_mu, gd_mu = jnp.split(mu, RWKV_SPLITS)
    mu3 = jnp.stack([r_mu, k_mu, v_mu])
    mu_wa = jnp.concatenate([wd_mu, ad_mu]).reshape(1, RWKV_WA_PAD)
    mu_g = jnp.pad(gd_mu, (0, RWKV_G_PAD - RWKV_LORA_G)).reshape(1, RWKV_G_PAD)
    w2p = jnp.pad(w2, ((0, RWKV_WA_PAD - RWKV_LORA_W), (0, 0))).astype(bf16)
    a2p = jnp.pad(a2, ((RWKV_LORA_W, 0), (0, 0))).astype(bf16)
    g2p = jnp.pad(g2, ((0, RWKV_G_PAD - RWKV_LORA_G), (0, 0))).astype(bf16)
    vecs = [w0, a0, k_k, k_a, r_k.reshape(MIX_B), lnx_g, lnx_b]
    if has_vres:
        vecs.append(vres[0])
    vec = jnp.stack(vecs)
    nv = vec.shape[0]
    row = lambda bi, ci: bi * n + ci
    const = lambda bi, ci: (0, 0)
    tile = lambda w, j: pl.BlockSpec((C, w), lambda bi, ci: (row(bi, ci), j))
    in_specs = [tile(MIX_B, EV_R0 // MIX_B), tile(MIX_B, EV_R0 // MIX_B + 1), tile(MIX_B, EV_R0 // MIX_B + 2),
                tile(RWKV_WA_PAD, EV_WA0 // RWKV_WA_PAD), tile(RWKV_G_PAD, EV_G0 // RWKV_G_PAD)]
    args = [p, p, p, p, p]
    if has_vres:
        in_specs.append(tile(MIX_B, 0))
        args.append(v_first)
    in_specs += [pl.BlockSpec((3, MIX_B), const), pl.BlockSpec((1, RWKV_WA_PAD), const),
                 pl.BlockSpec((1, RWKV_G_PAD), const), pl.BlockSpec((RWKV_WA_PAD, MIX_B), const),
                 pl.BlockSpec((RWKV_WA_PAD, MIX_B), const), pl.BlockSpec((RWKV_G_PAD, MIX_B), const),
                 pl.BlockSpec((nv, MIX_B), const)]
    args += [mu3, mu_wa, mu_g, w2p, a2p, g2p, vec]
    out_specs = [tile(MIX_B, 0)]
    out_shape = [jax.ShapeDtypeStruct((B * S, MIX_B), bf16)]
    if has_vres:
        v1p = jnp.pad(vres[1], ((0, 0), (0, RWKV_V_PAD - vres[1].shape[1]))).astype(bf16)
        v2p = jnp.pad(vres[2], ((0, RWKV_V_PAD - vres[2].shape[0]), (0, 0))).astype(bf16)
        in_specs += [pl.BlockSpec((MIX_B, RWKV_V_PAD), const), pl.BlockSpec((RWKV_V_PAD, MIX_B), const)]
        args += [v1p, v2p]
    else:
        out_specs.append(tile(MIX_B, 0))
        out_shape.append(jax.ShapeDtypeStruct((B * S, MIX_B), f32))
    outs = pl.pallas_call(
        functools.partial(_rwkv_kernel, has_vres),
        grid=(B, n),
        in_specs=in_specs,
        out_specs=out_specs,
        out_shape=out_shape,
        scratch_shapes=[pltpu.VMEM((WKV_PAIRS, WKV_PAIR, WKV_PAIR), f32),
                        pltpu.VMEM((3, 8, MIX_B), f32), pltpu.VMEM((1, 8, RWKV_WA_PAD), f32),
                        pltpu.VMEM((1, 8, RWKV_G_PAD), f32)],
        compiler_params=pltpu.CompilerParams(
            dimension_semantics=("parallel", "arbitrary"), vmem_limit_bytes=VMEM_LIMIT_BYTES),
        name="rwkv7_time_mix",
    )(*args)
    return (outs[0], v_first) if has_vres else (outs[0], outs[1])


def even_win_layout(win):
    ret, rw = win[:, :RET_COLS], win[:, RET_COLS:]
    rkv = rw[:, :RWKV_SPLITS[2]]
    wd, ad, gd = rw[:, RWKV_SPLITS[2]:RWKV_SPLITS[3]], rw[:, RWKV_SPLITS[3]:RWKV_SPLITS[4]], rw[:, RWKV_SPLITS[4]:]
    z = lambda n: jnp.zeros((win.shape[0], n), win.dtype)
    return jnp.concatenate([ret, rkv, wd, ad, z(RWKV_WA_PAD), gd, z(RWKV_G_PAD - RWKV_LORA_G)], axis=1)


RET_LOG_GAMMA = tuple(float(np.log1p(-np.exp2(-5.0 - h))) for h in range(RET_HEADS))
ROPE_HALF = RET_HEAD_DIM // 2


def _ret_kernel(q_ref, k_ref, v_ref, g_ref, pos_ref, inv_ref, gng_ref, gnb_ref, o_ref, s_ref):
    C = RET_CHUNK
    d = RET_HEAD_DIM

    @pl.when(pl.program_id(1) == 0)
    def _():
        s_ref[...] = jnp.zeros_like(s_ref)

    ang = pos_ref[...] * inv_ref[...]
    cos = jnp.cos(ang)
    sin = jnp.sin(ang)

    def rot(x):
        x1, x2 = x[:, :ROPE_HALF], x[:, ROPE_HALF:]
        return jnp.concatenate([x1 * cos - x2 * sin, x1 * sin + x2 * cos], axis=-1)

    ri = lax.broadcasted_iota(jnp.int32, (C, C), 0)
    ci = lax.broadcasted_iota(jnp.int32, (C, C), 1)
    diff = (ri - ci).astype(f32)
    row = lax.broadcasted_iota(jnp.int32, (C, d), 0).astype(f32)
    nt = (((1,), (1,)), ((), ()))

    for h in range(RET_HEADS):
        hs = slice(h * d, (h + 1) * d)
        lg = RET_LOG_GAMMA[h]
        q = rot(q_ref[:, hs])
        k = rot(k_ref[:, hs]) * (d ** -0.5)
        v = v_ref[:, hs].astype(bf16)
        decay = jnp.where(diff >= 0, jnp.exp(lg * jnp.maximum(diff, 0.0)), 0.0)
        scores = lax.dot_general(q.astype(bf16), k.astype(bf16), nt, preferred_element_type=f32) * decay
        state = s_ref[h]
        y = jnp.dot(scores.astype(bf16), v, preferred_element_type=f32)
        y = y + jnp.dot((q * jnp.exp(lg * (row + 1.0))).astype(bf16), state.astype(bf16), preferred_element_type=f32)
        kw = (k * jnp.exp(lg * (C - 1.0 - row))).astype(bf16)
        s_ref[h] = state * float(np.exp(lg * C)) + jnp.dot(kw.T, v, preferred_element_type=f32)
        mu = jnp.mean(y, axis=-1, keepdims=True)
        yc = y - mu
        var = jnp.mean(yc * yc, axis=-1, keepdims=True)
        yn = yc * lax.rsqrt(var + LN_EPS) * gng_ref[:, hs] + gnb_ref[:, hs]
        g = g_ref[:, hs]
        o_ref[:, hs] = (g * jax.nn.sigmoid(g) * yn).astype(o_ref.dtype)


def retention_core(p, positions, gn_g, gn_b, B, S):
    C = RET_CHUNK
    n = S // C
    inv = (ROPE_BASE ** (-jnp.arange(ROPE_HALF, dtype=f32) / ROPE_HALF)).reshape(1, ROPE_HALF)
    pos = positions.astype(f32).reshape(B * S, 1)
    row = lambda bi, ci: bi * n + ci
    const = lambda bi, ci: (0, 0)
    col = lambda j: pl.BlockSpec((C, MIX_A), lambda bi, ci: (row(bi, ci), j))
    return pl.pallas_call(
        _ret_kernel,
        grid=(B, n),
        in_specs=[col(0), col(1), col(2), col(3),
                  pl.BlockSpec((C, 1), lambda bi, ci: (row(bi, ci), 0)),
                  pl.BlockSpec((1, ROPE_HALF), const),
                  pl.BlockSpec((1, MIX_A), const),
                  pl.BlockSpec((1, MIX_A), const)],
        out_specs=pl.BlockSpec((C, MIX_A), lambda bi, ci: (row(bi, ci), 0)),
        out_shape=jax.ShapeDtypeStruct((B * S, MIX_A), bf16),
        scratch_shapes=[pltpu.VMEM((RET_HEADS, RET_HEAD_DIM, RET_HEAD_DIM), f32)],
        compiler_params=pltpu.CompilerParams(
            dimension_semantics=("parallel", "arbitrary"), vmem_limit_bytes=VMEM_LIMIT_BYTES),
        name="retention_chunked",
    )(p, p, p, p, pos, inv, gn_g.reshape(1, MIX_A), gn_b.reshape(1, MIX_A))


GLA_LEVELS = int(np.log2(GLA_CHUNK))
GLA_DECAY_ROWS = (2 + GLA_LEVELS) * GLA_CHUNK
GLA_AD_PAD = 128
ODD_PROJ_PAD = 2 * GLA_DK + 2 * GLA_DV + GLA_AD_PAD


def _gla_decay_operator():
    C = GLA_CHUNK
    i = np.arange(C)[:, None]
    j = np.arange(C)[None, :]
    blocks = [(j <= i), (j > i)]
    for lvl in range(GLA_LEVELS):
        s = C >> (lvl + 1)
        mid = (i // (2 * s)) * (2 * s) + s
        right = i >= mid
        blocks.append(np.where(right, (j >= mid) & (j <= i), (j > i) & (j < mid)))
    return np.concatenate(blocks, 0).astype(np.float32)


def _gla_kernel(q_ref, k_ref, v_ref, r_ref, ad_ref, a2_ref, ab_ref, g_ref, lop_ref, o_ref, s_ref):
    C = GLA_CHUNK

    @pl.when(pl.program_id(1) == 0)
    def _():
        s_ref[...] = jnp.zeros_like(s_ref)

    x = jnp.dot(ad_ref[...].astype(bf16), a2_ref[...], preferred_element_type=f32) + ab_ref[...]
    la = (jnp.minimum(x, 0.0) - jnp.log1p(jnp.exp(-jnp.abs(x)))) * (1.0 / GLA_GATE_NORM)
    hi = la.astype(bf16)
    r1 = la - hi.astype(f32)
    mid = r1.astype(bf16)
    lo = (r1 - mid.astype(f32)).astype(bf16)
    lop = lop_ref[...]
    dec = (jnp.dot(lop, hi, preferred_element_type=f32) + jnp.dot(lop, mid, preferred_element_type=f32)
           + jnp.dot(lop, lo, preferred_element_type=f32))

    ri = lax.broadcasted_iota(jnp.int32, (C, C), 0)
    ci = lax.broadcasted_iota(jnp.int32, (C, C), 1)
    level_masks = []
    for lvl in range(GLA_LEVELS):
        s = C >> (lvl + 1)
        same_block = (ri // (2 * s)) == (ci // (2 * s))
        level_masks.append(same_block & ((ri % (2 * s)) >= s) & ((ci % (2 * s)) < s))
    eye = ri == ci
    nt = (((1,), (1,)), ((), ()))

    for h in range(GLA_HEADS):
        ks = slice(h * GLA_HEAD_DK, (h + 1) * GLA_HEAD_DK)
        vs = slice(h * GLA_HEAD_DV, (h + 1) * GLA_HEAD_DV)
        q = q_ref[:, ks] * (GLA_HEAD_DK ** -0.5)
        k = k_ref[:, ks]
        v = v_ref[:, vs].astype(bf16)
        e = jnp.exp(dec[:, ks])
        state = s_ref[h]
        o = lax.dot_general((q * e[:C]).astype(bf16), state.astype(bf16), nt, preferred_element_type=f32)
        attn = jnp.where(eye, lax.dot_general(q.astype(bf16), k.astype(bf16), nt, preferred_element_type=f32), 0.0)
        for lvl in range(GLA_LEVELS):
            el = e[(2 + lvl) * C:(3 + lvl) * C]
            sc = lax.dot_general((q * el).astype(bf16), (k * el).astype(bf16), nt, preferred_element_type=f32)
            attn = attn + jnp.where(level_masks[lvl], sc, 0.0)
        o = o + jnp.dot(attn.astype(bf16), v, preferred_element_type=f32)
        kd = (k * e[C:2 * C]).astype(bf16)
        s_ref[h] = state * e[C - 1:C] + jnp.dot(v.T, kd, preferred_element_type=f32)
        o = o * lax.rsqrt(jnp.mean(o * o, axis=-1, keepdims=True) + LN_EPS) * g_ref[:, vs]
        rr = r_ref[:, vs]
        o_ref[:, vs] = (rr * jax.nn.sigmoid(rr) * o).astype(o_ref.dtype)


def gla_core(p, a2, ab, norm_g, B, S):
    C = GLA_CHUNK
    n = S // C
    a2p = jnp.pad(a2, ((0, GLA_AD_PAD - GLA_LORA), (0, 0))).astype(bf16)
    lop = jnp.asarray(_gla_decay_operator(), dtype=bf16)
    row = lambda bi, ci: bi * n + ci
    const = lambda bi, ci: (0, 0)
    return pl.pallas_call(
        _gla_kernel,
        grid=(B, n),
        in_specs=[pl.BlockSpec((C, GLA_DK), lambda bi, ci: (row(bi, ci), 0)),
                  pl.BlockSpec((C, GLA_DK), lambda bi, ci: (row(bi, ci), 1)),
                  pl.BlockSpec((C, GLA_DV), lambda bi, ci: (row(bi, ci), 1)),
                  pl.BlockSpec((C, GLA_DV), lambda bi, ci: (row(bi, ci), 2)),
                  pl.BlockSpec((C, GLA_AD_PAD), lambda bi, ci: (row(bi, ci), (2 * GLA_DK + 2 * GLA_DV) // GLA_AD_PAD)),
                  pl.BlockSpec((GLA_AD_PAD, GLA_DK), const),
                  pl.BlockSpec((1, GLA_DK), const),
                  pl.BlockSpec((1, GLA_DV), const),
                  pl.BlockSpec((GLA_DECAY_ROWS, C), const)],
        out_specs=pl.BlockSpec((C, GLA_DV), lambda bi, ci: (row(bi, ci), 0)),
        out_shape=jax.ShapeDtypeStruct((B * S, GLA_DV), bf16),
        scratch_shapes=[pltpu.VMEM((GLA_HEADS, GLA_HEAD_DV, GLA_HEAD_DK), f32)],
        compiler_params=pltpu.CompilerParams(
            dimension_semantics=("parallel", "arbitrary"), vmem_limit_bytes=VMEM_LIMIT_BYTES),
        name="gla_chunked",
    )(p, p, p, p, p, a2p, ab.reshape(1, GLA_DK), norm_g.reshape(1, GLA_DV), lop)


PROJ_TM = 256
ROUTER_PAD = 128


def _proj_norm_kernel(n_in, *refs):
    a_refs = refs[:n_in]
    w_ref, x_ref, mod_ref, g_ref, b_ref, wr_ref, br_ref, xo_ref, ho_ref, lo_ref = refs[n_in:]
    acc = None
    k0 = 0
    for a_ref in a_refs:
        kw = a_ref.shape[1]
        part = jnp.dot(a_ref[...], w_ref[k0:k0 + kw, :], preferred_element_type=f32)
        acc = part if acc is None else acc + part
        k0 += kw
    gt, sc, sh = mod_ref[0, 0:1, :], mod_ref[0, 1:2, :], mod_ref[0, 2:3, :]
    z = DEEPNORM_ALPHA * x_ref[...] + (1.0 + gt) * acc
    mu = jnp.mean(z, axis=-1, keepdims=True)
    zc = z - mu
    var = jnp.mean(zc * zc, axis=-1, keepdims=True)
    xn = zc * lax.rsqrt(var + LN_EPS) * g_ref[...] + b_ref[...]
    xo_ref[...] = xn
    hf = xn * (1.0 + sc) + sh
    ho_ref[...] = hf
    lo_ref[...] = jnp.dot(hf.astype(bf16), wr_ref[...], preferred_element_type=f32) + br_ref[...]


def proj_norm(acts, wo, x, mod, ln_g, ln_b, w_router, b_router, S):
    T, D = x.shape
    tiles_per_batch = S // PROJ_TM
    const = lambda i: (0, 0)
    rowblk = lambda w: pl.BlockSpec((PROJ_TM, w), lambda i: (i, 0))
    in_specs = [rowblk(a.shape[1]) for a in acts]
    in_specs += [pl.BlockSpec(wo.shape, const), rowblk(D),
                 pl.BlockSpec((1, 3, D), lambda i: (i // tiles_per_batch, 0, 0)),
                 pl.BlockSpec((1, D), const), pl.BlockSpec((1, D), const),
                 pl.BlockSpec((D, ROUTER_PAD), const), pl.BlockSpec((1, ROUTER_PAD), const)]
    return pl.pallas_call(
        functools.partial(_proj_norm_kernel, len(acts)),
        grid=(T // PROJ_TM,),
        in_specs=in_specs,
        out_specs=[rowblk(D), rowblk(D), rowblk(ROUTER_PAD)],
        out_shape=[jax.ShapeDtypeStruct((T, D), f32), jax.ShapeDtypeStruct((T, D), f32),
                   jax.ShapeDtypeStruct((T, ROUTER_PAD), f32)],
        compiler_params=pltpu.CompilerParams(
            dimension_semantics=("parallel",), vmem_limit_bytes=VMEM_LIMIT_BYTES),
        name="proj_norm",
    )(*acts, wo.astype(bf16), x, mod, ln_g.reshape(1, D), ln_b.reshape(1, D), w_router, b_router)


MOE_ROWS = 256
MOE_VMEM_LIMIT_BYTES = 56 * 1024 * 1024
COMBINE_ROWS = 256


def _moe_ffn_kernel(tok_ref, be_ref, nused_ref, x_hbm, w1_ref, w3_ref, w2_ref, y_ref, xbuf, sem):
    i = pl.program_id(0)
    n_used = nused_ref[0]
    slot = lax.rem(i, 2)

    def start_gather(blk, s):
        def body(r, carry):
            t = tok_ref[blk * MOE_ROWS + r]
            pltpu.make_async_copy(x_hbm.at[pl.ds(t, 1)], xbuf.at[s, pl.ds(r, 1)], sem.at[s]).start()
            return carry
        lax.fori_loop(0, MOE_ROWS, body, 0, unroll=8)

    @pl.when((i == 0) & (n_used > 0))
    def _():
        start_gather(0, 0)

    @pl.when(i + 1 < n_used)
    def _():
        start_gather(i + 1, 1 - slot)

    @pl.when(i < n_used)
    def _():
        pltpu.make_async_copy(x_hbm.at[pl.ds(0, MOE_ROWS)], xbuf.at[slot], sem.at[slot]).wait()
        xb = xbuf[slot].astype(bf16)
        a = jnp.dot(xb, w1_ref[0, 0].astype(bf16), preferred_element_type=f32)
        b = jnp.dot(xb, w3_ref[0, 0].astype(bf16), preferred_element_type=f32)
        hid = (a * jax.nn.sigmoid(a) * b).astype(bf16)
        y_ref[...] = jnp.dot(hid, w2_ref[0, 0].astype(bf16), preferred_element_type=f32)

    @pl.when(i >= n_used)
    def _():
        y_ref[...] = jnp.zeros_like(y_ref)


def moe_ffn(x, buf_tok, blk_expert, n_used, w1, w3, w2, layer):
    D = x.shape[1]
    nblk = blk_expert.shape[0]
    wspec = lambda shape: pl.BlockSpec((1, 1) + shape, lambda i, tok, be, nu: (layer, be[i], 0, 0))
    return pl.pallas_call(
        _moe_ffn_kernel,
        grid_spec=pltpu.PrefetchScalarGridSpec(
            num_scalar_prefetch=3,
            grid=(nblk,),
            in_specs=[pl.BlockSpec(memory_space=pl.ANY),
                      wspec((D, EXPERT_FF)), wspec((D, EXPERT_FF)), wspec((EXPERT_FF, D))],
            out_specs=pl.BlockSpec((MOE_ROWS, D), lambda i, tok, be, nu: (i, 0)),
            scratch_shapes=[pltpu.VMEM((2, MOE_ROWS, D), f32), pltpu.SemaphoreType.DMA((2,))]),
        out_shape=jax.ShapeDtypeStruct((nblk * MOE_ROWS, D), f32),
        compiler_params=pltpu.CompilerParams(
            dimension_semantics=("arbitrary",), vmem_limit_bytes=MOE_VMEM_LIMIT_BYTES),
        name="moe_expert_ffn",
    )(buf_tok, blk_expert, n_used, x, w1, w3, w2)


def _moe_combine_kernel(emit_h, s0_ref, s1_ref, y_hbm, gate_ref, x_ref, mod_ref, g_ref, b_ref, *rest):
    if emit_h:
        o_ref, h_ref, buf, sem = rest
    else:
        o_ref, buf, sem = rest
    i = pl.program_id(0)
    n = pl.num_programs(0)
    slot = lax.rem(i, 2)

    def start_gather(blk, s):
        def body(r, carry):
            t = blk * COMBINE_ROWS + r
            pltpu.make_async_copy(y_hbm.at[pl.ds(s0_ref[t], 1)], buf.at[s, 0, pl.ds(r, 1)], sem.at[s]).start()
            pltpu.make_async_copy(y_hbm.at[pl.ds(s1_ref[t], 1)], buf.at[s, 1, pl.ds(r, 1)], sem.at[s]).start()
            return carry
        lax.fori_loop(0, COMBINE_ROWS, body, 0, unroll=8)

    @pl.when(i == 0)
    def _():
        start_gather(0, 0)

    @pl.when(i + 1 < n)
    def _():
        start_gather(i + 1, 1 - slot)

    for c in range(MOE_TOPK):
        pltpu.make_async_copy(y_hbm.at[pl.ds(0, COMBINE_ROWS)], buf.at[slot, c], sem.at[slot]).wait()
    y = buf[slot, 0] * gate_ref[:, 0:1] + buf[slot, 1] * gate_ref[:, 1:2]
    z = DEEPNORM_ALPHA * x_ref[...] + (1.0 + mod_ref[0, 0:1, :]) * y
    mu = jnp.mean(z, axis=-1, keepdims=True)
    zc = z - mu
    var = jnp.mean(zc * zc, axis=-1, keepdims=True)
    xn = zc * lax.rsqrt(var + LN_EPS) * g_ref[...] + b_ref[...]
    o_ref[...] = xn
    if emit_h:
        h_ref[...] = (xn * (1.0 + mod_ref[0, 1:2, :]) + mod_ref[0, 2:3, :]).astype(h_ref.dtype)


def moe_combine_norm(y_rows, slot0, slot1, gate, x, mod, ln_g, ln_b, S, emit_h):
    T, D = x.shape
    tiles_per_batch = S // COMBINE_ROWS
    const = lambda i, s0, s1: (0, 0)
    rowblk = pl.BlockSpec((COMBINE_ROWS, D), lambda i, s0, s1: (i, 0))
    out_specs = [rowblk, rowblk] if emit_h else [rowblk]
    out_shape = [jax.ShapeDtypeStruct((T, D), f32)] + ([jax.ShapeDtypeStruct((T, D), bf16)] if emit_h else [])
    return pl.pallas_call(
        functools.partial(_moe_combine_kernel, emit_h),
        grid_spec=pltpu.PrefetchScalarGridSpec(
            num_scalar_prefetch=2,
            grid=(T // COMBINE_ROWS,),
            in_specs=[pl.BlockSpec(memory_space=pl.ANY),
                      pl.BlockSpec((COMBINE_ROWS, MOE_TOPK), lambda i, s0, s1: (i, 0)), rowblk,
                      pl.BlockSpec((1, 3, D), lambda i, s0, s1: (i // tiles_per_batch, 0, 0)),
                      pl.BlockSpec((1, D), const), pl.BlockSpec((1, D), const)],
            out_specs=out_specs,
            scratch_shapes=[pltpu.VMEM((2, MOE_TOPK, COMBINE_ROWS, D), f32), pltpu.SemaphoreType.DMA((2,))]),
        out_shape=out_shape,
        compiler_params=pltpu.CompilerParams(
            dimension_semantics=("arbitrary",), vmem_limit_bytes=VMEM_LIMIT_BYTES),
        name="moe_combine_norm",
    )(slot0, slot1, y_rows, gate, x, mod, ln_g.reshape(1, D), ln_b.reshape(1, D))


def router_params(wg, bg, we, be):
    D = wg.shape[0]
    pad = ROUTER_PAD - MOE_GROUPS - N_EXPERTS
    w = jnp.concatenate([wg, we, jnp.zeros((D, pad), wg.dtype)], 1).astype(bf16)
    b = jnp.concatenate([bg, be, jnp.zeros((pad,), bg.dtype)]).reshape(1, ROUTER_PAD)
    return w, b


def moe_route(logits):
    T = logits.shape[0]
    g_prob = jax.nn.softmax(logits[:, :MOE_GROUPS], -1)
    g_top = jnp.argmax(g_prob, -1).astype(jnp.int32)
    p_top = jnp.max(g_prob, -1)
    e_logits = logits[:, MOE_GROUPS:MOE_GROUPS + N_EXPERTS].reshape(T, MOE_GROUPS, MOE_PER_GROUP)
    e_sel = jnp.take_along_axis(e_logits, g_top[:, None, None], axis=1)[:, 0]
    i1 = jnp.argmax(e_sel, -1).astype(jnp.int32)
    v1 = jnp.max(e_sel, -1)
    rest = jnp.where(jnp.arange(MOE_PER_GROUP, dtype=jnp.int32)[None, :] == i1[:, None], -jnp.inf, e_sel)
    i2 = jnp.argmax(rest, -1).astype(jnp.int32)
    v2 = jnp.max(rest, -1)
    gate = p_top[:, None] * jax.nn.softmax(jnp.stack([v1, v2], -1), -1)
    expert = g_top[:, None] * MOE_PER_GROUP + jnp.stack([i1, i2], -1)
    return gate, expert


def moe_layout(expert):
    T = expert.shape[0]
    A = T * MOE_TOPK
    e_flat = expert.reshape(A)
    onehot = (e_flat[:, None] == jnp.arange(N_EXPERTS, dtype=jnp.int32)[None, :]).astype(jnp.int32)
    rank = jnp.cumsum(onehot, axis=0)
    counts = rank[-1]
    padded = (counts + MOE_ROWS - 1) // MOE_ROWS * MOE_ROWS
    pad_end = jnp.cumsum(padded)
    pad_start = pad_end - padded
    slot = jnp.sum(onehot * (pad_start[None, :] + rank - 1), axis=1).astype(jnp.int32)
    tok_flat = jnp.arange(A, dtype=jnp.int32) // MOE_TOPK
    nblk = A // MOE_ROWS + N_EXPERTS
    buf_tok = jnp.zeros((nblk * MOE_ROWS,), jnp.int32).at[slot].set(tok_flat)
    n_used = (pad_end[-1] // MOE_ROWS).astype(jnp.int32)
    blk = jnp.minimum(jnp.arange(nblk, dtype=jnp.int32), n_used - 1) * MOE_ROWS
    blk_expert = jnp.minimum(jnp.sum(blk[:, None] >= pad_end[None, :], axis=1), N_EXPERTS - 1).astype(jnp.int32)
    slot2 = slot.reshape(T, MOE_TOPK)
    return buf_tok, blk_expert, n_used.reshape(1), slot2[:, 0], slot2[:, 1]


def kernel(x, c, positions, ada_w, ada_b, ln_g, ln_b, ev_win, ev_wo, ret_gn_g, ret_gn_b, rw_mu, rw_w0, rw_w2, rw_a0, rw_a2, rw_g2, rw_kk, rw_ka, rw_rk, rw_lnx_g, rw_lnx_b, rw_v0, rw_v1, rw_v2, od_win, od_wo, gla_a2, gla_ab, gla_norm_g, moe_wg, moe_bg, moe_we, moe_be, moe_w1, moe_w3, moe_w2):
    B, S, D = x.shape
    T = B * S
    depth = ada_w.shape[0]
    mods = ada_mod(c, ada_w, ada_b).reshape(depth, B, 6, D)
    h = (x * (1.0 + mods[0, :, 1])[:, None, :] + mods[0, :, 0][:, None, :]).astype(bf16).reshape(T, D)
    x = x.reshape(T, D)
    v_first = None
    for layer in range(depth):
        j = layer // 2
        mixer_mod = jnp.stack([mods[layer, :, 2], mods[layer, :, 4], mods[layer, :, 3]], axis=1)
        if layer % 2 == 0:
            p = matmul(h, even_win_layout(ev_win[j]))
            out_a = retention_core(p, positions, ret_gn_g[j], ret_gn_b[j], B, S)
            vres = None if j == 0 else (rw_v0[j - 1], rw_v1[j - 1], rw_v2[j - 1])
            out_b, v_first = rwkv_core(p, v_first, vres, rw_mu[j], rw_w0[j], rw_w2[j], rw_a0[j], rw_a2[j], rw_g2[j],
                                       rw_kk[j], rw_ka[j], rw_rk[j], rw_lnx_g[j], rw_lnx_b[j], B, S)
            acts, wo = [out_a, out_b], ev_wo[j]
        else:
            p = matmul(h, jnp.pad(od_win[j], ((0, 0), (0, ODD_PROJ_PAD - ODD_PROJ))), tn=896)
            acts, wo = [gla_core(p, gla_a2[j], gla_ab[j], gla_norm_g[j], B, S)], od_wo[j]
        w_router, b_router = router_params(moe_wg[layer], moe_bg[layer], moe_we[layer], moe_be[layer])
        x, hf, logits = proj_norm(acts, wo, x, mixer_mod, ln_g[layer, 0], ln_b[layer, 0], w_router, b_router, S)
        gate, expert = moe_route(logits)
        buf_tok, blk_expert, n_used, slot0, slot1 = moe_layout(expert)
        y_rows = moe_ffn(hf, buf_tok, blk_expert, n_used, moe_w1, moe_w3, moe_w2, layer)
        last = layer == depth - 1
        nxt = layer if last else layer + 1
        moe_mod = jnp.stack([mods[layer, :, 5], mods[nxt, :, 1], mods[nxt, :, 0]], axis=1)
        outs = moe_combine_norm(y_rows, slot0, slot1, gate, x, moe_mod, ln_g[layer, 1], ln_b[layer, 1], S, not last)
        x = outs[0]
        if not last:
            h = outs[1]
    return x.reshape(B, S, D)
```

```python
import functools

import jax
import jax.numpy as jnp
import numpy as np
from jax import lax
from jax.experimental import pallas as pl
from jax.experimental.pallas import tpu as pltpu

D_MODEL = 2048
DEPTH = 4
MIX_A = D_MODEL // 2
RET_HEAD_DIM = 256
RET_HEADS = MIX_A // RET_HEAD_DIM
RET_CHUNK = 128
RET_COLS = 4 * MIX_A
ROPE_BASE = 10000.0
MIX_B = D_MODEL - MIX_A
RWKV_HEAD_DIM = 64
RWKV_HEADS = MIX_B // RWKV_HEAD_DIM
RWKV_LORA_W = 64
RWKV_LORA_A = 64
RWKV_LORA_G = 160
RWKV_COLS = 3 * MIX_B + RWKV_LORA_W + RWKV_LORA_A + RWKV_LORA_G
RWKV_SPLITS = (MIX_B, 2 * MIX_B, 3 * MIX_B, 3 * MIX_B + RWKV_LORA_W, 3 * MIX_B + RWKV_LORA_W + RWKV_LORA_A)
RWKV_GN_EPS = 64e-5
GLA_HEADS = 4
GLA_DK = D_MODEL // 2
GLA_DV = D_MODEL
GLA_HEAD_DK = GLA_DK // GLA_HEADS
GLA_HEAD_DV = GLA_DV // GLA_HEADS
GLA_LORA = 16
GLA_GATE_NORM = 16.0
GLA_CHUNK = 64
ODD_PROJ = 2 * GLA_DK + 2 * GLA_DV + GLA_LORA
MOE_GROUPS = 4
MOE_PER_GROUP = 8
N_EXPERTS = MOE_GROUPS * MOE_PER_GROUP
MOE_TOPK = 2
EXPERT_FF = D_MODEL // 4
DEEPNORM_ALPHA = (2 * DEPTH) ** 0.25
LN_EPS = 1e-5

f32 = jnp.float32
bf16 = jnp.bfloat16

VMEM_LIMIT_BYTES = 48 * 1024 * 1024


def _dotb(a, b):
    return jnp.dot(a.astype(bf16), b.astype(bf16), preferred_element_type=f32)


def _split_dot(x, ones_bf16):
    hi = x.astype(bf16)
    lo = (x - hi.astype(f32)).astype(bf16)
    return jnp.dot(hi, ones_bf16, preferred_element_type=f32) + jnp.dot(lo, ones_bf16, preferred_element_type=f32)


def _softplus(z):
    return jnp.maximum(z, 0.0) + jnp.log1p(jnp.exp(-jnp.abs(z)))


ADA_ROWS = 8
ADA_TN = 1024


def _ada_kernel(c_ref, w_ref, b_ref, o_ref):
    c = c_ref[...]
    act = (c * jax.nn.sigmoid(c)).astype(bf16)
    o_ref[0] = jnp.dot(act, w_ref[0].astype(bf16), preferred_element_type=f32) + b_ref[0]


def ada_mod(c, ada_w, ada_b):
    B, D = c.shape
    depth, _, N = ada_w.shape
    c_pad = jnp.pad(c, ((0, ADA_ROWS - B), (0, 0)))
    out = pl.pallas_call(
        _ada_kernel,
        grid=(depth, N // ADA_TN),
        in_specs=[pl.BlockSpec((ADA_ROWS, D), lambda l, j: (0, 0)),
                  pl.BlockSpec((1, D, ADA_TN), lambda l, j: (l, 0, j)),
                  pl.BlockSpec((1, 1, ADA_TN), lambda l, j: (l, 0, j))],
        out_specs=pl.BlockSpec((1, ADA_ROWS, ADA_TN), lambda l, j: (l, 0, j)),
        out_shape=jax.ShapeDtypeStruct((depth, ADA_ROWS, N), f32),
        compiler_params=pltpu.CompilerParams(
            dimension_semantics=("parallel", "parallel"), vmem_limit_bytes=VMEM_LIMIT_BYTES),
        name="ada_mod",
    )(c_pad, ada_w, ada_b.reshape(depth, 1, N))
    return out[:, :B]


def _matmul_kernel(x_ref, w_ref, o_ref):
    o_ref[...] = jnp.dot(x_ref[...], w_ref[...], preferred_element_type=f32)


def matmul(x, w, tm=1024, tn=512):
    M, K = x.shape
    Np = w.shape[1]
    assert Np % tn == 0 and M % min(tm, M) == 0
    wb = w.astype(bf16)
    tm = min(tm, M)
    return pl.pallas_call(
        _matmul_kernel,
        grid=(M // tm, Np // tn),
        in_specs=[pl.BlockSpec((tm, K), lambda i, j: (i, 0)),
                  pl.BlockSpec((K, tn), lambda i, j: (0, j))],
        out_specs=pl.BlockSpec((tm, tn), lambda i, j: (i, j)),
        out_shape=jax.ShapeDtypeStruct((M, Np), f32),
        compiler_params=pltpu.CompilerParams(
            dimension_semantics=("parallel", "parallel"), vmem_limit_bytes=VMEM_LIMIT_BYTES),
        name="dense_matmul",
    )(x.astype(bf16), wb)


WKV_CHUNK = 64
WKV_PAIR = 2 * RWKV_HEAD_DIM
WKV_PAIRS = MIX_B // WKV_PAIR
RWKV_WA_PAD = 128
RWKV_G_PAD = 256
RWKV_V_PAD = 128
EV_R0 = RET_COLS
EV_WA0 = EV_R0 + 3 * MIX_B
EV_G0 = EV_WA0 + 2 * RWKV_WA_PAD
EVEN_COLS = EV_G0 + RWKV_G_PAD


def _rwkv_kernel(has_vres, *refs):
    if has_vres:
        (r_ref, k_ref, v_ref, wa_ref, gd_ref, vf_ref, mu_ref, muwa_ref, mug_ref, w2_ref, a2_ref, g2_ref,
         vec_ref, v1_ref, v2_ref, o_ref, h_ref, prev_ref, prevwa_ref, prevg_ref) = refs
    else:
        (r_ref, k_ref, v_ref, wa_ref, gd_ref, mu_ref, muwa_ref, mug_ref, w2_ref, a2_ref, g2_ref,
         vec_ref, o_ref, vf_ref, h_ref, prev_ref, prevwa_ref, prevg_ref) = refs
    C = WKV_CHUNK
    P2 = WKV_PAIR

    @pl.when(pl.program_id(1) == 0)
    def _():
        h_ref[...] = jnp.zeros_like(h_ref)
        prev_ref[...] = jnp.zeros_like(prev_ref)
        prevwa_ref[...] = jnp.zeros_like(prevwa_ref)
        prevg_ref[...] = jnp.zeros_like(prevg_ref)

    def shift(x, carry_ref, j, mu):
        first = lax.broadcasted_iota(jnp.int32, x.shape, 0) == 0
        prev = jnp.where(first, jnp.broadcast_to(carry_ref[j, 7:8, :], x.shape), pltpu.roll(x, 1, 0))
        carry_ref[j] = x[C - 8:, :]
        return x + (prev - x) * mu

    r_all = shift(r_ref[...], prev_ref, 0, mu_ref[0:1, :])
    k_all = shift(k_ref[...], prev_ref, 1, mu_ref[1:2, :])
    v_all = shift(v_ref[...], prev_ref, 2, mu_ref[2:3, :])
    wa = shift(wa_ref[...], prevwa_ref, 0, muwa_ref[...])
    gd = shift(gd_ref[...], prevg_ref, 0, mug_ref[...])
    w0, a0, k_k, k_a, r_k, lnx_g, lnx_b = (vec_ref[i:i + 1, :] for i in range(7))

    w_log = -_softplus(-(w0 + _dotb(jnp.tanh(wa), w2_ref[...]))) - 0.5
    lw_all = -jnp.exp(w_log)
    a_all = jax.nn.sigmoid(a0 + _dotb(wa, a2_ref[...]))
    gate_all = _dotb(jax.nn.sigmoid(gd), g2_ref[...])
    if has_vres:
        v0 = vec_ref[7:8, :]
        mix = jax.nn.sigmoid(v0 + _dotb(_dotb(v_all, v1_ref[...]), v2_ref[...]))
        v_all = v_all + (vf_ref[...] - v_all) * mix
    else:
        vf_ref[...] = v_all
    kkraw_all = k_all * k_k
    k_all = k_all * (1.0 + (a_all - 1.0) * k_a)

    row = lax.broadcasted_iota(jnp.int32, (P2, P2), 0)
    col = lax.broadcasted_iota(jnp.int32, (P2, P2), 1)
    same_head = (row >= C) == (col >= RWKV_HEAD_DIM)
    head_ones = jnp.where((row >= RWKV_HEAD_DIM) == (col >= RWKV_HEAD_DIM), 1.0, 0.0).astype(bf16)
    t_idx = jnp.bitwise_and(row, C - 1)
    s_idx = jnp.bitwise_and(col, C - 1)
    strict_lower = t_idx > s_idx
    incl_lower = t_idx >= s_idx
    eye = row == col
    tri = (lax.broadcasted_iota(jnp.int32, (C, C), 0) >= lax.broadcasted_iota(jnp.int32, (C, C), 1)).astype(bf16)
    zeros_pp = jnp.zeros((P2, P2), f32)
    nt = (((1,), (1,)), ((), ()))
    pairs = range(WKV_PAIRS)
    sl = [slice(p * P2, (p + 1) * P2) for p in pairs]

    def bd(x):
        return jnp.where(same_head, jnp.concatenate([x, x], axis=0), 0.0)

    kk = []
    for p in pairs:
        kr = kkraw_all[:, sl[p]]
        ss = _split_dot(kr * kr, head_ones)
        kk.append(kr / jnp.maximum(jnp.sqrt(ss), 1e-12))
    lw = [lw_all[:, sl[p]] for p in pairs]
    cum = []
    for p in pairs:
        hi = lw[p].astype(bf16)
        r1 = lw[p] - hi.astype(f32)
        mid = r1.astype(bf16)
        lo = (r1 - mid.astype(f32)).astype(bf16)
        cum.append(jnp.dot(tri, hi, preferred_element_type=f32) + jnp.dot(tri, mid, preferred_element_type=f32)
                   + jnp.dot(tri, lo, preferred_element_type=f32))
    tot = [cum[p][C - 1:C, :] for p in pairs]
    w_incl = [jnp.exp(cum[p]) for p in pairs]
    w_excl = [jnp.exp(cum[p] - lw[p]) for p in pairs]
    w_inv = [jnp.exp(-cum[p]) for p in pairs]
    w_rest = [jnp.exp(tot[p] - cum[p]) for p in pairs]
    r_p = [r_all[:, sl[p]] for p in pairs]
    k_p = [k_all[:, sl[p]] for p in pairs]
    v_p = [v_all[:, sl[p]] for p in pairs]
    a_p = [a_all[:, sl[p]] for p in pairs]
    at = [bd(-kk[p] * w_excl[p]) for p in pairs]
    rt = [bd(r_p[p] * w_incl[p]) for p in pairs]
    bt = [bd(kk[p] * a_p[p] * w_inv[p]) for p in pairs]
    kt = [bd(k_p[p] * w_inv[p]) for p in pairs]
    bh = [bd(kk[p] * a_p[p] * w_rest[p]) for p in pairs]
    kh = [bd(k_p[p] * w_rest[p]) for p in pairs]
    vb = [bd(v_p[p]) for p in pairs]

    scores = [lax.dot_general(jnp.concatenate([at[p], rt[p]], axis=0).astype(bf16),
                              jnp.concatenate([bt[p], kt[p]], axis=0).astype(bf16),
                              nt, preferred_element_type=f32) for p in pairs]
    a_ab = [jnp.where(strict_lower, scores[p][:P2, :P2], 0.0) for p in pairs]
    a_ak = [jnp.where(strict_lower, scores[p][:P2, P2:], 0.0) for p in pairs]
    a_rb = [jnp.where(incl_lower, scores[p][P2:, :P2], 0.0) for p in pairs]
    a_rk = [jnp.where(incl_lower, scores[p][P2:, P2:], 0.0) for p in pairs]
    x0 = [jnp.concatenate([at[p], _dotb(a_ak[p], vb[p])], axis=1) for p in pairs]
    eye_f = jnp.where(eye, 1.0, 0.0)
    inv = [eye_f + a_ab[p] for p in pairs]
    pw = [_dotb(a_ab[p], a_ab[p]) for p in pairs]
    for _ in range(int(np.log2(C)) - 2):
        both = [_dotb(pw[p], jnp.concatenate([pw[p], inv[p]], axis=1)) for p in pairs]
        inv = [inv[p] + both[p][:, P2:] for p in pairs]
        pw = [both[p][:, :P2] for p in pairs]
    inv = [inv[p] + _dotb(pw[p], inv[p]) for p in pairs]
    x = [_dotb(inv[p], x0[p]) for p in pairs]
    rhs = [jnp.concatenate([x[p], jnp.concatenate([zeros_pp, vb[p]], axis=1)], axis=0) for p in pairs]
    gz = [_dotb(jnp.concatenate([a_rb[p], a_rk[p]], axis=1), rhs[p]) for p in pairs]
    mn = [_dotb(jnp.concatenate([bh[p], kh[p]], axis=0).T, rhs[p]) for p in pairs]
    yh = []
    for p in pairs:
        g = rt[p] + gz[p][:, :P2]
        m = mn[p][:, :P2] + jnp.where(eye, jnp.broadcast_to(jnp.exp(tot[p]), (P2, P2)), 0.0)
        yh.append(_dotb(jnp.concatenate([g, m], axis=0), h_ref[p]))
    for p in pairs:
        y2 = yh[p][:P2] + gz[p][:, P2:]
        h_ref[p] = yh[p][P2:] + mn[p][:, P2:]
        y = y2[:C] + y2[C:]
        mean = _split_dot(y, head_ones) * (1.0 / RWKV_HEAD_DIM)
        yc = y - mean
        var = _split_dot(yc * yc, head_ones) * (1.0 / RWKV_HEAD_DIM)
        yn = yc * lax.rsqrt(var + RWKV_GN_EPS) * lnx_g[:, sl[p]] + lnx_b[:, sl[p]]
        bonus = _split_dot(r_p[p] * k_p[p] * r_k[:, sl[p]], head_ones) * v_p[p]
        o_ref[:, sl[p]] = ((yn + bonus) * gate_all[:, sl[p]]).astype(o_ref.dtype)


def rwkv_core(p, v_first, vres, mu, w0, w2, a0, a2, g2, k_k, k_a, r_k, lnx_g, lnx_b, B, S):
    C = WKV_CHUNK
    n = S // C
    has_vres = vres is not None
    r_mu, k_mu, v_mu, wd_mu, ad_mu, gd_mu = jnp.split(mu, RWKV_SPLITS)
    mu3 = jnp.stack([r_mu, k_mu, v_mu])
    mu_wa = jnp.concatenate([wd_mu, ad_mu]).reshape(1, RWKV_WA_PAD)
    mu_g = jnp.pad(gd_mu, (0, RWKV_G_PAD - RWKV_LORA_G)).reshape(1, RWKV_G_PAD)
    w2p = jnp.pad(w2, ((0, RWKV_WA_PAD - RWKV_LORA_W), (0, 0))).astype(bf16)
    a2p = jnp.pad(a2, ((RWKV_LORA_W, 0), (0, 0))).astype(bf16)
    g2p = jnp.pad(g2, ((0, RWKV_G_PAD - RWKV_LORA_G), (0, 0))).astype(bf16)
    vecs = [w0, a0, k_k, k_a, r_k.reshape(MIX_B), lnx_g, lnx_b]
    if has_vres:
        vecs.append(vres[0])
    vec = jnp.stack(vecs)
    nv = vec.shape[0]
    row = lambda bi, ci: bi * n + ci
    const = lambda bi, ci: (0, 0)
    tile = lambda w, j: pl.BlockSpec((C, w), lambda bi, ci: (row(bi, ci), j))
    in_specs = [tile(MIX_B, EV_R0 // MIX_B), tile(MIX_B, EV_R0 // MIX_B + 1), tile(MIX_B, EV_R0 // MIX_B + 2),
                tile(RWKV_WA_PAD, EV_WA0 // RWKV_WA_PAD), tile(RWKV_G_PAD, EV_G0 // RWKV_G_PAD)]
    args = [p, p, p, p, p]
    if has_vres:
        in_specs.append(tile(MIX_B, 0))
        args.append(v_first)
    in_specs += [pl.BlockSpec((3, MIX_B), const), pl.BlockSpec((1, RWKV_WA_PAD), const),
                 pl.BlockSpec((1, RWKV_G_PAD), const), pl.BlockSpec((RWKV_WA_PAD, MIX_B), const),
                 pl.BlockSpec((RWKV_WA_PAD, MIX_B), const), pl.BlockSpec((RWKV_G_PAD, MIX_B), const),
                 pl.BlockSpec((nv, MIX_B), const)]
    args += [mu3, mu_wa, mu_g, w2p, a2p, g2p, vec]
    out_specs = [tile(MIX_B, 0)]
    out_shape = [jax.ShapeDtypeStruct((B * S, MIX_B), bf16)]
    if has_vres:
        v1p = jnp.pad(vres[1], ((0, 0), (0, RWKV_V_PAD - vres[1].shape[1]))).astype(bf16)
        v2p = jnp.pad(vres[2], ((0, RWKV_V_PAD - vres[2].shape[0]), (0, 0))).astype(bf16)
        in_specs += [pl.BlockSpec((MIX_B, RWKV_V_PAD), const), pl.BlockSpec((RWKV_V_PAD, MIX_B), const)]
        args += [v1p, v2p]
    else:
        out_specs.append(tile(MIX_B, 0))
        out_shape.append(jax.ShapeDtypeStruct((B * S, MIX_B), f32))
    outs = pl.pallas_call(
        functools.partial(_rwkv_kernel, has_vres),
        grid=(B, n),
        in_specs=in_specs,
        out_specs=out_specs,
        out_shape=out_shape,
        scratch_shapes=[pltpu.VMEM((WKV_PAIRS, WKV_PAIR, WKV_PAIR), f32),
                        pltpu.VMEM((3, 8, MIX_B), f32), pltpu.VMEM((1, 8, RWKV_WA_PAD), f32),
                        pltpu.VMEM((1, 8, RWKV_G_PAD), f32)],
        compiler_params=pltpu.CompilerParams(
            dimension_semantics=("parallel", "arbitrary"), vmem_limit_bytes=VMEM_LIMIT_BYTES),
        name="rwkv7_time_mix",
    )(*args)
    return (outs[0], v_first) if has_vres else (outs[0], outs[1])


def even_win_layout(win):
    ret, rw = win[:, :RET_COLS], win[:, RET_COLS:]
    rkv = rw[:, :RWKV_SPLITS[2]]
    wd, ad, gd = rw[:, RWKV_SPLITS[2]:RWKV_SPLITS[3]], rw[:, RWKV_SPLITS[3]:RWKV_SPLITS[4]], rw[:, RWKV_SPLITS[4]:]
    z = lambda n: jnp.zeros((win.shape[0], n), win.dtype)
    return jnp.concatenate([ret, rkv, wd, ad, z(RWKV_WA_PAD), gd, z(RWKV_G_PAD - RWKV_LORA_G)], axis=1)


RET_LOG_GAMMA = tuple(float(np.log1p(-np.exp2(-5.0 - h))) for h in range(RET_HEADS))
ROPE_HALF = RET_HEAD_DIM // 2


def _ret_kernel(q_ref, k_ref, v_ref, g_ref, pos_ref, inv_ref, gng_ref, gnb_ref, o_ref, s_ref):
    C = RET_CHUNK
    d = RET_HEAD_DIM

    @pl.when(pl.program_id(1) == 0)
    def _():
        s_ref[...] = jnp.zeros_like(s_ref)

    ang = pos_ref[...] * inv_ref[...]
    cos = jnp.cos(ang)
    sin = jnp.sin(ang)

    def rot(x):
        x1, x2 = x[:, :ROPE_HALF], x[:, ROPE_HALF:]
        return jnp.concatenate([x1 * cos - x2 * sin, x1 * sin + x2 * cos], axis=-1)

    ri = lax.broadcasted_iota(jnp.int32, (C, C), 0)
    ci = lax.broadcasted_iota(jnp.int32, (C, C), 1)
    diff = (ri - ci).astype(f32)
    row = lax.broadcasted_iota(jnp.int32, (C, d), 0).astype(f32)
    nt = (((1,), (1,)), ((), ()))

    for h in range(RET_HEADS):
        hs = slice(h * d, (h + 1) * d)
        lg = RET_LOG_GAMMA[h]
        q = rot(q_ref[:, hs])
        k = rot(k_ref[:, hs]) * (d ** -0.5)
        v = v_ref[:, hs].astype(bf16)
        decay = jnp.where(diff >= 0, jnp.exp(lg * jnp.maximum(diff, 0.0)), 0.0)
        scores = lax.dot_general(q.astype(bf16), k.astype(bf16), nt, preferred_element_type=f32) * decay
        state = s_ref[h]
        y = jnp.dot(scores.astype(bf16), v, preferred_element_type=f32)
        y = y + jnp.dot((q * jnp.exp(lg * (row + 1.0))).astype(bf16), state.astype(bf16), preferred_element_type=f32)
        kw = (k * jnp.exp(lg * (C - 1.0 - row))).astype(bf16)
        s_ref[h] = state * float(np.exp(lg * C)) + jnp.dot(kw.T, v, preferred_element_type=f32)
        mu = jnp.mean(y, axis=-1, keepdims=True)
        yc = y - mu
        var = jnp.mean(yc * yc, axis=-1, keepdims=True)
        yn = yc * lax.rsqrt(var + LN_EPS) * gng_ref[:, hs] + gnb_ref[:, hs]
        g = g_ref[:, hs]
        o_ref[:, hs] = (g * jax.nn.sigmoid(g) * yn).astype(o_ref.dtype)


def retention_core(p, positions, gn_g, gn_b, B, S):
    C = RET_CHUNK
    n = S // C
    inv = (ROPE_BASE ** (-jnp.arange(ROPE_HALF, dtype=f32) / ROPE_HALF)).reshape(1, ROPE_HALF)
    pos = positions.astype(f32).reshape(B * S, 1)
    row = lambda bi, ci: bi * n + ci
    const = lambda bi, ci: (0, 0)
    col = lambda j: pl.BlockSpec((C, MIX_A), lambda bi, ci: (row(bi, ci), j))
    return pl.pallas_call(
        _ret_kernel,
        grid=(B, n),
        in_specs=[col(0), col(1), col(2), col(3),
                  pl.BlockSpec((C, 1), lambda bi, ci: (row(bi, ci), 0)),
                  pl.BlockSpec((1, ROPE_HALF), const),
                  pl.BlockSpec((1, MIX_A), const),
                  pl.BlockSpec((1, MIX_A), const)],
        out_specs=pl.BlockSpec((C, MIX_A), lambda bi, ci: (row(bi, ci), 0)),
        out_shape=jax.ShapeDtypeStruct((B * S, MIX_A), bf16),
        scratch_shapes=[pltpu.VMEM((RET_HEADS, RET_HEAD_DIM, RET_HEAD_DIM), f32)],
        compiler_params=pltpu.CompilerParams(
            dimension_semantics=("parallel", "arbitrary"), vmem_limit_bytes=VMEM_LIMIT_BYTES),
        name="retention_chunked",
    )(p, p, p, p, pos, inv, gn_g.reshape(1, MIX_A), gn_b.reshape(1, MIX_A))


GLA_LEVELS = int(np.log2(GLA_CHUNK))
GLA_DECAY_ROWS = (2 + GLA_LEVELS) * GLA_CHUNK
GLA_AD_PAD = 128
ODD_PROJ_PAD = 2 * GLA_DK + 2 * GLA_DV + GLA_AD_PAD


def _gla_decay_operator():
    C = GLA_CHUNK
    i = np.arange(C)[:, None]
    j = np.arange(C)[None, :]
    blocks = [(j <= i), (j > i)]
    for lvl in range(GLA_LEVELS):
        s = C >> (lvl + 1)
        mid = (i // (2 * s)) * (2 * s) + s
        right = i >= mid
        blocks.append(np.where(right, (j >= mid) & (j <= i), (j > i) & (j < mid)))
    return np.concatenate(blocks, 0).astype(np.float32)


def _gla_kernel(q_ref, k_ref, v_ref, r_ref, ad_ref, a2_ref, ab_ref, g_ref, lop_ref, o_ref, s_ref):
    C = GLA_CHUNK

    @pl.when(pl.program_id(1) == 0)
    def _():
        s_ref[...] = jnp.zeros_like(s_ref)

    x = jnp.dot(ad_ref[...].astype(bf16), a2_ref[...], preferred_element_type=f32) + ab_ref[...]
    la = (jnp.minimum(x, 0.0) - jnp.log1p(jnp.exp(-jnp.abs(x)))) * (1.0 / GLA_GATE_NORM)
    hi = la.astype(bf16)
    r1 = la - hi.astype(f32)
    mid = r1.astype(bf16)
    lo = (r1 - mid.astype(f32)).astype(bf16)
    lop = lop_ref[...]
    dec = (jnp.dot(lop, hi, preferred_element_type=f32) + jnp.dot(lop, mid, preferred_element_type=f32)
           + jnp.dot(lop, lo, preferred_element_type=f32))

    ri = lax.broadcasted_iota(jnp.int32, (C, C), 0)
    ci = lax.broadcasted_iota(jnp.int32, (C, C), 1)
    level_masks = []
    for lvl in range(GLA_LEVELS):
        s = C >> (lvl + 1)
        same_block = (ri // (2 * s)) == (ci // (2 * s))
        level_masks.append(same_block & ((ri % (2 * s)) >= s) & ((ci % (2 * s)) < s))
    eye = ri == ci
    nt = (((1,), (1,)), ((), ()))

    for h in range(GLA_HEADS):
        ks = slice(h * GLA_HEAD_DK, (h + 1) * GLA_HEAD_DK)
        vs = slice(h * GLA_HEAD_DV, (h + 1) * GLA_HEAD_DV)
        q = q_ref[:, ks] * (GLA_HEAD_DK ** -0.5)
        k = k_ref[:, ks]
        v = v_ref[:, vs].astype(bf16)
        e = jnp.exp(dec[:, ks])
        state = s_ref[h]
        o = lax.dot_general((q * e[:C]).astype(bf16), state.astype(bf16), nt, preferred_element_type=f32)
        attn = jnp.where(eye, lax.dot_general(q.astype(bf16), k.astype(bf16), nt, preferred_element_type=f32), 0.0)
        for lvl in range(GLA_LEVELS):
            el = e[(2 + lvl) * C:(3 + lvl) * C]
            sc = lax.dot_general((q * el).astype(bf16), (k * el).astype(bf16), nt, preferred_element_type=f32)
            attn = attn + jnp.where(level_masks[lvl], sc, 0.0)
        o = o + jnp.dot(attn.astype(bf16), v, preferred_element_type=f32)
        kd = (k * e[C:2 * C]).astype(bf16)
        s_ref[h] = state * e[C - 1:C] + jnp.dot(v.T, kd, preferred_element_type=f32)
        o = o * lax.rsqrt(jnp.mean(o * o, axis=-1, keepdims=True) + LN_EPS) * g_ref[:, vs]
        rr = r_ref[:, vs]
        o_ref[:, vs] = (rr * jax.nn.sigmoid(rr) * o).astype(o_ref.dtype)


def gla_core(p, a2, ab, norm_g, B, S):
    C = GLA_CHUNK
    n = S // C
    a2p = jnp.pad(a2, ((0, GLA_AD_PAD - GLA_LORA), (0, 0))).astype(bf16)
    lop = jnp.asarray(_gla_decay_operator(), dtype=bf16)
    row = lambda bi, ci: bi * n + ci
    const = lambda bi, ci: (0, 0)
    return pl.pallas_call(
        _gla_kernel,
        grid=(B, n),
        in_specs=[pl.BlockSpec((C, GLA_DK), lambda bi, ci: (row(bi, ci), 0)),
                  pl.BlockSpec((C, GLA_DK), lambda bi, ci: (row(bi, ci), 1)),
                  pl.BlockSpec((C, GLA_DV), lambda bi, ci: (row(bi, ci), 1)),
                  pl.BlockSpec((C, GLA_DV), lambda bi, ci: (row(bi, ci), 2)),
                  pl.BlockSpec((C, GLA_AD_PAD), lambda bi, ci: (row(bi, ci), (2 * GLA_DK + 2 * GLA_DV) // GLA_AD_PAD)),
                  pl.BlockSpec((GLA_AD_PAD, GLA_DK), const),
                  pl.BlockSpec((1, GLA_DK), const),
                  pl.BlockSpec((1, GLA_DV), const),
                  pl.BlockSpec((GLA_DECAY_ROWS, C), const)],
        out_specs=pl.BlockSpec((C, GLA_DV), lambda bi, ci: (row(bi, ci), 0)),
        out_shape=jax.ShapeDtypeStruct((B * S, GLA_DV), bf16),
        scratch_shapes=[pltpu.VMEM((GLA_HEADS, GLA_HEAD_DV, GLA_HEAD_DK), f32)],
        compiler_params=pltpu.CompilerParams(
            dimension_semantics=("parallel", "arbitrary"), vmem_limit_bytes=VMEM_LIMIT_BYTES),
        name="gla_chunked",
    )(p, p, p, p, p, a2p, ab.reshape(1, GLA_DK), norm_g.reshape(1, GLA_DV), lop)


PROJ_TM = 256
ROUTER_PAD = 128


def _proj_norm_kernel(n_in, *refs):
    a_refs = refs[:n_in]
    w_ref, x_ref, mod_ref, g_ref, b_ref, wr_ref, br_ref, xo_ref, ho_ref, lo_ref = refs[n_in:]
    acc = None
    k0 = 0
    for a_ref in a_refs:
        kw = a_ref.shape[1]
        part = jnp.dot(a_ref[...], w_ref[k0:k0 + kw, :], preferred_element_type=f32)
        acc = part if acc is None else acc + part
        k0 += kw
    gt, sc, sh = mod_ref[0, 0:1, :], mod_ref[0, 1:2, :], mod_ref[0, 2:3, :]
    z = DEEPNORM_ALPHA * x_ref[...] + (1.0 + gt) * acc
    mu = jnp.mean(z, axis=-1, keepdims=True)
    zc = z - mu
    var = jnp.mean(zc * zc, axis=-1, keepdims=True)
    xn = zc * lax.rsqrt(var + LN_EPS) * g_ref[...] + b_ref[...]
    xo_ref[...] = xn
    hf = xn * (1.0 + sc) + sh
    ho_ref[...] = hf
    lo_ref[...] = jnp.dot(hf.astype(bf16), wr_ref[...], preferred_element_type=f32) + br_ref[...]


def proj_norm(acts, wo, x, mod, ln_g, ln_b, w_router, b_router, S):
    T, D = x.shape
    tiles_per_batch = S // PROJ_TM
    const = lambda i: (0, 0)
    rowblk = lambda w: pl.BlockSpec((PROJ_TM, w), lambda i: (i, 0))
    in_specs = [rowblk(a.shape[1]) for a in acts]
    in_specs += [pl.BlockSpec(wo.shape, const), rowblk(D),
                 pl.BlockSpec((1, 3, D), lambda i: (i // tiles_per_batch, 0, 0)),
                 pl.BlockSpec((1, D), const), pl.BlockSpec((1, D), const),
                 pl.BlockSpec((D, ROUTER_PAD), const), pl.BlockSpec((1, ROUTER_PAD), const)]
    return pl.pallas_call(
        functools.partial(_proj_norm_kernel, len(acts)),
        grid=(T // PROJ_TM,),
        in_specs=in_specs,
        out_specs=[rowblk(D), rowblk(D), rowblk(ROUTER_PAD)],
        out_shape=[jax.ShapeDtypeStruct((T, D), f32), jax.ShapeDtypeStruct((T, D), f32),
                   jax.ShapeDtypeStruct((T, ROUTER_PAD), f32)],
        compiler_params=pltpu.CompilerParams(
            dimension_semantics=("parallel",), vmem_limit_bytes=VMEM_LIMIT_BYTES),
        name="proj_norm",
    )(*acts, wo.astype(bf16), x, mod, ln_g.reshape(1, D), ln_b.reshape(1, D), w_router, b_router)


MOE_ROWS = 256
MOE_VMEM_LIMIT_BYTES = 56 * 1024 * 1024
COMBINE_ROWS = 256
SUBLANES = 8
GATHER_DMA_PRIORITY = 1


def _moe_ffn_kernel(tok_ref, be_ref, nused_ref, x_hbm, w1_ref, w3_ref, w2_ref, y_ref, xbuf, sem):
    i = pl.program_id(0)
    n_used = nused_ref[0]
    slot = lax.rem(i, 2)

    def start_gather(blk, s):
        def body(g, carry):
            for u in range(SUBLANES):
                t = tok_ref[blk * MOE_ROWS + g * SUBLANES + u]
                pltpu.make_async_copy(x_hbm.at[lax.shift_right_logical(t, 3), pl.ds(jnp.bitwise_and(t, SUBLANES - 1), 1)],
                                      xbuf.at[s, g, pl.ds(u, 1)], sem.at[s]).start(priority=GATHER_DMA_PRIORITY)
            return carry
        lax.fori_loop(0, MOE_ROWS // SUBLANES, body, 0)

    @pl.when((i == 0) & (n_used > 0))
    def _():
        start_gather(0, 0)

    @pl.when(i + 1 < n_used)
    def _():
        start_gather(i + 1, 1 - slot)

    @pl.when(i < n_used)
    def _():
        pltpu.make_async_copy(x_hbm.at[pl.ds(0, MOE_ROWS // SUBLANES)], xbuf.at[slot], sem.at[slot]).wait()
        xb = xbuf[slot].reshape(MOE_ROWS, xbuf.shape[-1]).astype(bf16)
        a = jnp.dot(xb, w1_ref[0, 0].astype(bf16), preferred_element_type=f32)
        b = jnp.dot(xb, w3_ref[0, 0].astype(bf16), preferred_element_type=f32)
        hid = (a * jax.nn.sigmoid(a) * b).astype(bf16)
        y_ref[...] = jnp.dot(hid, w2_ref[0, 0].astype(bf16), preferred_element_type=f32)

    @pl.when(i >= n_used)
    def _():
        y_ref[...] = jnp.zeros_like(y_ref)


def moe_ffn(x, buf_tok, blk_expert, n_used, w1, w3, w2, layer):
    D = x.shape[1]
    nblk = blk_expert.shape[0]
    wspec = lambda shape: pl.BlockSpec((1, 1) + shape, lambda i, tok, be, nu: (layer, be[i], 0, 0))
    return pl.pallas_call(
        _moe_ffn_kernel,
        grid_spec=pltpu.PrefetchScalarGridSpec(
            num_scalar_prefetch=3,
            grid=(nblk,),
            in_specs=[pl.BlockSpec(memory_space=pl.ANY),
                      wspec((D, EXPERT_FF)), wspec((D, EXPERT_FF)), wspec((EXPERT_FF, D))],
            out_specs=pl.BlockSpec((MOE_ROWS, D), lambda i, tok, be, nu: (i, 0)),
            scratch_shapes=[pltpu.VMEM((2, MOE_ROWS // SUBLANES, SUBLANES, D), f32), pltpu.SemaphoreType.DMA((2,))]),
        out_shape=jax.ShapeDtypeStruct((nblk * MOE_ROWS, D), f32),
        compiler_params=pltpu.CompilerParams(
            dimension_semantics=("arbitrary",), vmem_limit_bytes=MOE_VMEM_LIMIT_BYTES),
        name="moe_expert_ffn",
    )(buf_tok, blk_expert, n_used, x.reshape(x.shape[0] // SUBLANES, SUBLANES, D), w1, w3, w2)


def _moe_combine_kernel(emit_h, s0_ref, s1_ref, y_hbm, gate_ref, x_ref, mod_ref, g_ref, b_ref, *rest):
    if emit_h:
        o_ref, h_ref, buf, sem = rest
    else:
        o_ref, buf, sem = rest
    i = pl.program_id(0)
    n = pl.num_programs(0)
    slot = lax.rem(i, 2)

    def start_gather(blk, s):
        def body(g, carry):
            base = pl.multiple_of(g * SUBLANES, SUBLANES)
            for u in range(SUBLANES):
                t = blk * COMBINE_ROWS + base + u
                pltpu.make_async_copy(y_hbm.at[pl.ds(s0_ref[t], 1)], buf.at[s, 0, pl.ds(base + u, 1)], sem.at[s]).start()
                pltpu.make_async_copy(y_hbm.at[pl.ds(s1_ref[t], 1)], buf.at[s, 1, pl.ds(base + u, 1)], sem.at[s]).start()
            return carry
        lax.fori_loop(0, COMBINE_ROWS // SUBLANES, body, 0)

    @pl.when(i == 0)
    def _():
        start_gather(0, 0)

    @pl.when(i + 1 < n)
    def _():
        start_gather(i + 1, 1 - slot)

    for c in range(MOE_TOPK):
        pltpu.make_async_copy(y_hbm.at[pl.ds(0, COMBINE_ROWS)], buf.at[slot, c], sem.at[slot]).wait()
    y = buf[slot, 0] * gate_ref[:, 0:1] + buf[slot, 1] * gate_ref[:, 1:2]
    z = DEEPNORM_ALPHA * x_ref[...] + (1.0 + mod_ref[0, 0:1, :]) * y
    mu = jnp.mean(z, axis=-1, keepdims=True)
    zc = z - mu
    var = jnp.mean(zc * zc, axis=-1, keepdims=True)
    xn = zc * lax.rsqrt(var + LN_EPS) * g_ref[...] + b_ref[...]
    o_ref[...] = xn
    if emit_h:
        h_ref[...] = (xn * (1.0 + mod_ref[0, 1:2, :]) + mod_ref[0, 2:3, :]).astype(h_ref.dtype)


def moe_combine_norm(y_rows, slot0, slot1, gate, x, mod, ln_g, ln_b, S, emit_h):
    T, D = x.shape
    tiles_per_batch = S // COMBINE_ROWS
    const = lambda i, s0, s1: (0, 0)
    rowblk = pl.BlockSpec((COMBINE_ROWS, D), lambda i, s0, s1: (i, 0))
    out_specs = [rowblk, rowblk] if emit_h else [rowblk]
    out_shape = [jax.ShapeDtypeStruct((T, D), f32)] + ([jax.ShapeDtypeStruct((T, D), bf16)] if emit_h else [])
    return pl.pallas_call(
        functools.partial(_moe_combine_kernel, emit_h),
        grid_spec=pltpu.PrefetchScalarGridSpec(
            num_scalar_prefetch=2,
            grid=(T // COMBINE_ROWS,),
            in_specs=[pl.BlockSpec(memory_space=pl.ANY),
                      pl.BlockSpec((COMBINE_ROWS, MOE_TOPK), lambda i, s0, s1: (i, 0)), rowblk,
                      pl.BlockSpec((1, 3, D), lambda i, s0, s1: (i // tiles_per_batch, 0, 0)),
                      pl.BlockSpec((1, D), const), pl.BlockSpec((1, D), const)],
            out_specs=out_specs,
            scratch_shapes=[pltpu.VMEM((2, MOE_TOPK, COMBINE_ROWS, D), f32), pltpu.SemaphoreType.DMA((2,))]),
        out_shape=out_shape,
        compiler_params=pltpu.CompilerParams(
            dimension_semantics=("arbitrary",), vmem_limit_bytes=VMEM_LIMIT_BYTES),
        name="moe_combine_norm",
    )(slot0, slot1, y_rows, gate, x, mod, ln_g.reshape(1, D), ln_b.reshape(1, D))


def router_params(wg, bg, we, be):
    D = wg.shape[0]
    pad = ROUTER_PAD - MOE_GROUPS - N_EXPERTS
    w = jnp.concatenate([wg, we, jnp.zeros((D, pad), wg.dtype)], 1).astype(bf16)
    b = jnp.concatenate([bg, be, jnp.zeros((pad,), bg.dtype)]).reshape(1, ROUTER_PAD)
    return w, b


def moe_route(logits):
    T = logits.shape[0]
    g_prob = jax.nn.softmax(logits[:, :MOE_GROUPS], -1)
    g_top = jnp.argmax(g_prob, -1).astype(jnp.int32)
    p_top = jnp.max(g_prob, -1)
    e_logits = logits[:, MOE_GROUPS:MOE_GROUPS + N_EXPERTS].reshape(T, MOE_GROUPS, MOE_PER_GROUP)
    e_sel = jnp.take_along_axis(e_logits, g_top[:, None, None], axis=1)[:, 0]
    i1 = jnp.argmax(e_sel, -1).astype(jnp.int32)
    v1 = jnp.max(e_sel, -1)
    rest = jnp.where(jnp.arange(MOE_PER_GROUP, dtype=jnp.int32)[None, :] == i1[:, None], -jnp.inf, e_sel)
    i2 = jnp.argmax(rest, -1).astype(jnp.int32)
    v2 = jnp.max(rest, -1)
    gate = p_top[:, None] * jax.nn.softmax(jnp.stack([v1, v2], -1), -1)
    expert = g_top[:, None] * MOE_PER_GROUP + jnp.stack([i1, i2], -1)
    return gate, expert


def moe_layout(expert):
    T = expert.shape[0]
    A = T * MOE_TOPK
    e_flat = expert.reshape(A)
    onehot = (e_flat[:, None] == jnp.arange(N_EXPERTS, dtype=jnp.int32)[None, :]).astype(jnp.int32)
    rank = jnp.cumsum(onehot, axis=0)
    counts = rank[-1]
    padded = (counts + MOE_ROWS - 1) // MOE_ROWS * MOE_ROWS
    pad_end = jnp.cumsum(padded)
    pad_start = pad_end - padded
    slot = jnp.sum(onehot * (pad_start[None, :] + rank - 1), axis=1).astype(jnp.int32)
    tok_flat = jnp.arange(A, dtype=jnp.int32) // MOE_TOPK
    nblk = A // MOE_ROWS + N_EXPERTS
    buf_tok = jnp.zeros((nblk * MOE_ROWS,), jnp.int32).at[slot].set(tok_flat)
    n_used = (pad_end[-1] // MOE_ROWS).astype(jnp.int32)
    blk = jnp.minimum(jnp.arange(nblk, dtype=jnp.int32), n_used - 1) * MOE_ROWS
    blk_expert = jnp.minimum(jnp.sum(blk[:, None] >= pad_end[None, :], axis=1), N_EXPERTS - 1).astype(jnp.int32)
    slot2 = slot.reshape(T, MOE_TOPK)
    return buf_tok, blk_expert, n_used.reshape(1), slot2[:, 0], slot2[:, 1]


def kernel(x, c, positions, ada_w, ada_b, ln_g, ln_b, ev_win, ev_wo, ret_gn_g, ret_gn_b, rw_mu, rw_w0, rw_w2, rw_a0, rw_a2, rw_g2, rw_kk, rw_ka, rw_rk, rw_lnx_g, rw_lnx_b, rw_v0, rw_v1, rw_v2, od_win, od_wo, gla_a2, gla_ab, gla_norm_g, moe_wg, moe_bg, moe_we, moe_be, moe_w1, moe_w3, moe_w2):
    B, S, D = x.shape
    T = B * S
    depth = ada_w.shape[0]
    mods = ada_mod(c, ada_w, ada_b).reshape(depth, B, 6, D)
    h = (x * (1.0 + mods[0, :, 1])[:, None, :] + mods[0, :, 0][:, None, :]).astype(bf16).reshape(T, D)
    x = x.reshape(T, D)
    v_first = None
    for layer in range(depth):
        j = layer // 2
        mixer_mod = jnp.stack([mods[layer, :, 2], mods[layer, :, 4], mods[layer, :, 3]], axis=1)
        if layer % 2 == 0:
            p = matmul(h, even_win_layout(ev_win[j]))
            out_a = retention_core(p, positions, ret_gn_g[j], ret_gn_b[j], B, S)
            vres = None if j == 0 else (rw_v0[j - 1], rw_v1[j - 1], rw_v2[j - 1])
            out_b, v_first = rwkv_core(p, v_first, vres, rw_mu[j], rw_w0[j], rw_w2[j], rw_a0[j], rw_a2[j], rw_g2[j],
                                       rw_kk[j], rw_ka[j], rw_rk[j], rw_lnx_g[j], rw_lnx_b[j], B, S)
            acts, wo = [out_a, out_b], ev_wo[j]
        else:
            p = matmul(h, jnp.pad(od_win[j], ((0, 0), (0, ODD_PROJ_PAD - ODD_PROJ))), tn=896)
            acts, wo = [gla_core(p, gla_a2[j], gla_ab[j], gla_norm_g[j], B, S)], od_wo[j]
        w_router, b_router = router_params(moe_wg[layer], moe_bg[layer], moe_we[layer], moe_be[layer])
        x, hf, logits = proj_norm(acts, wo, x, mixer_mod, ln_g[layer, 0], ln_b[layer, 0], w_router, b_router, S)
        gate, expert = moe_route(logits)
        buf_tok, blk_expert, n_used, slot0, slot1 = moe_layout(expert)
        y_rows = moe_ffn(hf, buf_tok, blk_expert, n_used, moe_w1, moe_w3, moe_w2, layer)
        last = layer == depth - 1
        nxt = layer if last else layer + 1
        moe_mod = jnp.stack([mods[layer, :, 5], mods[nxt, :, 1], mods[nxt, :, 0]], axis=1)
        outs = moe_combine_norm(y_rows, slot0, slot1, gate, x, moe_mod, ln_g[layer, 1], ln_b[layer, 1], S, not last)
        x = outs[0]
        if not last:
            h = outs[1]
    return x.reshape(B, S, D)
```

```python
import functools

import jax
import jax.numpy as jnp
import numpy as np
from jax import lax
from jax.experimental import pallas as pl
from jax.experimental.pallas import tpu as pltpu

D_MODEL = 2048
DEPTH = 4
MIX_A = D_MODEL // 2
RET_HEAD_DIM = 256
RET_HEADS = MIX_A // RET_HEAD_DIM
RET_CHUNK = 128
RET_COLS = 4 * MIX_A
ROPE_BASE = 10000.0
MIX_B = D_MODEL - MIX_A
RWKV_HEAD_DIM = 64
RWKV_HEADS = MIX_B // RWKV_HEAD_DIM
RWKV_LORA_W = 64
RWKV_LORA_A = 64
RWKV_LORA_G = 160
RWKV_COLS = 3 * MIX_B + RWKV_LORA_W + RWKV_LORA_A + RWKV_LORA_G
RWKV_SPLITS = (MIX_B, 2 * MIX_B, 3 * MIX_B, 3 * MIX_B + RWKV_LORA_W, 3 * MIX_B + RWKV_LORA_W + RWKV_LORA_A)
RWKV_GN_EPS = 64e-5
GLA_HEADS = 4
GLA_DK = D_MODEL // 2
GLA_DV = D_MODEL
GLA_HEAD_DK = GLA_DK // GLA_HEADS
GLA_HEAD_DV = GLA_DV // GLA_HEADS
GLA_LORA = 16
GLA_GATE_NORM = 16.0
GLA_CHUNK = 64
ODD_PROJ = 2 * GLA_DK + 2 * GLA_DV + GLA_LORA
MOE_GROUPS = 4
MOE_PER_GROUP = 8
N_EXPERTS = MOE_GROUPS * MOE_PER_GROUP
MOE_TOPK = 2
EXPERT_FF = D_MODEL // 4
DEEPNORM_ALPHA = (2 * DEPTH) ** 0.25
LN_EPS = 1e-5

f32 = jnp.float32
bf16 = jnp.bfloat16

VMEM_LIMIT_BYTES = 48 * 1024 * 1024


def _dotb(a, b):
    return jnp.dot(a.astype(bf16), b.astype(bf16), preferred_element_type=f32)


def _split_dot(x, ones_bf16):
    hi = x.astype(bf16)
    lo = (x - hi.astype(f32)).astype(bf16)
    return jnp.dot(hi, ones_bf16, preferred_element_type=f32) + jnp.dot(lo, ones_bf16, preferred_element_type=f32)


def _softplus(z):
    return jnp.maximum(z, 0.0) + jnp.log1p(jnp.exp(-jnp.abs(z)))


ADA_ROWS = 8
ADA_TN = 1024


def _ada_kernel(c_ref, w_ref, b_ref, o_ref):
    c = c_ref[...]
    act = (c * jax.nn.sigmoid(c)).astype(bf16)
    o_ref[0] = jnp.dot(act, w_ref[0].astype(bf16), preferred_element_type=f32) + b_ref[0]


def ada_mod(c, ada_w, ada_b):
    B, D = c.shape
    depth, _, N = ada_w.shape
    c_pad = jnp.pad(c, ((0, ADA_ROWS - B), (0, 0)))
    out = pl.pallas_call(
        _ada_kernel,
        grid=(depth, N // ADA_TN),
        in_specs=[pl.BlockSpec((ADA_ROWS, D), lambda l, j: (0, 0)),
                  pl.BlockSpec((1, D, ADA_TN), lambda l, j: (l, 0, j)),
                  pl.BlockSpec((1, 1, ADA_TN), lambda l, j: (l, 0, j))],
        out_specs=pl.BlockSpec((1, ADA_ROWS, ADA_TN), lambda l, j: (l, 0, j)),
        out_shape=jax.ShapeDtypeStruct((depth, ADA_ROWS, N), f32),
        compiler_params=pltpu.CompilerParams(
            dimension_semantics=("parallel", "parallel"), vmem_limit_bytes=VMEM_LIMIT_BYTES),
        name="ada_mod",
    )(c_pad, ada_w, ada_b.reshape(depth, 1, N))
    return out[:, :B]


def _matmul_kernel(x_ref, w_ref, o_ref):
    o_ref[...] = jnp.dot(x_ref[...], w_ref[...], preferred_element_type=f32)


def matmul(x, w, tm=1024, tn=512):
    M, K = x.shape
    Np = w.shape[1]
    assert Np % tn == 0 and M % min(tm, M) == 0
    wb = w.astype(bf16)
    tm = min(tm, M)
    return pl.pallas_call(
        _matmul_kernel,
        grid=(M // tm, Np // tn),
        in_specs=[pl.BlockSpec((tm, K), lambda i, j: (i, 0)),
                  pl.BlockSpec((K, tn), lambda i, j: (0, j))],
        out_specs=pl.BlockSpec((tm, tn), lambda i, j: (i, j)),
        out_shape=jax.ShapeDtypeStruct((M, Np), f32),
        compiler_params=pltpu.CompilerParams(
            dimension_semantics=("parallel", "parallel"), vmem_limit_bytes=VMEM_LIMIT_BYTES),
        name="dense_matmul",
    )(x.astype(bf16), wb)


WKV_CHUNK = 64
WKV_PAIR = 2 * RWKV_HEAD_DIM
WKV_PAIRS = MIX_B // WKV_PAIR
RWKV_WA_PAD = 128
RWKV_G_PAD = 256
RWKV_V_PAD = 128
EV_R0 = RET_COLS
EV_WA0 = EV_R0 + 3 * MIX_B
EV_G0 = EV_WA0 + 2 * RWKV_WA_PAD
EVEN_COLS = EV_G0 + RWKV_G_PAD


def _rwkv_kernel(has_vres, *refs):
    if has_vres:
        (r_ref, k_ref, v_ref, wa_ref, gd_ref, vf_ref, mu_ref, muwa_ref, mug_ref, w2_ref, a2_ref, g2_ref,
         vec_ref, v1_ref, v2_ref, o_ref, h_ref, prev_ref, prevwa_ref, prevg_ref) = refs
    else:
        (r_ref, k_ref, v_ref, wa_ref, gd_ref, mu_ref, muwa_ref, mug_ref, w2_ref, a2_ref, g2_ref,
         vec_ref, o_ref, vf_ref, h_ref, prev_ref, prevwa_ref, prevg_ref) = refs
    C = WKV_CHUNK
    P2 = WKV_PAIR

    @pl.when(pl.program_id(1) == 0)
    def _():
        h_ref[...] = jnp.zeros_like(h_ref)
        prev_ref[...] = jnp.zeros_like(prev_ref)
        prevwa_ref[...] = jnp.zeros_like(prevwa_ref)
        prevg_ref[...] = jnp.zeros_like(prevg_ref)

    def shift(x, carry_ref, j, mu):
        first = lax.broadcasted_iota(jnp.int32, x.shape, 0) == 0
        prev = jnp.where(first, jnp.broadcast_to(carry_ref[j, 7:8, :], x.shape), pltpu.roll(x, 1, 0))
        carry_ref[j] = x[C - 8:, :]
        return x + (prev - x) * mu

    r_all = shift(r_ref[...], prev_ref, 0, mu_ref[0:1, :])
    k_all = shift(k_ref[...], prev_ref, 1, mu_ref[1:2, :])
    v_all = shift(v_ref[...], prev_ref, 2, mu_ref[2:3, :])
    wa = shift(wa_ref[...], prevwa_ref, 0, muwa_ref[...])
    gd = shift(gd_ref[...], prevg_ref, 0, mug_ref[...])
    w0, a0, k_k, k_a, r_k, lnx_g, lnx_b = (vec_ref[i:i + 1, :] for i in range(7))

    w_log = -_softplus(-(w0 + _dotb(jnp.tanh(wa), w2_ref[...]))) - 0.5
    lw_all = -jnp.exp(w_log)
    a_all = jax.nn.sigmoid(a0 + _dotb(wa, a2_ref[...]))
    gate_all = _dotb(jax.nn.sigmoid(gd), g2_ref[...])
    if has_vres:
        v0 = vec_ref[7:8, :]
        mix = jax.nn.sigmoid(v0 + _dotb(_dotb(v_all, v1_ref[...]), v2_ref[...]))
        v_all = v_all + (vf_ref[...] - v_all) * mix
    else:
        vf_ref[...] = v_all
    kkraw_all = k_all * k_k
    k_all = k_all * (1.0 + (a_all - 1.0) * k_a)

    row = lax.broadcasted_iota(jnp.int32, (P2, P2), 0)
    col = lax.broadcasted_iota(jnp.int32, (P2, P2), 1)
    same_head = (row >= C) == (col >= RWKV_HEAD_DIM)
    head_ones = jnp.where((row >= RWKV_HEAD_DIM) == (col >= RWKV_HEAD_DIM), 1.0, 0.0).astype(bf16)
    t_idx = jnp.bitwise_and(row, C - 1)
    s_idx = jnp.bitwise_and(col, C - 1)
    strict_lower = t_idx > s_idx
    incl_lower = t_idx >= s_idx
    eye = row == col
    tri = (lax.broadcasted_iota(jnp.int32, (C, C), 0) >= lax.broadcasted_iota(jnp.int32, (C, C), 1)).astype(bf16)
    zeros_pp = jnp.zeros((P2, P2), f32)
    nt = (((1,), (1,)), ((), ()))
    pairs = range(WKV_PAIRS)
    sl = [slice(p * P2, (p + 1) * P2) for p in pairs]

    def bd(x):
        return jnp.where(same_head, jnp.concatenate([x, x], axis=0), 0.0)

    kk = []
    for p in pairs:
        kr = kkraw_all[:, sl[p]]
        ss = _split_dot(kr * kr, head_ones)
        kk.append(kr / jnp.maximum(jnp.sqrt(ss), 1e-12))
    lw = [lw_all[:, sl[p]] for p in pairs]
    cum = []
    for p in pairs:
        hi = lw[p].astype(bf16)
        r1 = lw[p] - hi.astype(f32)
        mid = r1.astype(bf16)
        lo = (r1 - mid.astype(f32)).astype(bf16)
        cum.append(jnp.dot(tri, hi, preferred_element_type=f32) + jnp.dot(tri, mid, preferred_element_type=f32)
                   + jnp.dot(tri, lo, preferred_element_type=f32))
    tot = [cum[p][C - 1:C, :] for p in pairs]
    w_incl = [jnp.exp(cum[p]) for p in pairs]
    w_excl = [jnp.exp(cum[p] - lw[p]) for p in pairs]
    w_inv = [jnp.exp(-cum[p]) for p in pairs]
    w_rest = [jnp.exp(tot[p] - cum[p]) for p in pairs]
    r_p = [r_all[:, sl[p]] for p in pairs]
    k_p = [k_all[:, sl[p]] for p in pairs]
    v_p = [v_all[:, sl[p]] for p in pairs]
    a_p = [a_all[:, sl[p]] for p in pairs]
    at = [bd(-kk[p] * w_excl[p]) for p in pairs]
    rt = [bd(r_p[p] * w_incl[p]) for p in pairs]
    bt = [bd(kk[p] * a_p[p] * w_inv[p]) for p in pairs]
    kt = [bd(k_p[p] * w_inv[p]) for p in pairs]
    bh = [bd(kk[p] * a_p[p] * w_rest[p]) for p in pairs]
    kh = [bd(k_p[p] * w_rest[p]) for p in pairs]
    vb = [bd(v_p[p]) for p in pairs]

    scores = [lax.dot_general(jnp.concatenate([at[p], rt[p]], axis=0).astype(bf16),
                              jnp.concatenate([bt[p], kt[p]], axis=0).astype(bf16),
                              nt, preferred_element_type=f32) for p in pairs]
    a_ab = [jnp.where(strict_lower, scores[p][:P2, :P2], 0.0) for p in pairs]
    a_ak = [jnp.where(strict_lower, scores[p][:P2, P2:], 0.0) for p in pairs]
    a_rb = [jnp.where(incl_lower, scores[p][P2:, :P2], 0.0) for p in pairs]
    a_rk = [jnp.where(incl_lower, scores[p][P2:, P2:], 0.0) for p in pairs]
    x0 = [jnp.concatenate([at[p], _dotb(a_ak[p], vb[p])], axis=1) for p in pairs]
    eye_f = jnp.where(eye, 1.0, 0.0)
    inv = [eye_f + a_ab[p] for p in pairs]
    pw = [_dotb(a_ab[p], a_ab[p]) for p in pairs]
    for _ in range(int(np.log2(C)) - 2):
        both = [_dotb(pw[p], jnp.concatenate([pw[p], inv[p]], axis=1)) for p in pairs]
        inv = [inv[p] + both[p][:, P2:] for p in pairs]
        pw = [both[p][:, :P2] for p in pairs]
    inv = [inv[p] + _dotb(pw[p], inv[p]) for p in pairs]
    x = [_dotb(inv[p], x0[p]) for p in pairs]
    rhs = [jnp.concatenate([x[p], jnp.concatenate([zeros_pp, vb[p]], axis=1)], axis=0) for p in pairs]
    gz = [_dotb(jnp.concatenate([a_rb[p], a_rk[p]], axis=1), rhs[p]) for p in pairs]
    mn = [_dotb(jnp.concatenate([bh[p], kh[p]], axis=0).T, rhs[p]) for p in pairs]
    yh = []
    for p in pairs:
        g = rt[p] + gz[p][:, :P2]
        m = mn[p][:, :P2] + jnp.where(eye, jnp.broadcast_to(jnp.exp(tot[p]), (P2, P2)), 0.0)
        yh.append(_dotb(jnp.concatenate([g, m], axis=0), h_ref[p]))
    for p in pairs:
        y2 = yh[p][:P2] + gz[p][:, P2:]
        h_ref[p] = yh[p][P2:] + mn[p][:, P2:]
        y = y2[:C] + y2[C:]
        mean = _split_dot(y, head_ones) * (1.0 / RWKV_HEAD_DIM)
        yc = y - mean
        var = _split_dot(yc * yc, head_ones) * (1.0 / RWKV_HEAD_DIM)
        yn = yc * lax.rsqrt(var + RWKV_GN_EPS) * lnx_g[:, sl[p]] + lnx_b[:, sl[p]]
        bonus = _split_dot(r_p[p] * k_p[p] * r_k[:, sl[p]], head_ones) * v_p[p]
        o_ref[:, sl[p]] = ((yn + bonus) * gate_all[:, sl[p]]).astype(o_ref.dtype)


def rwkv_core(p, v_first, vres, mu, w0, w2, a0, a2, g2, k_k, k_a, r_k, lnx_g, lnx_b, B, S):
    C = WKV_CHUNK
    n = S // C
    has_vres = vres is not None
    r_mu, k_mu, v_mu, wd_mu, ad_mu, gd_mu = jnp.split(mu, RWKV_SPLITS)
    mu3 = jnp.stack([r_mu, k_mu, v_mu])
    mu_wa = jnp.concatenate([wd_mu, ad_mu]).reshape(1, RWKV_WA_PAD)
    mu_g = jnp.pad(gd_mu, (0, RWKV_G_PAD - RWKV_LORA_G)).reshape(1, RWKV_G_PAD)
    w2p = jnp.pad(w2, ((0, RWKV_WA_PAD - RWKV_LORA_W), (0, 0))).astype(bf16)
    a2p = jnp.pad(a2, ((RWKV_LORA_W, 0), (0, 0))).astype(bf16)
    g2p = jnp.pad(g2, ((0, RWKV_G_PAD - RWKV_LORA_G), (0, 0))).astype(bf16)
    vecs = [w0, a0, k_k, k_a, r_k.reshape(MIX_B), lnx_g, lnx_b]
    if has_vres:
        vecs.append(vres[0])
    vec = jnp.stack(vecs)
    nv = vec.shape[0]
    row = lambda bi, ci: bi * n + ci
    const = lambda bi, ci: (0, 0)
    tile = lambda w, j: pl.BlockSpec((C, w), lambda bi, ci: (row(bi, ci), j))
    in_specs = [tile(MIX_B, EV_R0 // MIX_B), tile(MIX_B, EV_R0 // MIX_B + 1), tile(MIX_B, EV_R0 // MIX_B + 2),
                tile(RWKV_WA_PAD, EV_WA0 // RWKV_WA_PAD), tile(RWKV_G_PAD, EV_G0 // RWKV_G_PAD)]
    args = [p, p, p, p, p]
    if has_vres:
        in_specs.append(tile(MIX_B, 0))
        args.append(v_first)
    in_specs += [pl.BlockSpec((3, MIX_B), const), pl.BlockSpec((1, RWKV_WA_PAD), const),
                 pl.BlockSpec((1, RWKV_G_PAD), const), pl.BlockSpec((RWKV_WA_PAD, MIX_B), const),
                 pl.BlockSpec((RWKV_WA_PAD, MIX_B), const), pl.BlockSpec((RWKV_G_PAD, MIX_B), const),
                 pl.BlockSpec((nv, MIX_B), const)]
    args += [mu3, mu_wa, mu_g, w2p, a2p, g2p, vec]
    out_specs = [tile(MIX_B, 0)]
    out_shape = [jax.ShapeDtypeStruct((B * S, MIX_B), bf16)]
    if has_vres:
        v1p = jnp.pad(vres[1], ((0, 0), (0, RWKV_V_PAD - vres[1].shape[1]))).astype(bf16)
        v2p = jnp.pad(vres[2], ((0, RWKV_V_PAD - vres[2].shape[0]), (0, 0))).astype(bf16)
        in_specs += [pl.BlockSpec((MIX_B, RWKV_V_PAD), const), pl.BlockSpec((RWKV_V_PAD, MIX_B), const)]
        args += [v1p, v2p]
    else:
        out_specs.append(tile(MIX_B, 0))
        out_shape.append(jax.ShapeDtypeStruct((B * S, MIX_B), f32))
    outs = pl.pallas_call(
        functools.partial(_rwkv_kernel, has_vres),
        grid=(B, n),
        in_specs=in_specs,
        out_specs=out_specs,
        out_shape=out_shape,
        scratch_shapes=[pltpu.VMEM((WKV_PAIRS, WKV_PAIR, WKV_PAIR), f32),
                        pltpu.VMEM((3, 8, MIX_B), f32), pltpu.VMEM((1, 8, RWKV_WA_PAD), f32),
                        pltpu.VMEM((1, 8, RWKV_G_PAD), f32)],
        compiler_params=pltpu.CompilerParams(
            dimension_semantics=("parallel", "arbitrary"), vmem_limit_bytes=VMEM_LIMIT_BYTES),
        name="rwkv7_time_mix",
    )(*args)
    return (outs[0], v_first) if has_vres else (outs[0], outs[1])


def even_win_layout(win):
    ret, rw = win[:, :RET_COLS], win[:, RET_COLS:]
    rkv = rw[:, :RWKV_SPLITS[2]]
    wd, ad, gd = rw[:, RWKV_SPLITS[2]:RWKV_SPLITS[3]], rw[:, RWKV_SPLITS[3]:RWKV_SPLITS[4]], rw[:, RWKV_SPLITS[4]:]
    z = lambda n: jnp.zeros((win.shape[0], n), win.dtype)
    return jnp.concatenate([ret, rkv, wd, ad, z(RWKV_WA_PAD), gd, z(RWKV_G_PAD - RWKV_LORA_G)], axis=1)


RET_LOG_GAMMA = tuple(float(np.log1p(-np.exp2(-5.0 - h))) for h in range(RET_HEADS))
ROPE_HALF = RET_HEAD_DIM // 2


def _ret_kernel(q_ref, k_ref, v_ref, g_ref, pos_ref, inv_ref, gng_ref, gnb_ref, o_ref, s_ref):
    C = RET_CHUNK
    d = RET_HEAD_DIM

    @pl.when(pl.program_id(1) == 0)
    def _():
        s_ref[...] = jnp.zeros_like(s_ref)

    ang = pos_ref[...] * inv_ref[...]
    cos = jnp.cos(ang)
    sin = jnp.sin(ang)

    def rot(x):
        x1, x2 = x[:, :ROPE_HALF], x[:, ROPE_HALF:]
        return jnp.concatenate([x1 * cos - x2 * sin, x1 * sin + x2 * cos], axis=-1)

    ri = lax.broadcasted_iota(jnp.int32, (C, C), 0)
    ci = lax.broadcasted_iota(jnp.int32, (C, C), 1)
    diff = (ri - ci).astype(f32)
    row = lax.broadcasted_iota(jnp.int32, (C, d), 0).astype(f32)
    nt = (((1,), (1,)), ((), ()))

    for h in range(RET_HEADS):
        hs = slice(h * d, (h + 1) * d)
        lg = RET_LOG_GAMMA[h]
        q = rot(q_ref[:, hs])
        k = rot(k_ref[:, hs]) * (d ** -0.5)
        v = v_ref[:, hs].astype(bf16)
        decay = jnp.where(diff >= 0, jnp.exp(lg * jnp.maximum(diff, 0.0)), 0.0)
        scores = lax.dot_general(q.astype(bf16), k.astype(bf16), nt, preferred_element_type=f32) * decay
        state = s_ref[h]
        y = jnp.dot(scores.astype(bf16), v, preferred_element_type=f32)
        y = y + jnp.dot((q * jnp.exp(lg * (row + 1.0))).astype(bf16), state.astype(bf16), preferred_element_type=f32)
        kw = (k * jnp.exp(lg * (C - 1.0 - row))).astype(bf16)
        s_ref[h] = state * float(np.exp(lg * C)) + jnp.dot(kw.T, v, preferred_element_type=f32)
        mu = jnp.mean(y, axis=-1, keepdims=True)
        yc = y - mu
        var = jnp.mean(yc * yc, axis=-1, keepdims=True)
        yn = yc * lax.rsqrt(var + LN_EPS) * gng_ref[:, hs] + gnb_ref[:, hs]
        g = g_ref[:, hs]
        o_ref[:, hs] = (g * jax.nn.sigmoid(g) * yn).astype(o_ref.dtype)


def retention_core(p, positions, gn_g, gn_b, B, S):
    C = RET_CHUNK
    n = S // C
    inv = (ROPE_BASE ** (-jnp.arange(ROPE_HALF, dtype=f32) / ROPE_HALF)).reshape(1, ROPE_HALF)
    pos = positions.astype(f32).reshape(B * S, 1)
    row = lambda bi, ci: bi * n + ci
    const = lambda bi, ci: (0, 0)
    col = lambda j: pl.BlockSpec((C, MIX_A), lambda bi, ci: (row(bi, ci), j))
    return pl.pallas_call(
        _ret_kernel,
        grid=(B, n),
        in_specs=[col(0), col(1), col(2), col(3),
                  pl.BlockSpec((C, 1), lambda bi, ci: (row(bi, ci), 0)),
                  pl.BlockSpec((1, ROPE_HALF), const),
                  pl.BlockSpec((1, MIX_A), const),
                  pl.BlockSpec((1, MIX_A), const)],
        out_specs=pl.BlockSpec((C, MIX_A), lambda bi, ci: (row(bi, ci), 0)),
        out_shape=jax.ShapeDtypeStruct((B * S, MIX_A), bf16),
        scratch_shapes=[pltpu.VMEM((RET_HEADS, RET_HEAD_DIM, RET_HEAD_DIM), f32)],
        compiler_params=pltpu.CompilerParams(
            dimension_semantics=("parallel", "arbitrary"), vmem_limit_bytes=VMEM_LIMIT_BYTES),
        name="retention_chunked",
    )(p, p, p, p, pos, inv, gn_g.reshape(1, MIX_A), gn_b.reshape(1, MIX_A))


GLA_LEVELS = int(np.log2(GLA_CHUNK))
GLA_DECAY_ROWS = (2 + GLA_LEVELS) * GLA_CHUNK
GLA_AD_PAD = 128
ODD_PROJ_PAD = 2 * GLA_DK + 2 * GLA_DV + GLA_AD_PAD


def _gla_decay_operator():
    C = GLA_CHUNK
    i = np.arange(C)[:, None]
    j = np.arange(C)[None, :]
    blocks = [(j <= i), (j > i)]
    for lvl in range(GLA_LEVELS):
        s = C >> (lvl + 1)
        mid = (i // (2 * s)) * (2 * s) + s
        right = i >= mid
        blocks.append(np.where(right, (j >= mid) & (j <= i), (j > i) & (j < mid)))
    return np.concatenate(blocks, 0).astype(np.float32)


def _gla_kernel(q_ref, k_ref, v_ref, r_ref, ad_ref, a2_ref, ab_ref, g_ref, lop_ref, o_ref, s_ref):
    C = GLA_CHUNK

    @pl.when(pl.program_id(1) == 0)
    def _():
        s_ref[...] = jnp.zeros_like(s_ref)

    x = jnp.dot(ad_ref[...].astype(bf16), a2_ref[...], preferred_element_type=f32) + ab_ref[...]
    la = (jnp.minimum(x, 0.0) - jnp.log1p(jnp.exp(-jnp.abs(x)))) * (1.0 / GLA_GATE_NORM)
    hi = la.astype(bf16)
    r1 = la - hi.astype(f32)
    mid = r1.astype(bf16)
    lo = (r1 - mid.astype(f32)).astype(bf16)
    lop = lop_ref[...]
    dec = (jnp.dot(lop, hi, preferred_element_type=f32) + jnp.dot(lop, mid, preferred_element_type=f32)
           + jnp.dot(lop, lo, preferred_element_type=f32))

    ri = lax.broadcasted_iota(jnp.int32, (C, C), 0)
    ci = lax.broadcasted_iota(jnp.int32, (C, C), 1)
    level_masks = []
    for lvl in range(GLA_LEVELS):
        s = C >> (lvl + 1)
        same_block = (ri // (2 * s)) == (ci // (2 * s))
        level_masks.append(same_block & ((ri % (2 * s)) >= s) & ((ci % (2 * s)) < s))
    eye = ri == ci
    nt = (((1,), (1,)), ((), ()))

    for h in range(GLA_HEADS):
        ks = slice(h * GLA_HEAD_DK, (h + 1) * GLA_HEAD_DK)
        vs = slice(h * GLA_HEAD_DV, (h + 1) * GLA_HEAD_DV)
        q = q_ref[:, ks] * (GLA_HEAD_DK ** -0.5)
        k = k_ref[:, ks]
        v = v_ref[:, vs].astype(bf16)
        e = jnp.exp(dec[:, ks])
        state = s_ref[h]
        o = lax.dot_general((q * e[:C]).astype(bf16), state.astype(bf16), nt, preferred_element_type=f32)
        attn = jnp.where(eye, lax.dot_general(q.astype(bf16), k.astype(bf16), nt, preferred_element_type=f32), 0.0)
        for lvl in range(GLA_LEVELS):
            el = e[(2 + lvl) * C:(3 + lvl) * C]
            sc = lax.dot_general((q * el).astype(bf16), (k * el).astype(bf16), nt, preferred_element_type=f32)
            attn = attn + jnp.where(level_masks[lvl], sc, 0.0)
        o = o + jnp.dot(attn.astype(bf16), v, preferred_element_type=f32)
        kd = (k * e[C:2 * C]).astype(bf16)
        s_ref[h] = state * e[C - 1:C] + jnp.dot(v.T, kd, preferred_element_type=f32)
        o = o * lax.rsqrt(jnp.mean(o * o, axis=-1, keepdims=True) + LN_EPS) * g_ref[:, vs]
        rr = r_ref[:, vs]
        o_ref[:, vs] = (rr * jax.nn.sigmoid(rr) * o).astype(o_ref.dtype)


def gla_core(p, a2, ab, norm_g, B, S):
    C = GLA_CHUNK
    n = S // C
    a2p = jnp.pad(a2, ((0, GLA_AD_PAD - GLA_LORA), (0, 0))).astype(bf16)
    lop = jnp.asarray(_gla_decay_operator(), dtype=bf16)
    row = lambda bi, ci: bi * n + ci
    const = lambda bi, ci: (0, 0)
    return pl.pallas_call(
        _gla_kernel,
        grid=(B, n),
        in_specs=[pl.BlockSpec((C, GLA_DK), lambda bi, ci: (row(bi, ci), 0)),
                  pl.BlockSpec((C, GLA_DK), lambda bi, ci: (row(bi, ci), 1)),
                  pl.BlockSpec((C, GLA_DV), lambda bi, ci: (row(bi, ci), 1)),
                  pl.BlockSpec((C, GLA_DV), lambda bi, ci: (row(bi, ci), 2)),
                  pl.BlockSpec((C, GLA_AD_PAD), lambda bi, ci: (row(bi, ci), (2 * GLA_DK + 2 * GLA_DV) // GLA_AD_PAD)),
                  pl.BlockSpec((GLA_AD_PAD, GLA_DK), const),
                  pl.BlockSpec((1, GLA_DK), const),
                  pl.BlockSpec((1, GLA_DV), const),
                  pl.BlockSpec((GLA_DECAY_ROWS, C), const)],
        out_specs=pl.BlockSpec((C, GLA_DV), lambda bi, ci: (row(bi, ci), 0)),
        out_shape=jax.ShapeDtypeStruct((B * S, GLA_DV), bf16),
        scratch_shapes=[pltpu.VMEM((GLA_HEADS, GLA_HEAD_DV, GLA_HEAD_DK), f32)],
        compiler_params=pltpu.CompilerParams(
            dimension_semantics=("parallel", "arbitrary"), vmem_limit_bytes=VMEM_LIMIT_BYTES),
        name="gla_chunked",
    )(p, p, p, p, p, a2p, ab.reshape(1, GLA_DK), norm_g.reshape(1, GLA_DV), lop)


PROJ_TM = 256
ROUTER_PAD = 128


def _proj_norm_kernel(n_in, *refs):
    a_refs = refs[:n_in]
    w_ref, x_ref, mod_ref, g_ref, b_ref, wr_ref, br_ref, xo_ref, ho_ref, lo_ref = refs[n_in:]
    acc = None
    k0 = 0
    for a_ref in a_refs:
        kw = a_ref.shape[1]
        part = jnp.dot(a_ref[...], w_ref[k0:k0 + kw, :], preferred_element_type=f32)
        acc = part if acc is None else acc + part
        k0 += kw
    gt, sc, sh = mod_ref[0, 0:1, :], mod_ref[0, 1:2, :], mod_ref[0, 2:3, :]
    z = DEEPNORM_ALPHA * x_ref[...] + (1.0 + gt) * acc
    mu = jnp.mean(z, axis=-1, keepdims=True)
    zc = z - mu
    var = jnp.mean(zc * zc, axis=-1, keepdims=True)
    xn = zc * lax.rsqrt(var + LN_EPS) * g_ref[...] + b_ref[...]
    xo_ref[...] = xn
    hf = xn * (1.0 + sc) + sh
    ho_ref[...] = hf
    lo_ref[...] = jnp.dot(hf.astype(bf16), wr_ref[...], preferred_element_type=f32) + br_ref[...]


def proj_norm(acts, wo, x, mod, ln_g, ln_b, w_router, b_router, S):
    T, D = x.shape
    tiles_per_batch = S // PROJ_TM
    const = lambda i: (0, 0)
    rowblk = lambda w: pl.BlockSpec((PROJ_TM, w), lambda i: (i, 0))
    in_specs = [rowblk(a.shape[1]) for a in acts]
    in_specs += [pl.BlockSpec(wo.shape, const), rowblk(D),
                 pl.BlockSpec((1, 3, D), lambda i: (i // tiles_per_batch, 0, 0)),
                 pl.BlockSpec((1, D), const), pl.BlockSpec((1, D), const),
                 pl.BlockSpec((D, ROUTER_PAD), const), pl.BlockSpec((1, ROUTER_PAD), const)]
    return pl.pallas_call(
        functools.partial(_proj_norm_kernel, len(acts)),
        grid=(T // PROJ_TM,),
        in_specs=in_specs,
        out_specs=[rowblk(D), rowblk(D), rowblk(ROUTER_PAD)],
        out_shape=[jax.ShapeDtypeStruct((T, D), f32), jax.ShapeDtypeStruct((T, D), f32),
                   jax.ShapeDtypeStruct((T, ROUTER_PAD), f32)],
        compiler_params=pltpu.CompilerParams(
            dimension_semantics=("parallel",), vmem_limit_bytes=VMEM_LIMIT_BYTES),
        name="proj_norm",
    )(*acts, wo.astype(bf16), x, mod, ln_g.reshape(1, D), ln_b.reshape(1, D), w_router, b_router)


MOE_ROWS = 256
MOE_VMEM_LIMIT_BYTES = 56 * 1024 * 1024
COMBINE_ROWS = 256
SUBLANES = 8
GATHER_DMA_PRIORITY = 1


def _moe_ffn_kernel(layer, tok_ref, be_ref, nused_ref, x_hbm, w1_hbm, w3_hbm, w2_hbm, y_ref,
                    xbuf, w1buf, w3buf, w2buf, sem, wsem, run_ref):
    i = pl.program_id(0)
    nblk = pl.num_programs(0)
    n_used = nused_ref[0]
    slot = lax.rem(i, 2)
    expert = be_ref[i]
    first = (i < n_used) & ((i == 0) | (expert != be_ref[jnp.maximum(i - 1, 0)]))

    def start_gather(blk, s):
        def body(g, carry):
            for u in range(SUBLANES):
                t = tok_ref[blk * MOE_ROWS + g * SUBLANES + u]
                pltpu.make_async_copy(x_hbm.at[lax.shift_right_logical(t, 3), pl.ds(jnp.bitwise_and(t, SUBLANES - 1), 1)],
                                      xbuf.at[s, g, pl.ds(u, 1)], sem.at[s]).start(priority=GATHER_DMA_PRIORITY)
            return carry
        lax.fori_loop(0, MOE_ROWS // SUBLANES, body, 0)

    def weight_copies(e, b):
        return [pltpu.make_async_copy(w_hbm.at[layer, e], wbuf.at[b], wsem.at[b])
                for w_hbm, wbuf in ((w1_hbm, w1buf), (w3_hbm, w3buf), (w2_hbm, w2buf))]

    @pl.when((i == 0) & (n_used > 0))
    def _():
        run_ref[0] = 0
        for cp in weight_copies(expert, 0):
            cp.start()
        start_gather(0, 0)

    @pl.when(i + 1 < n_used)
    def _():
        start_gather(i + 1, 1 - slot)

    @pl.when(first)
    def _():
        @pl.when(i > 0)
        def _():
            run_ref[0] = run_ref[0] + 1
        buf = jnp.bitwise_and(run_ref[0], 1)
        nxt = lax.while_loop(lambda j: (j < n_used) & (be_ref[jnp.minimum(j, nblk - 1)] == expert),
                             lambda j: j + 1, i + 1)

        @pl.when(nxt < n_used)
        def _():
            for cp in weight_copies(be_ref[jnp.minimum(nxt, nblk - 1)], 1 - buf):
                cp.start()
        for cp in weight_copies(expert, buf):
            cp.wait()

    @pl.when(i < n_used)
    def _():
        buf = jnp.bitwise_and(run_ref[0], 1)
        pltpu.make_async_copy(x_hbm.at[pl.ds(0, MOE_ROWS // SUBLANES)], xbuf.at[slot], sem.at[slot]).wait()
        xb = xbuf[slot].reshape(MOE_ROWS, xbuf.shape[-1]).astype(bf16)
        a = jnp.dot(xb, w1buf[buf].astype(bf16), preferred_element_type=f32)
        b = jnp.dot(xb, w3buf[buf].astype(bf16), preferred_element_type=f32)
        hid = (a * jax.nn.sigmoid(a) * b).astype(bf16)
        y_ref[...] = jnp.dot(hid, w2buf[buf].astype(bf16), preferred_element_type=f32)

    @pl.when(i >= n_used)
    def _():
        y_ref[...] = jnp.zeros_like(y_ref)


def moe_ffn(x, buf_tok, blk_expert, n_used, w1, w3, w2, layer):
    D = x.shape[1]
    nblk = blk_expert.shape[0]
    any_spec = pl.BlockSpec(memory_space=pl.ANY)
    return pl.pallas_call(
        functools.partial(_moe_ffn_kernel, layer),
        grid_spec=pltpu.PrefetchScalarGridSpec(
            num_scalar_prefetch=3,
            grid=(nblk,),
            in_specs=[any_spec, any_spec, any_spec, any_spec],
            out_specs=pl.BlockSpec((MOE_ROWS, D), lambda i, tok, be, nu: (i, 0)),
            scratch_shapes=[pltpu.VMEM((2, MOE_ROWS // SUBLANES, SUBLANES, D), f32),
                            pltpu.VMEM((2, D, EXPERT_FF), f32), pltpu.VMEM((2, D, EXPERT_FF), f32),
                            pltpu.VMEM((2, EXPERT_FF, D), f32),
                            pltpu.SemaphoreType.DMA((2,)), pltpu.SemaphoreType.DMA((2,)),
                            pltpu.SMEM((1,), jnp.int32)]),
        out_shape=jax.ShapeDtypeStruct((nblk * MOE_ROWS, D), f32),
        compiler_params=pltpu.CompilerParams(
            dimension_semantics=("arbitrary",), vmem_limit_bytes=MOE_VMEM_LIMIT_BYTES),
        name="moe_expert_ffn",
    )(buf_tok, blk_expert, n_used, x.reshape(x.shape[0] // SUBLANES, SUBLANES, D), w1, w3, w2)


def _moe_combine_kernel(emit_h, s0_ref, s1_ref, y_hbm, gate_ref, x_ref, mod_ref, g_ref, b_ref, *rest):
    if emit_h:
        o_ref, h_ref, buf, sem = rest
    else:
        o_ref, buf, sem = rest
    i = pl.program_id(0)
    n = pl.num_programs(0)
    slot = lax.rem(i, 2)

    def start_gather(blk, s):
        def body(g, carry):
            base = pl.multiple_of(g * SUBLANES, SUBLANES)
            for u in range(SUBLANES):
                t = blk * COMBINE_ROWS + base + u
                pltpu.make_async_copy(y_hbm.at[pl.ds(s0_ref[t], 1)], buf.at[s, 0, pl.ds(base + u, 1)], sem.at[s]).start()
                pltpu.make_async_copy(y_hbm.at[pl.ds(s1_ref[t], 1)], buf.at[s, 1, pl.ds(base + u, 1)], sem.at[s]).start()
            return carry
        lax.fori_loop(0, COMBINE_ROWS // SUBLANES, body, 0)

    @pl.when(i == 0)
    def _():
        start_gather(0, 0)

    @pl.when(i + 1 < n)
    def _():
        start_gather(i + 1, 1 - slot)

    for c in range(MOE_TOPK):
        pltpu.make_async_copy(y_hbm.at[pl.ds(0, COMBINE_ROWS)], buf.at[slot, c], sem.at[slot]).wait()
    y = buf[slot, 0] * gate_ref[:, 0:1] + buf[slot, 1] * gate_ref[:, 1:2]
    z = DEEPNORM_ALPHA * x_ref[...] + (1.0 + mod_ref[0, 0:1, :]) * y
    mu = jnp.mean(z, axis=-1, keepdims=True)
    zc = z - mu
    var = jnp.mean(zc * zc, axis=-1, keepdims=True)
    xn = zc * lax.rsqrt(var + LN_EPS) * g_ref[...] + b_ref[...]
    o_ref[...] = xn
    if emit_h:
        h_ref[...] = (xn * (1.0 + mod_ref[0, 1:2, :]) + mod_ref[0, 2:3, :]).astype(h_ref.dtype)


def moe_combine_norm(y_rows, slot0, slot1, gate, x, mod, ln_g, ln_b, S, emit_h):
    T, D = x.shape
    tiles_per_batch = S // COMBINE_ROWS
    const = lambda i, s0, s1: (0, 0)
    rowblk = pl.BlockSpec((COMBINE_ROWS, D), lambda i, s0, s1: (i, 0))
    out_specs = [rowblk, rowblk] if emit_h else [rowblk]
    out_shape = [jax.ShapeDtypeStruct((T, D), f32)] + ([jax.ShapeDtypeStruct((T, D), bf16)] if emit_h else [])
    return pl.pallas_call(
        functools.partial(_moe_combine_kernel, emit_h),
        grid_spec=pltpu.PrefetchScalarGridSpec(
            num_scalar_prefetch=2,
            grid=(T // COMBINE_ROWS,),
            in_specs=[pl.BlockSpec(memory_space=pl.ANY),
                      pl.BlockSpec((COMBINE_ROWS, MOE_TOPK), lambda i, s0, s1: (i, 0)), rowblk,
                      pl.BlockSpec((1, 3, D), lambda i, s0, s1: (i // tiles_per_batch, 0, 0)),
                      pl.BlockSpec((1, D), const), pl.BlockSpec((1, D), const)],
            out_specs=out_specs,
            scratch_shapes=[pltpu.VMEM((2, MOE_TOPK, COMBINE_ROWS, D), f32), pltpu.SemaphoreType.DMA((2,))]),
        out_shape=out_shape,
        compiler_params=pltpu.CompilerParams(
            dimension_semantics=("arbitrary",), vmem_limit_bytes=VMEM_LIMIT_BYTES),
        name="moe_combine_norm",
    )(slot0, slot1, y_rows, gate, x, mod, ln_g.reshape(1, D), ln_b.reshape(1, D))


def router_params(wg, bg, we, be):
    D = wg.shape[0]
    pad = ROUTER_PAD - MOE_GROUPS - N_EXPERTS
    w = jnp.concatenate([wg, we, jnp.zeros((D, pad), wg.dtype)], 1).astype(bf16)
    b = jnp.concatenate([bg, be, jnp.zeros((pad,), bg.dtype)]).reshape(1, ROUTER_PAD)
    return w, b


def moe_route(logits):
    T = logits.shape[0]
    g_prob = jax.nn.softmax(logits[:, :MOE_GROUPS], -1)
    g_top = jnp.argmax(g_prob, -1).astype(jnp.int32)
    p_top = jnp.max(g_prob, -1)
    e_logits = logits[:, MOE_GROUPS:MOE_GROUPS + N_EXPERTS].reshape(T, MOE_GROUPS, MOE_PER_GROUP)
    e_sel = jnp.take_along_axis(e_logits, g_top[:, None, None], axis=1)[:, 0]
    i1 = jnp.argmax(e_sel, -1).astype(jnp.int32)
    v1 = jnp.max(e_sel, -1)
    rest = jnp.where(jnp.arange(MOE_PER_GROUP, dtype=jnp.int32)[None, :] == i1[:, None], -jnp.inf, e_sel)
    i2 = jnp.argmax(rest, -1).astype(jnp.int32)
    v2 = jnp.max(rest, -1)
    gate = p_top[:, None] * jax.nn.softmax(jnp.stack([v1, v2], -1), -1)
    expert = g_top[:, None] * MOE_PER_GROUP + jnp.stack([i1, i2], -1)
    return gate, expert


def moe_layout(expert):
    T = expert.shape[0]
    A = T * MOE_TOPK
    e_flat = expert.reshape(A)
    onehot = (e_flat[:, None] == jnp.arange(N_EXPERTS, dtype=jnp.int32)[None, :]).astype(jnp.int32)
    rank = jnp.cumsum(onehot, axis=0)
    counts = rank[-1]
    padded = (counts + MOE_ROWS - 1) // MOE_ROWS * MOE_ROWS
    pad_end = jnp.cumsum(padded)
    pad_start = pad_end - padded
    slot = jnp.sum(onehot * (pad_start[None, :] + rank - 1), axis=1).astype(jnp.int32)
    tok_flat = jnp.arange(A, dtype=jnp.int32) // MOE_TOPK
    nblk = A // MOE_ROWS + N_EXPERTS
    buf_tok = jnp.zeros((nblk * MOE_ROWS,), jnp.int32).at[slot].set(tok_flat)
    n_used = (pad_end[-1] // MOE_ROWS).astype(jnp.int32)
    blk = jnp.minimum(jnp.arange(nblk, dtype=jnp.int32), n_used - 1) * MOE_ROWS
    blk_expert = jnp.minimum(jnp.sum(blk[:, None] >= pad_end[None, :], axis=1), N_EXPERTS - 1).astype(jnp.int32)
    slot2 = slot.reshape(T, MOE_TOPK)
    return buf_tok, blk_expert, n_used.reshape(1), slot2[:, 0], slot2[:, 1]


def kernel(x, c, positions, ada_w, ada_b, ln_g, ln_b, ev_win, ev_wo, ret_gn_g, ret_gn_b, rw_mu, rw_w0, rw_w2, rw_a0, rw_a2, rw_g2, rw_kk, rw_ka, rw_rk, rw_lnx_g, rw_lnx_b, rw_v0, rw_v1, rw_v2, od_win, od_wo, gla_a2, gla_ab, gla_norm_g, moe_wg, moe_bg, moe_we, moe_be, moe_w1, moe_w3, moe_w2):
    B, S, D = x.shape
    T = B * S
    depth = ada_w.shape[0]
    mods = ada_mod(c, ada_w, ada_b).reshape(depth, B, 6, D)
    h = (x * (1.0 + mods[0, :, 1])[:, None, :] + mods[0, :, 0][:, None, :]).astype(bf16).reshape(T, D)
    x = x.reshape(T, D)
    v_first = None
    for layer in range(depth):
        j = layer // 2
        mixer_mod = jnp.stack([mods[layer, :, 2], mods[layer, :, 4], mods[layer, :, 3]], axis=1)
        if layer % 2 == 0:
            p = matmul(h, even_win_layout(ev_win[j]))
            out_a = retention_core(p, positions, ret_gn_g[j], ret_gn_b[j], B, S)
            vres = None if j == 0 else (rw_v0[j - 1], rw_v1[j - 1], rw_v2[j - 1])
            out_b, v_first = rwkv_core(p, v_first, vres, rw_mu[j], rw_w0[j], rw_w2[j], rw_a0[j], rw_a2[j], rw_g2[j],
                                       rw_kk[j], rw_ka[j], rw_rk[j], rw_lnx_g[j], rw_lnx_b[j], B, S)
            acts, wo = [out_a, out_b], ev_wo[j]
        else:
            p = matmul(h, jnp.pad(od_win[j], ((0, 0), (0, ODD_PROJ_PAD - ODD_PROJ))), tn=896)
            acts, wo = [gla_core(p, gla_a2[j], gla_ab[j], gla_norm_g[j], B, S)], od_wo[j]
        w_router, b_router = router_params(moe_wg[layer], moe_bg[layer], moe_we[layer], moe_be[layer])
        x, hf, logits = proj_norm(acts, wo, x, mixer_mod, ln_g[layer, 0], ln_b[layer, 0], w_router, b_router, S)
        gate, expert = moe_route(logits)
        buf_tok, blk_expert, n_used, slot0, slot1 = moe_layout(expert)
        y_rows = moe_ffn(hf, buf_tok, blk_expert, n_used, moe_w1, moe_w3, moe_w2, layer)
        last = layer == depth - 1
        nxt = layer if last else layer + 1
        moe_mod = jnp.stack([mods[layer, :, 5], mods[nxt, :, 1], mods[nxt, :, 0]], axis=1)
        outs = moe_combine_norm(y_rows, slot0, slot1, gate, x, moe_mod, ln_g[layer, 1], ln_b[layer, 1], S, not last)
        x = outs[0]
        if not last:
            h = outs[1]
    return x.reshape(B, S, D)
```

```python
import functools

import jax
import jax.numpy as jnp
import numpy as np
from jax import lax
from jax.experimental import pallas as pl
from jax.experimental.pallas import tpu as pltpu

D_MODEL = 2048
DEPTH = 4
MIX_A = D_MODEL // 2
RET_HEAD_DIM = 256
RET_HEADS = MIX_A // RET_HEAD_DIM
RET_CHUNK = 128
RET_COLS = 4 * MIX_A
ROPE_BASE = 10000.0
MIX_B = D_MODEL - MIX_A
RWKV_HEAD_DIM = 64
RWKV_HEADS = MIX_B // RWKV_HEAD_DIM
RWKV_LORA_W = 64
RWKV_LORA_A = 64
RWKV_LORA_G = 160
RWKV_COLS = 3 * MIX_B + RWKV_LORA_W + RWKV_LORA_A + RWKV_LORA_G
RWKV_SPLITS = (MIX_B, 2 * MIX_B, 3 * MIX_B, 3 * MIX_B + RWKV_LORA_W, 3 * MIX_B + RWKV_LORA_W + RWKV_LORA_A)
RWKV_GN_EPS = 64e-5
GLA_HEADS = 4
GLA_DK = D_MODEL // 2
GLA_DV = D_MODEL
GLA_HEAD_DK = GLA_DK // GLA_HEADS
GLA_HEAD_DV = GLA_DV // GLA_HEADS
GLA_LORA = 16
GLA_GATE_NORM = 16.0
GLA_CHUNK = 64
ODD_PROJ = 2 * GLA_DK + 2 * GLA_DV + GLA_LORA
MOE_GROUPS = 4
MOE_PER_GROUP = 8
N_EXPERTS = MOE_GROUPS * MOE_PER_GROUP
MOE_TOPK = 2
EXPERT_FF = D_MODEL // 4
DEEPNORM_ALPHA = (2 * DEPTH) ** 0.25
LN_EPS = 1e-5

f32 = jnp.float32
bf16 = jnp.bfloat16

VMEM_LIMIT_BYTES = 48 * 1024 * 1024
DENSE_VMEM_LIMIT_BYTES = 56 * 1024 * 1024


def _dotb(a, b):
    return jnp.dot(a.astype(bf16), b.astype(bf16), preferred_element_type=f32)


def _split_dot(x, ones_bf16):
    hi = x.astype(bf16)
    lo = (x - hi.astype(f32)).astype(bf16)
    return jnp.dot(hi, ones_bf16, preferred_element_type=f32) + jnp.dot(lo, ones_bf16, preferred_element_type=f32)


def _softplus(z):
    return jnp.maximum(z, 0.0) + jnp.log1p(jnp.exp(-jnp.abs(z)))


ADA_ROWS = 8
ADA_TN = 1024


def _ada_kernel(c_ref, w_ref, b_ref, o_ref):
    c = c_ref[...]
    act = (c * jax.nn.sigmoid(c)).astype(bf16)
    o_ref[0] = jnp.dot(act, w_ref[0].astype(bf16), preferred_element_type=f32) + b_ref[0]


def ada_mod(c, ada_w, ada_b):
    B, D = c.shape
    depth, _, N = ada_w.shape
    c_pad = jnp.pad(c, ((0, ADA_ROWS - B), (0, 0)))
    out = pl.pallas_call(
        _ada_kernel,
        grid=(depth, N // ADA_TN),
        in_specs=[pl.BlockSpec((ADA_ROWS, D), lambda l, j: (0, 0)),
                  pl.BlockSpec((1, D, ADA_TN), lambda l, j: (l, 0, j)),
                  pl.BlockSpec((1, 1, ADA_TN), lambda l, j: (l, 0, j))],
        out_specs=pl.BlockSpec((1, ADA_ROWS, ADA_TN), lambda l, j: (l, 0, j)),
        out_shape=jax.ShapeDtypeStruct((depth, ADA_ROWS, N), f32),
        compiler_params=pltpu.CompilerParams(
            dimension_semantics=("parallel", "parallel"), vmem_limit_bytes=VMEM_LIMIT_BYTES),
        name="ada_mod",
    )(c_pad, ada_w, ada_b.reshape(depth, 1, N))
    return out[:, :B]


def _matmul_kernel(x_ref, w_ref, o_ref):
    o_ref[...] = jnp.dot(x_ref[...], w_ref[...], preferred_element_type=f32)


def matmul(x, w, tm=2048, tn=512):
    M, K = x.shape
    Np = w.shape[1]
    assert Np % tn == 0 and M % min(tm, M) == 0
    wb = w.astype(bf16)
    tm = min(tm, M)
    return pl.pallas_call(
        _matmul_kernel,
        grid=(M // tm, Np // tn),
        in_specs=[pl.BlockSpec((tm, K), lambda i, j: (i, 0)),
                  pl.BlockSpec((K, tn), lambda i, j: (0, j))],
        out_specs=pl.BlockSpec((tm, tn), lambda i, j: (i, j)),
        out_shape=jax.ShapeDtypeStruct((M, Np), f32),
        compiler_params=pltpu.CompilerParams(
            dimension_semantics=("parallel", "parallel"), vmem_limit_bytes=DENSE_VMEM_LIMIT_BYTES),
        name="dense_matmul",
    )(x.astype(bf16), wb)


WKV_CHUNK = 64
WKV_PAIR = 2 * RWKV_HEAD_DIM
WKV_PAIRS = MIX_B // WKV_PAIR
RWKV_WA_PAD = 128
RWKV_G_PAD = 256
RWKV_V_PAD = 128
EV_R0 = RET_COLS
EV_WA0 = EV_R0 + 3 * MIX_B
EV_G0 = EV_WA0 + RWKV_WA_PAD
EVEN_TN = 512
EVEN_COLS = -(-(EV_G0 + RWKV_G_PAD) // EVEN_TN) * EVEN_TN


def _rwkv_kernel(has_vres, *refs):
    if has_vres:
        (r_ref, k_ref, v_ref, wa_ref, gd0_ref, gd1_ref, vf_ref, mu_ref, muwa_ref, mug_ref, w2_ref, a2_ref, g2_ref,
         vec_ref, v1_ref, v2_ref, o_ref, h_ref, prev_ref, prevwa_ref, prevg_ref) = refs
    else:
        (r_ref, k_ref, v_ref, wa_ref, gd0_ref, gd1_ref, mu_ref, muwa_ref, mug_ref, w2_ref, a2_ref, g2_ref,
         vec_ref, o_ref, vf_ref, h_ref, prev_ref, prevwa_ref, prevg_ref) = refs
    C = WKV_CHUNK
    P2 = WKV_PAIR

    @pl.when(pl.program_id(1) == 0)
    def _():
        h_ref[...] = jnp.zeros_like(h_ref)
        prev_ref[...] = jnp.zeros_like(prev_ref)
        prevwa_ref[...] = jnp.zeros_like(prevwa_ref)
        prevg_ref[...] = jnp.zeros_like(prevg_ref)

    def shift(x, carry_ref, j, mu):
        first = lax.broadcasted_iota(jnp.int32, x.shape, 0) == 0
        prev = jnp.where(first, jnp.broadcast_to(carry_ref[j, 7:8, :], x.shape), pltpu.roll(x, 1, 0))
        carry_ref[j] = x[C - 8:, :]
        return x + (prev - x) * mu

    r_all = shift(r_ref[...], prev_ref, 0, mu_ref[0:1, :])
    k_all = shift(k_ref[...], prev_ref, 1, mu_ref[1:2, :])
    v_all = shift(v_ref[...], prev_ref, 2, mu_ref[2:3, :])
    wa = shift(wa_ref[...], prevwa_ref, 0, muwa_ref[...])
    gd = shift(jnp.concatenate([gd0_ref[...], gd1_ref[...]], axis=1), prevg_ref, 0, mug_ref[...])
    w0, a0, k_k, k_a, r_k, lnx_g, lnx_b = (vec_ref[i:i + 1, :] for i in range(7))

    w_log = -_softplus(-(w0 + _dotb(jnp.tanh(wa), w2_ref[...]))) - 0.5
    lw_all = -jnp.exp(w_log)
    a_all = jax.nn.sigmoid(a0 + _dotb(wa, a2_ref[...]))
    gate_all = _dotb(jax.nn.sigmoid(gd), g2_ref[...])
    if has_vres:
        v0 = vec_ref[7:8, :]
        mix = jax.nn.sigmoid(v0 + _dotb(_dotb(v_all, v1_ref[...]), v2_ref[...]))
        v_all = v_all + (vf_ref[...] - v_all) * mix
    else:
        vf_ref[...] = v_all
    kkraw_all = k_all * k_k
    k_all = k_all * (1.0 + (a_all - 1.0) * k_a)

    row = lax.broadcasted_iota(jnp.int32, (P2, P2), 0)
    col = lax.broadcasted_iota(jnp.int32, (P2, P2), 1)
    same_head = (row >= C) == (col >= RWKV_HEAD_DIM)
    head_ones = jnp.where((row >= RWKV_HEAD_DIM) == (col >= RWKV_HEAD_DIM), 1.0, 0.0).astype(bf16)
    t_idx = jnp.bitwise_and(row, C - 1)
    s_idx = jnp.bitwise_and(col, C - 1)
    strict_lower = t_idx > s_idx
    incl_lower = t_idx >= s_idx
    eye = row == col
    tri = (lax.broadcasted_iota(jnp.int32, (C, C), 0) >= lax.broadcasted_iota(jnp.int32, (C, C), 1)).astype(bf16)
    zeros_pp = jnp.zeros((P2, P2), f32)
    nt = (((1,), (1,)), ((), ()))
    pairs = range(WKV_PAIRS)
    sl = [slice(p * P2, (p + 1) * P2) for p in pairs]

    def bd(x):
        return jnp.where(same_head, jnp.concatenate([x, x], axis=0), 0.0)

    kk = []
    for p in pairs:
        kr = kkraw_all[:, sl[p]]
        ss = _split_dot(kr * kr, head_ones)
        kk.append(kr / jnp.maximum(jnp.sqrt(ss), 1e-12))
    lw = [lw_all[:, sl[p]] for p in pairs]
    cum = []
    for p in pairs:
        hi = lw[p].astype(bf16)
        r1 = lw[p] - hi.astype(f32)
        mid = r1.astype(bf16)
        lo = (r1 - mid.astype(f32)).astype(bf16)
        cum.append(jnp.dot(tri, hi, preferred_element_type=f32) + jnp.dot(tri, mid, preferred_element_type=f32)
                   + jnp.dot(tri, lo, preferred_element_type=f32))
    tot = [cum[p][C - 1:C, :] for p in pairs]
    w_incl = [jnp.exp(cum[p]) for p in pairs]
    w_excl = [jnp.exp(cum[p] - lw[p]) for p in pairs]
    w_inv = [jnp.exp(-cum[p]) for p in pairs]
    w_rest = [jnp.exp(tot[p] - cum[p]) for p in pairs]
    r_p = [r_all[:, sl[p]] for p in pairs]
    k_p = [k_all[:, sl[p]] for p in pairs]
    v_p = [v_all[:, sl[p]] for p in pairs]
    a_p = [a_all[:, sl[p]] for p in pairs]
    at = [bd(-kk[p] * w_excl[p]) for p in pairs]
    rt = [bd(r_p[p] * w_incl[p]) for p in pairs]
    bt = [bd(kk[p] * a_p[p] * w_inv[p]) for p in pairs]
    kt = [bd(k_p[p] * w_inv[p]) for p in pairs]
    bh = [bd(kk[p] * a_p[p] * w_rest[p]) for p in pairs]
    kh = [bd(k_p[p] * w_rest[p]) for p in pairs]
    vb = [bd(v_p[p]) for p in pairs]

    scores = [lax.dot_general(jnp.concatenate([at[p], rt[p]], axis=0).astype(bf16),
                              jnp.concatenate([bt[p], kt[p]], axis=0).astype(bf16),
                              nt, preferred_element_type=f32) for p in pairs]
    a_ab = [jnp.where(strict_lower, scores[p][:P2, :P2], 0.0) for p in pairs]
    a_ak = [jnp.where(strict_lower, scores[p][:P2, P2:], 0.0) for p in pairs]
    a_rb = [jnp.where(incl_lower, scores[p][P2:, :P2], 0.0) for p in pairs]
    a_rk = [jnp.where(incl_lower, scores[p][P2:, P2:], 0.0) for p in pairs]
    x0 = [jnp.concatenate([at[p], _dotb(a_ak[p], vb[p])], axis=1) for p in pairs]
    eye_f = jnp.where(eye, 1.0, 0.0)
    inv = [eye_f + a_ab[p] for p in pairs]
    pw = [_dotb(a_ab[p], a_ab[p]) for p in pairs]
    for _ in range(int(np.log2(C)) - 2):
        both = [_dotb(pw[p], jnp.concatenate([pw[p], inv[p]], axis=1)) for p in pairs]
        inv = [inv[p] + both[p][:, P2:] for p in pairs]
        pw = [both[p][:, :P2] for p in pairs]
    inv = [inv[p] + _dotb(pw[p], inv[p]) for p in pairs]
    x = [_dotb(inv[p], x0[p]) for p in pairs]
    rhs = [jnp.concatenate([x[p], jnp.concatenate([zeros_pp, vb[p]], axis=1)], axis=0) for p in pairs]
    gz = [_dotb(jnp.concatenate([a_rb[p], a_rk[p]], axis=1), rhs[p]) for p in pairs]
    mn = [_dotb(jnp.concatenate([bh[p], kh[p]], axis=0).T, rhs[p]) for p in pairs]
    yh = []
    for p in pairs:
        g = rt[p] + gz[p][:, :P2]
        m = mn[p][:, :P2] + jnp.where(eye, jnp.broadcast_to(jnp.exp(tot[p]), (P2, P2)), 0.0)
        yh.append(_dotb(jnp.concatenate([g, m], axis=0), h_ref[p]))
    for p in pairs:
        y2 = yh[p][:P2] + gz[p][:, P2:]
        h_ref[p] = yh[p][P2:] + mn[p][:, P2:]
        y = y2[:C] + y2[C:]
        mean = _split_dot(y, head_ones) * (1.0 / RWKV_HEAD_DIM)
        yc = y - mean
        var = _split_dot(yc * yc, head_ones) * (1.0 / RWKV_HEAD_DIM)
        yn = yc * lax.rsqrt(var + RWKV_GN_EPS) * lnx_g[:, sl[p]] + lnx_b[:, sl[p]]
        bonus = _split_dot(r_p[p] * k_p[p] * r_k[:, sl[p]], head_ones) * v_p[p]
        o_ref[:, sl[p]] = ((yn + bonus) * gate_all[:, sl[p]]).astype(o_ref.dtype)


def rwkv_core(p, v_first, vres, mu, w0, w2, a0, a2, g2, k_k, k_a, r_k, lnx_g, lnx_b, B, S):
    C = WKV_CHUNK
    n = S // C
    has_vres = vres is not None
    r_mu, k_mu, v_mu, wd_mu, ad_mu, gd_mu = jnp.split(mu, RWKV_SPLITS)
    mu3 = jnp.stack([r_mu, k_mu, v_mu])
    mu_wa = jnp.concatenate([wd_mu, ad_mu]).reshape(1, RWKV_WA_PAD)
    mu_g = jnp.pad(gd_mu, (0, RWKV_G_PAD - RWKV_LORA_G)).reshape(1, RWKV_G_PAD)
    w2p = jnp.pad(w2, ((0, RWKV_WA_PAD - RWKV_LORA_W), (0, 0))).astype(bf16)
    a2p = jnp.pad(a2, ((RWKV_LORA_W, 0), (0, 0))).astype(bf16)
    g2p = jnp.pad(g2, ((0, RWKV_G_PAD - RWKV_LORA_G), (0, 0))).astype(bf16)
    vecs = [w0, a0, k_k, k_a, r_k.reshape(MIX_B), lnx_g, lnx_b]
    if has_vres:
        vecs.append(vres[0])
    vec = jnp.stack(vecs)
    nv = vec.shape[0]
    row = lambda bi, ci: bi * n + ci
    const = lambda bi, ci: (0, 0)
    tile = lambda w, j: pl.BlockSpec((C, w), lambda bi, ci: (row(bi, ci), j))
    in_specs = [tile(MIX_B, EV_R0 // MIX_B), tile(MIX_B, EV_R0 // MIX_B + 1), tile(MIX_B, EV_R0 // MIX_B + 2),
                tile(RWKV_WA_PAD, EV_WA0 // RWKV_WA_PAD),
                tile(RWKV_G_PAD // 2, EV_G0 // (RWKV_G_PAD // 2)), tile(RWKV_G_PAD // 2, EV_G0 // (RWKV_G_PAD // 2) + 1)]
    args = [p, p, p, p, p, p]
    if has_vres:
        in_specs.append(tile(MIX_B, 0))
        args.append(v_first)
    in_specs += [pl.BlockSpec((3, MIX_B), const), pl.BlockSpec((1, RWKV_WA_PAD), const),
                 pl.BlockSpec((1, RWKV_G_PAD), const), pl.BlockSpec((RWKV_WA_PAD, MIX_B), const),
                 pl.BlockSpec((RWKV_WA_PAD, MIX_B), const), pl.BlockSpec((RWKV_G_PAD, MIX_B), const),
                 pl.BlockSpec((nv, MIX_B), const)]
    args += [mu3, mu_wa, mu_g, w2p, a2p, g2p, vec]
    out_specs = [tile(MIX_B, 0)]
    out_shape = [jax.ShapeDtypeStruct((B * S, MIX_B), bf16)]
    if has_vres:
        v1p = jnp.pad(vres[1], ((0, 0), (0, RWKV_V_PAD - vres[1].shape[1]))).astype(bf16)
        v2p = jnp.pad(vres[2], ((0, RWKV_V_PAD - vres[2].shape[0]), (0, 0))).astype(bf16)
        in_specs += [pl.BlockSpec((MIX_B, RWKV_V_PAD), const), pl.BlockSpec((RWKV_V_PAD, MIX_B), const)]
        args += [v1p, v2p]
    else:
        out_specs.append(tile(MIX_B, 0))
        out_shape.append(jax.ShapeDtypeStruct((B * S, MIX_B), f32))
    outs = pl.pallas_call(
        functools.partial(_rwkv_kernel, has_vres),
        grid=(B, n),
        in_specs=in_specs,
        out_specs=out_specs,
        out_shape=out_shape,
        scratch_shapes=[pltpu.VMEM((WKV_PAIRS, WKV_PAIR, WKV_PAIR), f32),
                        pltpu.VMEM((3, 8, MIX_B), f32), pltpu.VMEM((1, 8, RWKV_WA_PAD), f32),
                        pltpu.VMEM((1, 8, RWKV_G_PAD), f32)],
        compiler_params=pltpu.CompilerParams(
            dimension_semantics=("parallel", "arbitrary"), vmem_limit_bytes=VMEM_LIMIT_BYTES),
        name="rwkv7_time_mix",
    )(*args)
    return (outs[0], v_first) if has_vres else (outs[0], outs[1])


def even_win_layout(win):
    return jnp.pad(win, ((0, 0), (0, EVEN_COLS - win.shape[1])))


RET_LOG_GAMMA = tuple(float(np.log1p(-np.exp2(-5.0 - h))) for h in range(RET_HEADS))
ROPE_HALF = RET_HEAD_DIM // 2


def _ret_kernel(q_ref, k_ref, v_ref, g_ref, pos_ref, inv_ref, gng_ref, gnb_ref, o_ref, s_ref):
    C = RET_CHUNK
    d = RET_HEAD_DIM

    @pl.when(pl.program_id(1) == 0)
    def _():
        s_ref[...] = jnp.zeros_like(s_ref)

    ang = pos_ref[...] * inv_ref[...]
    cos = jnp.cos(ang)
    sin = jnp.sin(ang)

    def rot(x):
        x1, x2 = x[:, :ROPE_HALF], x[:, ROPE_HALF:]
        return jnp.concatenate([x1 * cos - x2 * sin, x1 * sin + x2 * cos], axis=-1)

    ri = lax.broadcasted_iota(jnp.int32, (C, C), 0)
    ci = lax.broadcasted_iota(jnp.int32, (C, C), 1)
    diff = (ri - ci).astype(f32)
    row = lax.broadcasted_iota(jnp.int32, (C, d), 0).astype(f32)
    nt = (((1,), (1,)), ((), ()))

    for h in range(RET_HEADS):
        hs = slice(h * d, (h + 1) * d)
        lg = RET_LOG_GAMMA[h]
        q = rot(q_ref[:, hs])
        k = rot(k_ref[:, hs]) * (d ** -0.5)
        v = v_ref[:, hs].astype(bf16)
        decay = jnp.where(diff >= 0, jnp.exp(lg * jnp.maximum(diff, 0.0)), 0.0)
        scores = lax.dot_general(q.astype(bf16), k.astype(bf16), nt, preferred_element_type=f32) * decay
        state = s_ref[h]
        y = jnp.dot(scores.astype(bf16), v, preferred_element_type=f32)
        y = y + jnp.dot((q * jnp.exp(lg * (row + 1.0))).astype(bf16), state.astype(bf16), preferred_element_type=f32)
        kw = (k * jnp.exp(lg * (C - 1.0 - row))).astype(bf16)
        s_ref[h] = state * float(np.exp(lg * C)) + jnp.dot(kw.T, v, preferred_element_type=f32)
        mu = jnp.mean(y, axis=-1, keepdims=True)
        yc = y - mu
        var = jnp.mean(yc * yc, axis=-1, keepdims=True)
        yn = yc * lax.rsqrt(var + LN_EPS) * gng_ref[:, hs] + gnb_ref[:, hs]
        g = g_ref[:, hs]
        o_ref[:, hs] = (g * jax.nn.sigmoid(g) * yn).astype(o_ref.dtype)


def retention_core(p, positions, gn_g, gn_b, B, S):
    C = RET_CHUNK
    n = S // C
    inv = (ROPE_BASE ** (-jnp.arange(ROPE_HALF, dtype=f32) / ROPE_HALF)).reshape(1, ROPE_HALF)
    pos = positions.astype(f32).reshape(B * S, 1)
    row = lambda bi, ci: bi * n + ci
    const = lambda bi, ci: (0, 0)
    col = lambda j: pl.BlockSpec((C, MIX_A), lambda bi, ci: (row(bi, ci), j))
    return pl.pallas_call(
        _ret_kernel,
        grid=(B, n),
        in_specs=[col(0), col(1), col(2), col(3),
                  pl.BlockSpec((C, 1), lambda bi, ci: (row(bi, ci), 0)),
                  pl.BlockSpec((1, ROPE_HALF), const),
                  pl.BlockSpec((1, MIX_A), const),
                  pl.BlockSpec((1, MIX_A), const)],
        out_specs=pl.BlockSpec((C, MIX_A), lambda bi, ci: (row(bi, ci), 0)),
        out_shape=jax.ShapeDtypeStruct((B * S, MIX_A), bf16),
        scratch_shapes=[pltpu.VMEM((RET_HEADS, RET_HEAD_DIM, RET_HEAD_DIM), f32)],
        compiler_params=pltpu.CompilerParams(
            dimension_semantics=("parallel", "arbitrary"), vmem_limit_bytes=VMEM_LIMIT_BYTES),
        name="retention_chunked",
    )(p, p, p, p, pos, inv, gn_g.reshape(1, MIX_A), gn_b.reshape(1, MIX_A))


GLA_LEVELS = int(np.log2(GLA_CHUNK))
GLA_DECAY_ROWS = (2 + GLA_LEVELS) * GLA_CHUNK
GLA_AD_PAD = 128
ODD_PROJ_PAD = 2 * GLA_DK + 2 * GLA_DV + GLA_AD_PAD


def _gla_decay_operator():
    C = GLA_CHUNK
    i = np.arange(C)[:, None]
    j = np.arange(C)[None, :]
    blocks = [(j <= i), (j > i)]
    for lvl in range(GLA_LEVELS):
        s = C >> (lvl + 1)
        mid = (i // (2 * s)) * (2 * s) + s
        right = i >= mid
        blocks.append(np.where(right, (j >= mid) & (j <= i), (j > i) & (j < mid)))
    return np.concatenate(blocks, 0).astype(np.float32)


def _gla_kernel(q_ref, k_ref, v_ref, r_ref, ad_ref, a2_ref, ab_ref, g_ref, lop_ref, o_ref, s_ref):
    C = GLA_CHUNK

    @pl.when(pl.program_id(1) == 0)
    def _():
        s_ref[...] = jnp.zeros_like(s_ref)

    x = jnp.dot(ad_ref[...].astype(bf16), a2_ref[...], preferred_element_type=f32) + ab_ref[...]
    la = (jnp.minimum(x, 0.0) - jnp.log1p(jnp.exp(-jnp.abs(x)))) * (1.0 / GLA_GATE_NORM)
    hi = la.astype(bf16)
    r1 = la - hi.astype(f32)
    mid = r1.astype(bf16)
    lo = (r1 - mid.astype(f32)).astype(bf16)
    lop = lop_ref[...]
    dec = (jnp.dot(lop, hi, preferred_element_type=f32) + jnp.dot(lop, mid, preferred_element_type=f32)
           + jnp.dot(lop, lo, preferred_element_type=f32))

    ri = lax.broadcasted_iota(jnp.int32, (C, C), 0)
    ci = lax.broadcasted_iota(jnp.int32, (C, C), 1)
    level_masks = []
    for lvl in range(GLA_LEVELS):
        s = C >> (lvl + 1)
        same_block = (ri // (2 * s)) == (ci // (2 * s))
        level_masks.append(same_block & ((ri % (2 * s)) >= s) & ((ci % (2 * s)) < s))
    eye = ri == ci
    nt = (((1,), (1,)), ((), ()))

    for h in range(GLA_HEADS):
        ks = slice(h * GLA_HEAD_DK, (h + 1) * GLA_HEAD_DK)
        vs = slice(h * GLA_HEAD_DV, (h + 1) * GLA_HEAD_DV)
        q = q_ref[:, ks] * (GLA_HEAD_DK ** -0.5)
        k = k_ref[:, ks]
        v = v_ref[:, vs].astype(bf16)
        e = jnp.exp(dec[:, ks])
        state = s_ref[h]
        o = lax.dot_general((q * e[:C]).astype(bf16), state.astype(bf16), nt, preferred_element_type=f32)
        attn = jnp.where(eye, lax.dot_general(q.astype(bf16), k.astype(bf16), nt, preferred_element_type=f32), 0.0)
        for lvl in range(GLA_LEVELS):
            el = e[(2 + lvl) * C:(3 + lvl) * C]
            sc = lax.dot_general((q * el).astype(bf16), (k * el).astype(bf16), nt, preferred_element_type=f32)
            attn = attn + jnp.where(level_masks[lvl], sc, 0.0)
        o = o + jnp.dot(attn.astype(bf16), v, preferred_element_type=f32)
        kd = (k * e[C:2 * C]).astype(bf16)
        s_ref[h] = state * e[C - 1:C] + jnp.dot(v.T, kd, preferred_element_type=f32)
        o = o * lax.rsqrt(jnp.mean(o * o, axis=-1, keepdims=True) + LN_EPS) * g_ref[:, vs]
        rr = r_ref[:, vs]
        o_ref[:, vs] = (rr * jax.nn.sigmoid(rr) * o).astype(o_ref.dtype)


def gla_core(p, a2, ab, norm_g, B, S):
    C = GLA_CHUNK
    n = S // C
    a2p = jnp.pad(a2, ((0, GLA_AD_PAD - GLA_LORA), (0, 0))).astype(bf16)
    lop = jnp.asarray(_gla_decay_operator(), dtype=bf16)
    row = lambda bi, ci: bi * n + ci
    const = lambda bi, ci: (0, 0)
    return pl.pallas_call(
        _gla_kernel,
        grid=(B, n),
        in_specs=[pl.BlockSpec((C, GLA_DK), lambda bi, ci: (row(bi, ci), 0)),
                  pl.BlockSpec((C, GLA_DK), lambda bi, ci: (row(bi, ci), 1)),
                  pl.BlockSpec((C, GLA_DV), lambda bi, ci: (row(bi, ci), 1)),
                  pl.BlockSpec((C, GLA_DV), lambda bi, ci: (row(bi, ci), 2)),
                  pl.BlockSpec((C, GLA_AD_PAD), lambda bi, ci: (row(bi, ci), (2 * GLA_DK + 2 * GLA_DV) // GLA_AD_PAD)),
                  pl.BlockSpec((GLA_AD_PAD, GLA_DK), const),
                  pl.BlockSpec((1, GLA_DK), const),
                  pl.BlockSpec((1, GLA_DV), const),
                  pl.BlockSpec((GLA_DECAY_ROWS, C), const)],
        out_specs=pl.BlockSpec((C, GLA_DV), lambda bi, ci: (row(bi, ci), 0)),
        out_shape=jax.ShapeDtypeStruct((B * S, GLA_DV), bf16),
        scratch_shapes=[pltpu.VMEM((GLA_HEADS, GLA_HEAD_DV, GLA_HEAD_DK), f32)],
        compiler_params=pltpu.CompilerParams(
            dimension_semantics=("parallel", "arbitrary"), vmem_limit_bytes=VMEM_LIMIT_BYTES),
        name="gla_chunked",
    )(p, p, p, p, p, a2p, ab.reshape(1, GLA_DK), norm_g.reshape(1, GLA_DV), lop)


PROJ_TM = 256
ROUTER_PAD = 128


def _proj_norm_kernel(n_in, *refs):
    a_refs = refs[:n_in]
    w_ref, x_ref, mod_ref, g_ref, b_ref, wr_ref, br_ref, xo_ref, ho_ref, lo_ref = refs[n_in:]
    acc = None
    k0 = 0
    for a_ref in a_refs:
        kw = a_ref.shape[1]
        part = jnp.dot(a_ref[...], w_ref[k0:k0 + kw, :], preferred_element_type=f32)
        acc = part if acc is None else acc + part
        k0 += kw
    gt, sc, sh = mod_ref[0, 0:1, :], mod_ref[0, 1:2, :], mod_ref[0, 2:3, :]
    z = DEEPNORM_ALPHA * x_ref[...] + (1.0 + gt) * acc
    mu = jnp.mean(z, axis=-1, keepdims=True)
    zc = z - mu
    var = jnp.mean(zc * zc, axis=-1, keepdims=True)
    xn = zc * lax.rsqrt(var + LN_EPS) * g_ref[...] + b_ref[...]
    xo_ref[...] = xn
    hf = xn * (1.0 + sc) + sh
    ho_ref[...] = hf
    lo_ref[...] = jnp.dot(hf.astype(bf16), wr_ref[...], preferred_element_type=f32) + br_ref[...]


def proj_norm(acts, wo, x, mod, ln_g, ln_b, w_router, b_router, S):
    T, D = x.shape
    tiles_per_batch = S // PROJ_TM
    const = lambda i: (0, 0)
    rowblk = lambda w: pl.BlockSpec((PROJ_TM, w), lambda i: (i, 0))
    in_specs = [rowblk(a.shape[1]) for a in acts]
    in_specs += [pl.BlockSpec(wo.shape, const), rowblk(D),
                 pl.BlockSpec((1, 3, D), lambda i: (i // tiles_per_batch, 0, 0)),
                 pl.BlockSpec((1, D), const), pl.BlockSpec((1, D), const),
                 pl.BlockSpec((D, ROUTER_PAD), const), pl.BlockSpec((1, ROUTER_PAD), const)]
    return pl.pallas_call(
        functools.partial(_proj_norm_kernel, len(acts)),
        grid=(T // PROJ_TM,),
        in_specs=in_specs,
        out_specs=[rowblk(D), rowblk(D), rowblk(ROUTER_PAD)],
        out_shape=[jax.ShapeDtypeStruct((T, D), f32), jax.ShapeDtypeStruct((T, D), f32),
                   jax.ShapeDtypeStruct((T, ROUTER_PAD), f32)],
        compiler_params=pltpu.CompilerParams(
            dimension_semantics=("parallel",), vmem_limit_bytes=VMEM_LIMIT_BYTES),
        name="proj_norm",
    )(*acts, wo.astype(bf16), x, mod, ln_g.reshape(1, D), ln_b.reshape(1, D), w_router, b_router)


MOE_ROWS = 256
MOE_VMEM_LIMIT_BYTES = 56 * 1024 * 1024
COMBINE_ROWS = 256
SUBLANES = 8
GATHER_DMA_PRIORITY = 1


def _moe_ffn_kernel(layer, tok_ref, be_ref, nused_ref, x_hbm, w1_hbm, w3_hbm, w2_hbm, y_ref,
                    xbuf, w1buf, w3buf, w2buf, sem, wsem, run_ref):
    i = pl.program_id(0)
    nblk = pl.num_programs(0)
    n_used = nused_ref[0]
    slot = lax.rem(i, 2)
    expert = be_ref[i]
    first = (i < n_used) & ((i == 0) | (expert != be_ref[jnp.maximum(i - 1, 0)]))

    def start_gather(blk, s):
        def body(g, carry):
            for u in range(SUBLANES):
                t = tok_ref[blk * MOE_ROWS + g * SUBLANES + u]
                pltpu.make_async_copy(x_hbm.at[lax.shift_right_logical(t, 3), pl.ds(jnp.bitwise_and(t, SUBLANES - 1), 1)],
                                      xbuf.at[s, g, pl.ds(u, 1)], sem.at[s]).start(priority=GATHER_DMA_PRIORITY)
            return carry
        lax.fori_loop(0, MOE_ROWS // SUBLANES, body, 0)

    def weight_copies(e, b):
        return [pltpu.make_async_copy(w_hbm.at[layer, e], wbuf.at[b], wsem.at[b])
                for w_hbm, wbuf in ((w1_hbm, w1buf), (w3_hbm, w3buf), (w2_hbm, w2buf))]

    @pl.when((i == 0) & (n_used > 0))
    def _():
        run_ref[0] = 0
        for cp in weight_copies(expert, 0):
            cp.start()
        start_gather(0, 0)

    @pl.when(i + 1 < n_used)
    def _():
        start_gather(i + 1, 1 - slot)

    @pl.when(first)
    def _():
        @pl.when(i > 0)
        def _():
            run_ref[0] = run_ref[0] + 1
        buf = jnp.bitwise_and(run_ref[0], 1)
        nxt = lax.while_loop(lambda j: (j < n_used) & (be_ref[jnp.minimum(j, nblk - 1)] == expert),
                             lambda j: j + 1, i + 1)

        @pl.when(nxt < n_used)
        def _():
            for cp in weight_copies(be_ref[jnp.minimum(nxt, nblk - 1)], 1 - buf):
                cp.start()
        for cp in weight_copies(expert, buf):
            cp.wait()

    @pl.when(i < n_used)
    def _():
        buf = jnp.bitwise_and(run_ref[0], 1)
        pltpu.make_async_copy(x_hbm.at[pl.ds(0, MOE_ROWS // SUBLANES)], xbuf.at[slot], sem.at[slot]).wait()
        xb = xbuf[slot].reshape(MOE_ROWS, xbuf.shape[-1]).astype(bf16)
        a = jnp.dot(xb, w1buf[buf].astype(bf16), preferred_element_type=f32)
        b = jnp.dot(xb, w3buf[buf].astype(bf16), preferred_element_type=f32)
        hid = (a * jax.nn.sigmoid(a) * b).astype(bf16)
        y_ref[...] = jnp.dot(hid, w2buf[buf].astype(bf16), preferred_element_type=f32)

    @pl.when(i >= n_used)
    def _():
        y_ref[...] = jnp.zeros_like(y_ref)


def moe_ffn(x, buf_tok, blk_expert, n_used, w1, w3, w2, layer):
    D = x.shape[1]
    nblk = blk_expert.shape[0]
    any_spec = pl.BlockSpec(memory_space=pl.ANY)
    return pl.pallas_call(
        functools.partial(_moe_ffn_kernel, layer),
        grid_spec=pltpu.PrefetchScalarGridSpec(
            num_scalar_prefetch=3,
            grid=(nblk,),
            in_specs=[any_spec, any_spec, any_spec, any_spec],
            out_specs=pl.BlockSpec((MOE_ROWS, D), lambda i, tok, be, nu: (i, 0)),
            scratch_shapes=[pltpu.VMEM((2, MOE_ROWS // SUBLANES, SUBLANES, D), f32),
                            pltpu.VMEM((2, D, EXPERT_FF), f32), pltpu.VMEM((2, D, EXPERT_FF), f32),
                            pltpu.VMEM((2, EXPERT_FF, D), f32),
                            pltpu.SemaphoreType.DMA((2,)), pltpu.SemaphoreType.DMA((2,)),
                            pltpu.SMEM((1,), jnp.int32)]),
        out_shape=jax.ShapeDtypeStruct((nblk * MOE_ROWS, D), f32),
        compiler_params=pltpu.CompilerParams(
            dimension_semantics=("arbitrary",), vmem_limit_bytes=MOE_VMEM_LIMIT_BYTES),
        name="moe_expert_ffn",
    )(buf_tok, blk_expert, n_used, x.reshape(x.shape[0] // SUBLANES, SUBLANES, D), w1, w3, w2)


def _moe_combine_kernel(emit_h, s0_ref, s1_ref, y_hbm, gate_ref, x_ref, mod_ref, g_ref, b_ref, *rest):
    if emit_h:
        o_ref, h_ref, buf, sem = rest
    else:
        o_ref, buf, sem = rest
    i = pl.program_id(0)
    n = pl.num_programs(0)
    slot = lax.rem(i, 2)

    def start_gather(blk, s):
        def body(g, carry):
            base = pl.multiple_of(g * SUBLANES, SUBLANES)
            for u in range(SUBLANES):
                t = blk * COMBINE_ROWS + base + u
                pltpu.make_async_copy(y_hbm.at[pl.ds(s0_ref[t], 1)], buf.at[s, 0, pl.ds(base + u, 1)], sem.at[s]).start()
                pltpu.make_async_copy(y_hbm.at[pl.ds(s1_ref[t], 1)], buf.at[s, 1, pl.ds(base + u, 1)], sem.at[s]).start()
            return carry
        lax.fori_loop(0, COMBINE_ROWS // SUBLANES, body, 0)

    @pl.when(i == 0)
    def _():
        start_gather(0, 0)

    @pl.when(i + 1 < n)
    def _():
        start_gather(i + 1, 1 - slot)

    for c in range(MOE_TOPK):
        pltpu.make_async_copy(y_hbm.at[pl.ds(0, COMBINE_ROWS)], buf.at[slot, c], sem.at[slot]).wait()
    y = buf[slot, 0] * gate_ref[:, 0:1] + buf[slot, 1] * gate_ref[:, 1:2]
    z = DEEPNORM_ALPHA * x_ref[...] + (1.0 + mod_ref[0, 0:1, :]) * y
    mu = jnp.mean(z, axis=-1, keepdims=True)
    zc = z - mu
    var = jnp.mean(zc * zc, axis=-1, keepdims=True)
    xn = zc * lax.rsqrt(var + LN_EPS) * g_ref[...] + b_ref[...]
    o_ref[...] = xn
    if emit_h:
        h_ref[...] = (xn * (1.0 + mod_ref[0, 1:2, :]) + mod_ref[0, 2:3, :]).astype(h_ref.dtype)


def moe_combine_norm(y_rows, slot0, slot1, gate, x, mod, ln_g, ln_b, S, emit_h):
    T, D = x.shape
    tiles_per_batch = S // COMBINE_ROWS
    const = lambda i, s0, s1: (0, 0)
    rowblk = pl.BlockSpec((COMBINE_ROWS, D), lambda i, s0, s1: (i, 0))
    out_specs = [rowblk, rowblk] if emit_h else [rowblk]
    out_shape = [jax.ShapeDtypeStruct((T, D), f32)] + ([jax.ShapeDtypeStruct((T, D), bf16)] if emit_h else [])
    return pl.pallas_call(
        functools.partial(_moe_combine_kernel, emit_h),
        grid_spec=pltpu.PrefetchScalarGridSpec(
            num_scalar_prefetch=2,
            grid=(T // COMBINE_ROWS,),
            in_specs=[pl.BlockSpec(memory_space=pl.ANY),
                      pl.BlockSpec((COMBINE_ROWS, MOE_TOPK), lambda i, s0, s1: (i, 0)), rowblk,
                      pl.BlockSpec((1, 3, D), lambda i, s0, s1: (i // tiles_per_batch, 0, 0)),
                      pl.BlockSpec((1, D), const), pl.BlockSpec((1, D), const)],
            out_specs=out_specs,
            scratch_shapes=[pltpu.VMEM((2, MOE_TOPK, COMBINE_ROWS, D), f32), pltpu.SemaphoreType.DMA((2,))]),
        out_shape=out_shape,
        compiler_params=pltpu.CompilerParams(
            dimension_semantics=("arbitrary",), vmem_limit_bytes=VMEM_LIMIT_BYTES),
        name="moe_combine_norm",
    )(slot0, slot1, y_rows, gate, x, mod, ln_g.reshape(1, D), ln_b.reshape(1, D))


def router_params(wg, bg, we, be):
    D = wg.shape[0]
    pad = ROUTER_PAD - MOE_GROUPS - N_EXPERTS
    w = jnp.concatenate([wg, we, jnp.zeros((D, pad), wg.dtype)], 1).astype(bf16)
    b = jnp.concatenate([bg, be, jnp.zeros((pad,), bg.dtype)]).reshape(1, ROUTER_PAD)
    return w, b


def moe_route(logits):
    g_prob = jax.nn.softmax(logits[:, :MOE_GROUPS], -1)
    g_top = jnp.argmax(g_prob, -1).astype(jnp.int32)
    p_top = jnp.max(g_prob, -1)
    ids = jnp.arange(N_EXPERTS, dtype=jnp.int32)[None, :]
    e_sel = jnp.where(ids // MOE_PER_GROUP == g_top[:, None], logits[:, MOE_GROUPS:MOE_GROUPS + N_EXPERTS], -jnp.inf)
    i1 = jnp.argmax(e_sel, -1).astype(jnp.int32)
    v1 = jnp.max(e_sel, -1)
    rest = jnp.where(ids == i1[:, None], -jnp.inf, e_sel)
    i2 = jnp.argmax(rest, -1).astype(jnp.int32)
    v2 = jnp.max(rest, -1)
    gate = p_top[:, None] * jax.nn.softmax(jnp.stack([v1, v2], -1), -1)
    return gate, jnp.stack([i1, i2], -1)


def moe_layout(expert):
    T = expert.shape[0]
    A = T * MOE_TOPK
    e_flat = expert.reshape(A)
    onehot = (e_flat[:, None] == jnp.arange(N_EXPERTS, dtype=jnp.int32)[None, :]).astype(jnp.int32)
    rank = jnp.cumsum(onehot, axis=0)
    counts = rank[-1]
    padded = (counts + MOE_ROWS - 1) // MOE_ROWS * MOE_ROWS
    pad_end = jnp.cumsum(padded)
    pad_start = pad_end - padded
    slot = jnp.sum(onehot * (pad_start[None, :] + rank - 1), axis=1).astype(jnp.int32)
    tok_flat = jnp.arange(A, dtype=jnp.int32) // MOE_TOPK
    nblk = A // MOE_ROWS + N_EXPERTS
    buf_tok = jnp.zeros((nblk * MOE_ROWS,), jnp.int32).at[slot].set(tok_flat)
    n_used = (pad_end[-1] // MOE_ROWS).astype(jnp.int32)
    blk = jnp.minimum(jnp.arange(nblk, dtype=jnp.int32), n_used - 1) * MOE_ROWS
    blk_expert = jnp.minimum(jnp.sum(blk[:, None] >= pad_end[None, :], axis=1), N_EXPERTS - 1).astype(jnp.int32)
    slot2 = slot.reshape(T, MOE_TOPK)
    return buf_tok, blk_expert, n_used.reshape(1), slot2[:, 0], slot2[:, 1]


def kernel(x, c, positions, ada_w, ada_b, ln_g, ln_b, ev_win, ev_wo, ret_gn_g, ret_gn_b, rw_mu, rw_w0, rw_w2, rw_a0, rw_a2, rw_g2, rw_kk, rw_ka, rw_rk, rw_lnx_g, rw_lnx_b, rw_v0, rw_v1, rw_v2, od_win, od_wo, gla_a2, gla_ab, gla_norm_g, moe_wg, moe_bg, moe_we, moe_be, moe_w1, moe_w3, moe_w2):
    B, S, D = x.shape
    T = B * S
    depth = ada_w.shape[0]
    mods = ada_mod(c, ada_w, ada_b).reshape(depth, B, 6, D)
    h = (x * (1.0 + mods[0, :, 1])[:, None, :] + mods[0, :, 0][:, None, :]).astype(bf16).reshape(T, D)
    x = x.reshape(T, D)
    v_first = None
    for layer in range(depth):
        j = layer // 2
        mixer_mod = jnp.stack([mods[layer, :, 2], mods[layer, :, 4], mods[layer, :, 3]], axis=1)
        if layer % 2 == 0:
            p = matmul(h, even_win_layout(ev_win[j]), tn=EVEN_TN)
            out_a = retention_core(p, positions, ret_gn_g[j], ret_gn_b[j], B, S)
            vres = None if j == 0 else (rw_v0[j - 1], rw_v1[j - 1], rw_v2[j - 1])
            out_b, v_first = rwkv_core(p, v_first, vres, rw_mu[j], rw_w0[j], rw_w2[j], rw_a0[j], rw_a2[j], rw_g2[j],
                                       rw_kk[j], rw_ka[j], rw_rk[j], rw_lnx_g[j], rw_lnx_b[j], B, S)
            acts, wo = [out_a, out_b], ev_wo[j]
        else:
            p = matmul(h, jnp.pad(od_win[j], ((0, 0), (0, ODD_PROJ_PAD - ODD_PROJ))), tn=896)
            acts, wo = [gla_core(p, gla_a2[j], gla_ab[j], gla_norm_g[j], B, S)], od_wo[j]
        w_router, b_router = router_params(moe_wg[layer], moe_bg[layer], moe_we[layer], moe_be[layer])
        x, hf, logits = proj_norm(acts, wo, x, mixer_mod, ln_g[layer, 0], ln_b[layer, 0], w_router, b_router, S)
        gate, expert = moe_route(logits)
        buf_tok, blk_expert, n_used, slot0, slot1 = moe_layout(expert)
        y_rows = moe_ffn(hf, buf_tok, blk_expert, n_used, moe_w1, moe_w3, moe_w2, layer)
        last = layer == depth - 1
        nxt = layer if last else layer + 1
        moe_mod = jnp.stack([mods[layer, :, 5], mods[nxt, :, 1], mods[nxt, :, 0]], axis=1)
        outs = moe_combine_norm(y_rows, slot0, slot1, gate, x, moe_mod, ln_g[layer, 1], ln_b[layer, 1], S, not last)
        x = outs[0]
        if not last:
            h = outs[1]
    return x.reshape(B, S, D)
```

```python
import functools

import jax
import jax.numpy as jnp
import numpy as np
from jax import lax
from jax.experimental import pallas as pl
from jax.experimental.pallas import tpu as pltpu

D_MODEL = 2048
DEPTH = 4
MIX_A = D_MODEL // 2
RET_HEAD_DIM = 256
RET_HEADS = MIX_A // RET_HEAD_DIM
RET_CHUNK = 128
RET_COLS = 4 * MIX_A
ROPE_BASE = 10000.0
MIX_B = D_MODEL - MIX_A
RWKV_HEAD_DIM = 64
RWKV_HEADS = MIX_B // RWKV_HEAD_DIM
RWKV_LORA_W = 64
RWKV_LORA_A = 64
RWKV_LORA_G = 160
RWKV_COLS = 3 * MIX_B + RWKV_LORA_W + RWKV_LORA_A + RWKV_LORA_G
RWKV_SPLITS = (MIX_B, 2 * MIX_B, 3 * MIX_B, 3 * MIX_B + RWKV_LORA_W, 3 * MIX_B + RWKV_LORA_W + RWKV_LORA_A)
RWKV_GN_EPS = 64e-5
GLA_HEADS = 4
GLA_DK = D_MODEL // 2
GLA_DV = D_MODEL
GLA_HEAD_DK = GLA_DK // GLA_HEADS
GLA_HEAD_DV = GLA_DV // GLA_HEADS
GLA_LORA = 16
GLA_GATE_NORM = 16.0
GLA_CHUNK = 64
ODD_PROJ = 2 * GLA_DK + 2 * GLA_DV + GLA_LORA
MOE_GROUPS = 4
MOE_PER_GROUP = 8
N_EXPERTS = MOE_GROUPS * MOE_PER_GROUP
MOE_TOPK = 2
EXPERT_FF = D_MODEL // 4
DEEPNORM_ALPHA = (2 * DEPTH) ** 0.25
LN_EPS = 1e-5

f32 = jnp.float32
bf16 = jnp.bfloat16

VMEM_LIMIT_BYTES = 48 * 1024 * 1024
DENSE_VMEM_LIMIT_BYTES = 56 * 1024 * 1024


def _dotb(a, b):
    return jnp.dot(a.astype(bf16), b.astype(bf16), preferred_element_type=f32)


def _split_dots(xs, ones_bf16):
    n, rows = len(xs), xs[0].shape[0]
    his = [x.astype(bf16) for x in xs]
    los = [(x - h.astype(f32)).astype(bf16) for x, h in zip(xs, his)]
    out = jnp.dot(jnp.concatenate(his + los, axis=0), ones_bf16, preferred_element_type=f32)
    return [out[i * rows:(i + 1) * rows] + out[(n + i) * rows:(n + i + 1) * rows] for i in range(n)]


def _softplus(z):
    return jnp.maximum(z, 0.0) + jnp.log1p(jnp.exp(-jnp.abs(z)))


ADA_ROWS = 8
ADA_TN = 1024


def _ada_kernel(c_ref, w_ref, b_ref, o_ref):
    c = c_ref[...]
    act = (c * jax.nn.sigmoid(c)).astype(bf16)
    o_ref[0] = jnp.dot(act, w_ref[0].astype(bf16), preferred_element_type=f32) + b_ref[0]


def ada_mod(c, ada_w, ada_b):
    B, D = c.shape
    depth, _, N = ada_w.shape
    c_pad = jnp.pad(c, ((0, ADA_ROWS - B), (0, 0)))
    out = pl.pallas_call(
        _ada_kernel,
        grid=(depth, N // ADA_TN),
        in_specs=[pl.BlockSpec((ADA_ROWS, D), lambda l, j: (0, 0)),
                  pl.BlockSpec((1, D, ADA_TN), lambda l, j: (l, 0, j)),
                  pl.BlockSpec((1, 1, ADA_TN), lambda l, j: (l, 0, j))],
        out_specs=pl.BlockSpec((1, ADA_ROWS, ADA_TN), lambda l, j: (l, 0, j)),
        out_shape=jax.ShapeDtypeStruct((depth, ADA_ROWS, N), f32),
        compiler_params=pltpu.CompilerParams(
            dimension_semantics=("parallel", "parallel"), vmem_limit_bytes=VMEM_LIMIT_BYTES),
        name="ada_mod",
    )(c_pad, ada_w, ada_b.reshape(depth, 1, N))
    return out[:, :B]


def _matmul_kernel(x_ref, w_ref, o_ref):
    o_ref[...] = jnp.dot(x_ref[...], w_ref[...], preferred_element_type=f32)


def matmul(x, w, tm=2048, tn=512):
    M, K = x.shape
    Np = w.shape[1]
    assert Np % tn == 0 and M % min(tm, M) == 0
    wb = w.astype(bf16)
    tm = min(tm, M)
    return pl.pallas_call(
        _matmul_kernel,
        grid=(M // tm, Np // tn),
        in_specs=[pl.BlockSpec((tm, K), lambda i, j: (i, 0)),
                  pl.BlockSpec((K, tn), lambda i, j: (0, j))],
        out_specs=pl.BlockSpec((tm, tn), lambda i, j: (i, j)),
        out_shape=jax.ShapeDtypeStruct((M, Np), f32),
        compiler_params=pltpu.CompilerParams(
            dimension_semantics=("parallel", "parallel"), vmem_limit_bytes=DENSE_VMEM_LIMIT_BYTES),
        name="dense_matmul",
    )(x.astype(bf16), wb)


WKV_CHUNK = 64
WKV_PAIR = 2 * RWKV_HEAD_DIM
WKV_PAIRS = MIX_B // WKV_PAIR
RWKV_WA_PAD = 128
RWKV_G_PAD = 256
RWKV_V_PAD = 128
EV_R0 = RET_COLS
EV_WA0 = EV_R0 + 3 * MIX_B
EV_G0 = EV_WA0 + RWKV_WA_PAD
EVEN_TN = 512
EVEN_COLS = -(-(EV_G0 + RWKV_G_PAD) // EVEN_TN) * EVEN_TN


def _rwkv_kernel(has_vres, *refs):
    if has_vres:
        (r_ref, k_ref, v_ref, wa_ref, gd0_ref, gd1_ref, vf_ref, mu_ref, muwa_ref, mug_ref, w2_ref, a2_ref, g2_ref,
         vec_ref, v1_ref, v2_ref, o_ref, h_ref, prev_ref, prevwa_ref, prevg_ref) = refs
    else:
        (r_ref, k_ref, v_ref, wa_ref, gd0_ref, gd1_ref, mu_ref, muwa_ref, mug_ref, w2_ref, a2_ref, g2_ref,
         vec_ref, o_ref, vf_ref, h_ref, prev_ref, prevwa_ref, prevg_ref) = refs
    C = WKV_CHUNK
    P2 = WKV_PAIR

    @pl.when(pl.program_id(1) == 0)
    def _():
        h_ref[...] = jnp.zeros_like(h_ref)
        prev_ref[...] = jnp.zeros_like(prev_ref)
        prevwa_ref[...] = jnp.zeros_like(prevwa_ref)
        prevg_ref[...] = jnp.zeros_like(prevg_ref)

    def shift(x, carry_ref, j, mu):
        first = lax.broadcasted_iota(jnp.int32, x.shape, 0) == 0
        prev = jnp.where(first, jnp.broadcast_to(carry_ref[j, 7:8, :], x.shape), pltpu.roll(x, 1, 0))
        carry_ref[j] = x[C - 8:, :]
        return x + (prev - x) * mu

    r_all = shift(r_ref[...], prev_ref, 0, mu_ref[0:1, :])
    k_all = shift(k_ref[...], prev_ref, 1, mu_ref[1:2, :])
    v_all = shift(v_ref[...], prev_ref, 2, mu_ref[2:3, :])
    wa = shift(wa_ref[...], prevwa_ref, 0, muwa_ref[...])
    gd = shift(jnp.concatenate([gd0_ref[...], gd1_ref[...]], axis=1), prevg_ref, 0, mug_ref[...])
    w0, a0, k_k, k_a, r_k, lnx_g, lnx_b = (vec_ref[i:i + 1, :] for i in range(7))

    w_log = -_softplus(-(w0 + _dotb(jnp.tanh(wa), w2_ref[...]))) - 0.5
    lw_all = -jnp.exp(w_log)
    a_all = jax.nn.sigmoid(a0 + _dotb(wa, a2_ref[...]))
    gate_all = _dotb(jax.nn.sigmoid(gd), g2_ref[...])
    if has_vres:
        v0 = vec_ref[7:8, :]
        mix = jax.nn.sigmoid(v0 + _dotb(_dotb(v_all, v1_ref[...]), v2_ref[...]))
        v_all = v_all + (vf_ref[...] - v_all) * mix
    else:
        vf_ref[...] = v_all
    kkraw_all = k_all * k_k
    k_all = k_all * (1.0 + (a_all - 1.0) * k_a)

    row = lax.broadcasted_iota(jnp.int32, (P2, P2), 0)
    col = lax.broadcasted_iota(jnp.int32, (P2, P2), 1)
    same_head = (row >= C) == (col >= RWKV_HEAD_DIM)
    head_ones = jnp.where((row >= RWKV_HEAD_DIM) == (col >= RWKV_HEAD_DIM), 1.0, 0.0).astype(bf16)
    t_idx = jnp.bitwise_and(row, C - 1)
    s_idx = jnp.bitwise_and(col, C - 1)
    strict_lower = t_idx > s_idx
    incl_lower = t_idx >= s_idx
    eye = row == col
    tri = (lax.broadcasted_iota(jnp.int32, (C, C), 0) >= lax.broadcasted_iota(jnp.int32, (C, C), 1)).astype(bf16)
    zeros_pp = jnp.zeros((P2, P2), f32)
    nt = (((1,), (1,)), ((), ()))
    pairs = range(WKV_PAIRS)
    sl = [slice(p * P2, (p + 1) * P2) for p in pairs]

    def bd(x):
        return jnp.where(same_head, jnp.concatenate([x, x], axis=0), 0.0)

    kkraw = [kkraw_all[:, sl[p]] for p in pairs]
    sumsq = _split_dots([kkraw[p] * kkraw[p] for p in pairs], head_ones)
    kk = [kkraw[p] / jnp.maximum(jnp.sqrt(sumsq[p]), 1e-12) for p in pairs]
    lw = [lw_all[:, sl[p]] for p in pairs]
    cum = []
    for p in pairs:
        hi = lw[p].astype(bf16)
        r1 = lw[p] - hi.astype(f32)
        mid = r1.astype(bf16)
        lo = (r1 - mid.astype(f32)).astype(bf16)
        cum.append(jnp.dot(tri, hi, preferred_element_type=f32) + jnp.dot(tri, mid, preferred_element_type=f32)
                   + jnp.dot(tri, lo, preferred_element_type=f32))
    tot = [cum[p][C - 1:C, :] for p in pairs]
    w_incl = [jnp.exp(cum[p]) for p in pairs]
    w_excl = [jnp.exp(cum[p] - lw[p]) for p in pairs]
    w_inv = [jnp.exp(-cum[p]) for p in pairs]
    w_rest = [jnp.exp(tot[p] - cum[p]) for p in pairs]
    r_p = [r_all[:, sl[p]] for p in pairs]
    k_p = [k_all[:, sl[p]] for p in pairs]
    v_p = [v_all[:, sl[p]] for p in pairs]
    a_p = [a_all[:, sl[p]] for p in pairs]
    at = [bd(-kk[p] * w_excl[p]) for p in pairs]
    rt = [bd(r_p[p] * w_incl[p]) for p in pairs]
    bt = [bd(kk[p] * a_p[p] * w_inv[p]) for p in pairs]
    kt = [bd(k_p[p] * w_inv[p]) for p in pairs]
    bh = [bd(kk[p] * a_p[p] * w_rest[p]) for p in pairs]
    kh = [bd(k_p[p] * w_rest[p]) for p in pairs]
    vb = [bd(v_p[p]) for p in pairs]

    scores = [lax.dot_general(jnp.concatenate([at[p], rt[p]], axis=0).astype(bf16),
                              jnp.concatenate([bt[p], kt[p]], axis=0).astype(bf16),
                              nt, preferred_element_type=f32) for p in pairs]
    a_ab = [jnp.where(strict_lower, scores[p][:P2, :P2], 0.0) for p in pairs]
    a_ak = [jnp.where(strict_lower, scores[p][:P2, P2:], 0.0) for p in pairs]
    a_rb = [jnp.where(incl_lower, scores[p][P2:, :P2], 0.0) for p in pairs]
    a_rk = [jnp.where(incl_lower, scores[p][P2:, P2:], 0.0) for p in pairs]
    x0 = [jnp.concatenate([at[p], _dotb(a_ak[p], vb[p])], axis=1) for p in pairs]
    eye_f = jnp.where(eye, 1.0, 0.0)
    inv = [eye_f + a_ab[p] for p in pairs]
    pw = [_dotb(a_ab[p], a_ab[p]) for p in pairs]
    for _ in range(int(np.log2(C)) - 2):
        both = [_dotb(pw[p], jnp.concatenate([pw[p], inv[p]], axis=1)) for p in pairs]
        inv = [inv[p] + both[p][:, P2:] for p in pairs]
        pw = [both[p][:, :P2] for p in pairs]
    inv = [inv[p] + _dotb(pw[p], inv[p]) for p in pairs]
    x = [_dotb(inv[p], x0[p]) for p in pairs]
    rhs = [jnp.concatenate([x[p], jnp.concatenate([zeros_pp, vb[p]], axis=1)], axis=0) for p in pairs]
    gz = [_dotb(jnp.concatenate([a_rb[p], a_rk[p]], axis=1), rhs[p]) for p in pairs]
    mn = [_dotb(jnp.concatenate([bh[p], kh[p]], axis=0).T, rhs[p]) for p in pairs]
    yh = []
    for p in pairs:
        g = rt[p] + gz[p][:, :P2]
        m = mn[p][:, :P2] + jnp.where(eye, jnp.broadcast_to(jnp.exp(tot[p]), (P2, P2)), 0.0)
        yh.append(_dotb(jnp.concatenate([g, m], axis=0), h_ref[p]))
    ys = []
    for p in pairs:
        y2 = yh[p][:P2] + gz[p][:, P2:]
        h_ref[p] = yh[p][P2:] + mn[p][:, P2:]
        ys.append(y2[:C] + y2[C:])
    mean = _split_dots(ys, head_ones)
    yc = [ys[p] - mean[p] * (1.0 / RWKV_HEAD_DIM) for p in pairs]
    var = _split_dots([yc[p] * yc[p] for p in pairs], head_ones)
    rk = _split_dots([r_p[p] * k_p[p] * r_k[:, sl[p]] for p in pairs], head_ones)
    for p in pairs:
        yn = yc[p] * lax.rsqrt(var[p] * (1.0 / RWKV_HEAD_DIM) + RWKV_GN_EPS) * lnx_g[:, sl[p]] + lnx_b[:, sl[p]]
        o_ref[:, sl[p]] = ((yn + rk[p] * v_p[p]) * gate_all[:, sl[p]]).astype(o_ref.dtype)


def rwkv_core(p, v_first, vres, mu, w0, w2, a0, a2, g2, k_k, k_a, r_k, lnx_g, lnx_b, B, S):
    C = WKV_CHUNK
    n = S // C
    has_vres = vres is not None
    r_mu, k_mu, v_mu, wd_mu, ad_mu, gd_mu = jnp.split(mu, RWKV_SPLITS)
    mu3 = jnp.stack([r_mu, k_mu, v_mu])
    mu_wa = jnp.concatenate([wd_mu, ad_mu]).reshape(1, RWKV_WA_PAD)
    mu_g = jnp.pad(gd_mu, (0, RWKV_G_PAD - RWKV_LORA_G)).reshape(1, RWKV_G_PAD)
    w2p = jnp.pad(w2, ((0, RWKV_WA_PAD - RWKV_LORA_W), (0, 0))).astype(bf16)
    a2p = jnp.pad(a2, ((RWKV_LORA_W, 0), (0, 0))).astype(bf16)
    g2p = jnp.pad(g2, ((0, RWKV_G_PAD - RWKV_LORA_G), (0, 0))).astype(bf16)
    vecs = [w0, a0, k_k, k_a, r_k.reshape(MIX_B), lnx_g, lnx_b]
    if has_vres:
        vecs.append(vres[0])
    vec = jnp.stack(vecs)
    nv = vec.shape[0]
    row = lambda bi, ci: bi * n + ci
    const = lambda bi, ci: (0, 0)
    tile = lambda w, j: pl.BlockSpec((C, w), lambda bi, ci: (row(bi, ci), j))
    in_specs = [tile(MIX_B, EV_R0 // MIX_B), tile(MIX_B, EV_R0 // MIX_B + 1), tile(MIX_B, EV_R0 // MIX_B + 2),
                tile(RWKV_WA_PAD, EV_WA0 // RWKV_WA_PAD),
                tile(RWKV_G_PAD // 2, EV_G0 // (RWKV_G_PAD // 2)), tile(RWKV_G_PAD // 2, EV_G0 // (RWKV_G_PAD // 2) + 1)]
    args = [p, p, p, p, p, p]
    if has_vres:
        in_specs.append(tile(MIX_B, 0))
        args.append(v_first)
    in_specs += [pl.BlockSpec((3, MIX_B), const), pl.BlockSpec((1, RWKV_WA_PAD), const),
                 pl.BlockSpec((1, RWKV_G_PAD), const), pl.BlockSpec((RWKV_WA_PAD, MIX_B), const),
                 pl.BlockSpec((RWKV_WA_PAD, MIX_B), const), pl.BlockSpec((RWKV_G_PAD, MIX_B), const),
                 pl.BlockSpec((nv, MIX_B), const)]
    args += [mu3, mu_wa, mu_g, w2p, a2p, g2p, vec]
    out_specs = [tile(MIX_B, 0)]
    out_shape = [jax.ShapeDtypeStruct((B * S, MIX_B), bf16)]
    if has_vres:
        v1p = jnp.pad(vres[1], ((0, 0), (0, RWKV_V_PAD - vres[1].shape[1]))).astype(bf16)
        v2p = jnp.pad(vres[2], ((0, RWKV_V_PAD - vres[2].shape[0]), (0, 0))).astype(bf16)
        in_specs += [pl.BlockSpec((MIX_B, RWKV_V_PAD), const), pl.BlockSpec((RWKV_V_PAD, MIX_B), const)]
        args += [v1p, v2p]
    else:
        out_specs.append(tile(MIX_B, 0))
        out_shape.append(jax.ShapeDtypeStruct((B * S, MIX_B), f32))
    outs = pl.pallas_call(
        functools.partial(_rwkv_kernel, has_vres),
        grid=(B, n),
        in_specs=in_specs,
        out_specs=out_specs,
        out_shape=out_shape,
        scratch_shapes=[pltpu.VMEM((WKV_PAIRS, WKV_PAIR, WKV_PAIR), f32),
                        pltpu.VMEM((3, 8, MIX_B), f32), pltpu.VMEM((1, 8, RWKV_WA_PAD), f32),
                        pltpu.VMEM((1, 8, RWKV_G_PAD), f32)],
        compiler_params=pltpu.CompilerParams(
            dimension_semantics=("parallel", "arbitrary"), vmem_limit_bytes=VMEM_LIMIT_BYTES),
        name="rwkv7_time_mix",
    )(*args)
    return (outs[0], v_first) if has_vres else (outs[0], outs[1])


def even_win_layout(win):
    return jnp.pad(win, ((0, 0), (0, EVEN_COLS - win.shape[1])))


RET_LOG_GAMMA = tuple(float(np.log1p(-np.exp2(-5.0 - h))) for h in range(RET_HEADS))
ROPE_HALF = RET_HEAD_DIM // 2


def _ret_kernel(q_ref, k_ref, v_ref, g_ref, pos_ref, inv_ref, gng_ref, gnb_ref, o_ref, s_ref):
    C = RET_CHUNK
    d = RET_HEAD_DIM

    @pl.when(pl.program_id(1) == 0)
    def _():
        s_ref[...] = jnp.zeros_like(s_ref)

    ang = pos_ref[...] * inv_ref[...]
    cos = jnp.cos(ang)
    sin = jnp.sin(ang)

    def rot(x):
        x1, x2 = x[:, :ROPE_HALF], x[:, ROPE_HALF:]
        return jnp.concatenate([x1 * cos - x2 * sin, x1 * sin + x2 * cos], axis=-1)

    ri = lax.broadcasted_iota(jnp.int32, (C, C), 0)
    ci = lax.broadcasted_iota(jnp.int32, (C, C), 1)
    diff = (ri - ci).astype(f32)
    row = lax.broadcasted_iota(jnp.int32, (C, d), 0).astype(f32)
    nt = (((1,), (1,)), ((), ()))

    for h in range(RET_HEADS):
        hs = slice(h * d, (h + 1) * d)
        lg = RET_LOG_GAMMA[h]
        q = rot(q_ref[:, hs])
        k = rot(k_ref[:, hs]) * (d ** -0.5)
        v = v_ref[:, hs].astype(bf16)
        decay = jnp.where(diff >= 0, jnp.exp(lg * jnp.maximum(diff, 0.0)), 0.0)
        scores = lax.dot_general(q.astype(bf16), k.astype(bf16), nt, preferred_element_type=f32) * decay
        state = s_ref[h]
        y = jnp.dot(scores.astype(bf16), v, preferred_element_type=f32)
        y = y + jnp.dot((q * jnp.exp(lg * (row + 1.0))).astype(bf16), state.astype(bf16), preferred_element_type=f32)
        kw = (k * jnp.exp(lg * (C - 1.0 - row))).astype(bf16)
        s_ref[h] = state * float(np.exp(lg * C)) + jnp.dot(kw.T, v, preferred_element_type=f32)
        mu = jnp.mean(y, axis=-1, keepdims=True)
        yc = y - mu
        var = jnp.mean(yc * yc, axis=-1, keepdims=True)
        yn = yc * lax.rsqrt(var + LN_EPS) * gng_ref[:, hs] + gnb_ref[:, hs]
        g = g_ref[:, hs]
        o_ref[:, hs] = (g * jax.nn.sigmoid(g) * yn).astype(o_ref.dtype)


def retention_core(p, positions, gn_g, gn_b, B, S):
    C = RET_CHUNK
    n = S // C
    inv = (ROPE_BASE ** (-jnp.arange(ROPE_HALF, dtype=f32) / ROPE_HALF)).reshape(1, ROPE_HALF)
    pos = positions.astype(f32).reshape(B * S, 1)
    row = lambda bi, ci: bi * n + ci
    const = lambda bi, ci: (0, 0)
    col = lambda j: pl.BlockSpec((C, MIX_A), lambda bi, ci: (row(bi, ci), j))
    return pl.pallas_call(
        _ret_kernel,
        grid=(B, n),
        in_specs=[col(0), col(1), col(2), col(3),
                  pl.BlockSpec((C, 1), lambda bi, ci: (row(bi, ci), 0)),
                  pl.BlockSpec((1, ROPE_HALF), const),
                  pl.BlockSpec((1, MIX_A), const),
                  pl.BlockSpec((1, MIX_A), const)],
        out_specs=pl.BlockSpec((C, MIX_A), lambda bi, ci: (row(bi, ci), 0)),
        out_shape=jax.ShapeDtypeStruct((B * S, MIX_A), bf16),
        scratch_shapes=[pltpu.VMEM((RET_HEADS, RET_HEAD_DIM, RET_HEAD_DIM), f32)],
        compiler_params=pltpu.CompilerParams(
            dimension_semantics=("parallel", "arbitrary"), vmem_limit_bytes=VMEM_LIMIT_BYTES),
        name="retention_chunked",
    )(p, p, p, p, pos, inv, gn_g.reshape(1, MIX_A), gn_b.reshape(1, MIX_A))


GLA_LEVELS = int(np.log2(GLA_CHUNK))
GLA_DECAY_ROWS = (2 + GLA_LEVELS) * GLA_CHUNK
GLA_AD_PAD = 128
ODD_PROJ_PAD = 2 * GLA_DK + 2 * GLA_DV + GLA_AD_PAD


def _gla_decay_operator():
    C = GLA_CHUNK
    i = np.arange(C)[:, None]
    j = np.arange(C)[None, :]
    blocks = [(j <= i), (j > i)]
    for lvl in range(GLA_LEVELS):
        s = C >> (lvl + 1)
        mid = (i // (2 * s)) * (2 * s) + s
        right = i >= mid
        blocks.append(np.where(right, (j >= mid) & (j <= i), (j > i) & (j < mid)))
    return np.concatenate(blocks, 0).astype(np.float32)


def _gla_kernel(q_ref, k_ref, v_ref, r_ref, ad_ref, a2_ref, ab_ref, g_ref, lop_ref, o_ref, s_ref):
    C = GLA_CHUNK

    @pl.when(pl.program_id(1) == 0)
    def _():
        s_ref[...] = jnp.zeros_like(s_ref)

    x = jnp.dot(ad_ref[...].astype(bf16), a2_ref[...], preferred_element_type=f32) + ab_ref[...]
    la = (jnp.minimum(x, 0.0) - jnp.log1p(jnp.exp(-jnp.abs(x)))) * (1.0 / GLA_GATE_NORM)
    hi = la.astype(bf16)
    r1 = la - hi.astype(f32)
    mid = r1.astype(bf16)
    lo = (r1 - mid.astype(f32)).astype(bf16)
    lop = lop_ref[...]
    dec = (jnp.dot(lop, hi, preferred_element_type=f32) + jnp.dot(lop, mid, preferred_element_type=f32)
           + jnp.dot(lop, lo, preferred_element_type=f32))

    ri = lax.broadcasted_iota(jnp.int32, (C, C), 0)
    ci = lax.broadcasted_iota(jnp.int32, (C, C), 1)
    level_masks = []
    for lvl in range(GLA_LEVELS):
        s = C >> (lvl + 1)
        same_block = (ri // (2 * s)) == (ci // (2 * s))
        level_masks.append(same_block & ((ri % (2 * s)) >= s) & ((ci % (2 * s)) < s))
    eye = ri == ci
    nt = (((1,), (1,)), ((), ()))

    for h in range(GLA_HEADS):
        ks = slice(h * GLA_HEAD_DK, (h + 1) * GLA_HEAD_DK)
        vs = slice(h * GLA_HEAD_DV, (h + 1) * GLA_HEAD_DV)
        q = q_ref[:, ks] * (GLA_HEAD_DK ** -0.5)
        k = k_ref[:, ks]
        v = v_ref[:, vs].astype(bf16)
        e = jnp.exp(dec[:, ks])
        state = s_ref[h]
        o = lax.dot_general((q * e[:C]).astype(bf16), state.astype(bf16), nt, preferred_element_type=f32)
        attn = jnp.where(eye, lax.dot_general(q.astype(bf16), k.astype(bf16), nt, preferred_element_type=f32), 0.0)
        for lvl in range(GLA_LEVELS):
            el = e[(2 + lvl) * C:(3 + lvl) * C]
            sc = lax.dot_general((q * el).astype(bf16), (k * el).astype(bf16), nt, preferred_element_type=f32)
            attn = attn + jnp.where(level_masks[lvl], sc, 0.0)
        o = o + jnp.dot(attn.astype(bf16), v, preferred_element_type=f32)
        kd = (k * e[C:2 * C]).astype(bf16)
        s_ref[h] = state * e[C - 1:C] + jnp.dot(v.T, kd, preferred_element_type=f32)
        o = o * lax.rsqrt(jnp.mean(o * o, axis=-1, keepdims=True) + LN_EPS) * g_ref[:, vs]
        rr = r_ref[:, vs]
        o_ref[:, vs] = (rr * jax.nn.sigmoid(rr) * o).astype(o_ref.dtype)


def gla_core(p, a2, ab, norm_g, B, S):
    C = GLA_CHUNK
    n = S // C
    a2p = jnp.pad(a2, ((0, GLA_AD_PAD - GLA_LORA), (0, 0))).astype(bf16)
    lop = jnp.asarray(_gla_decay_operator(), dtype=bf16)
    row = lambda bi, ci: bi * n + ci
    const = lambda bi, ci: (0, 0)
    return pl.pallas_call(
        _gla_kernel,
        grid=(B, n),
        in_specs=[pl.BlockSpec((C, GLA_DK), lambda bi, ci: (row(bi, ci), 0)),
                  pl.BlockSpec((C, GLA_DK), lambda bi, ci: (row(bi, ci), 1)),
                  pl.BlockSpec((C, GLA_DV), lambda bi, ci: (row(bi, ci), 1)),
                  pl.BlockSpec((C, GLA_DV), lambda bi, ci: (row(bi, ci), 2)),
                  pl.BlockSpec((C, GLA_AD_PAD), lambda bi, ci: (row(bi, ci), (2 * GLA_DK + 2 * GLA_DV) // GLA_AD_PAD)),
                  pl.BlockSpec((GLA_AD_PAD, GLA_DK), const),
                  pl.BlockSpec((1, GLA_DK), const),
                  pl.BlockSpec((1, GLA_DV), const),
                  pl.BlockSpec((GLA_DECAY_ROWS, C), const)],
        out_specs=pl.BlockSpec((C, GLA_DV), lambda bi, ci: (row(bi, ci), 0)),
        out_shape=jax.ShapeDtypeStruct((B * S, GLA_DV), bf16),
        scratch_shapes=[pltpu.VMEM((GLA_HEADS, GLA_HEAD_DV, GLA_HEAD_DK), f32)],
        compiler_params=pltpu.CompilerParams(
            dimension_semantics=("parallel", "arbitrary"), vmem_limit_bytes=VMEM_LIMIT_BYTES),
        name="gla_chunked",
    )(p, p, p, p, p, a2p, ab.reshape(1, GLA_DK), norm_g.reshape(1, GLA_DV), lop)


PROJ_TM = 256
ROUTER_PAD = 128


def _proj_norm_kernel(n_in, *refs):
    a_refs = refs[:n_in]
    w_ref, x_ref, mod_ref, g_ref, b_ref, wr_ref, br_ref, xo_ref, ho_ref, lo_ref = refs[n_in:]
    acc = None
    k0 = 0
    for a_ref in a_refs:
        kw = a_ref.shape[1]
        part = jnp.dot(a_ref[...], w_ref[k0:k0 + kw, :], preferred_element_type=f32)
        acc = part if acc is None else acc + part
        k0 += kw
    gt, sc, sh = mod_ref[0, 0:1, :], mod_ref[0, 1:2, :], mod_ref[0, 2:3, :]
    z = DEEPNORM_ALPHA * x_ref[...] + (1.0 + gt) * acc
    mu = jnp.mean(z, axis=-1, keepdims=True)
    zc = z - mu
    var = jnp.mean(zc * zc, axis=-1, keepdims=True)
    xn = zc * lax.rsqrt(var + LN_EPS) * g_ref[...] + b_ref[...]
    xo_ref[...] = xn
    hf = xn * (1.0 + sc) + sh
    ho_ref[...] = hf
    lo_ref[...] = jnp.dot(hf.astype(bf16), wr_ref[...], preferred_element_type=f32) + br_ref[...]


def proj_norm(acts, wo, x, mod, ln_g, ln_b, w_router, b_router, S):
    T, D = x.shape
    tiles_per_batch = S // PROJ_TM
    const = lambda i: (0, 0)
    rowblk = lambda w: pl.BlockSpec((PROJ_TM, w), lambda i: (i, 0))
    in_specs = [rowblk(a.shape[1]) for a in acts]
    in_specs += [pl.BlockSpec(wo.shape, const), rowblk(D),
                 pl.BlockSpec((1, 3, D), lambda i: (i // tiles_per_batch, 0, 0)),
                 pl.BlockSpec((1, D), const), pl.BlockSpec((1, D), const),
                 pl.BlockSpec((D, ROUTER_PAD), const), pl.BlockSpec((1, ROUTER_PAD), const)]
    return pl.pallas_call(
        functools.partial(_proj_norm_kernel, len(acts)),
        grid=(T // PROJ_TM,),
        in_specs=in_specs,
        out_specs=[rowblk(D), rowblk(D), rowblk(ROUTER_PAD)],
        out_shape=[jax.ShapeDtypeStruct((T, D), f32), jax.ShapeDtypeStruct((T, D), f32),
                   jax.ShapeDtypeStruct((T, ROUTER_PAD), f32)],
        compiler_params=pltpu.CompilerParams(
            dimension_semantics=("parallel",), vmem_limit_bytes=VMEM_LIMIT_BYTES),
        name="proj_norm",
    )(*acts, wo.astype(bf16), x, mod, ln_g.reshape(1, D), ln_b.reshape(1, D), w_router, b_router)


MOE_ROWS = 256
MOE_VMEM_LIMIT_BYTES = 56 * 1024 * 1024
COMBINE_ROWS = 256
SUBLANES = 8
GATHER_DMA_PRIORITY = 1


def _moe_ffn_kernel(layer, tok_ref, be_ref, nused_ref, x_hbm, w1_hbm, w3_hbm, w2_hbm, y_ref,
                    xbuf, w1buf, w3buf, w2buf, sem, wsem, run_ref):
    i = pl.program_id(0)
    nblk = pl.num_programs(0)
    n_used = nused_ref[0]
    slot = lax.rem(i, 2)
    expert = be_ref[i]
    first = (i < n_used) & ((i == 0) | (expert != be_ref[jnp.maximum(i - 1, 0)]))

    n_groups = MOE_ROWS // SUBLANES

    def more_groups(blk):
        return lambda g: (g < n_groups) & (tok_ref[blk * MOE_ROWS + jnp.minimum(g, n_groups - 1) * SUBLANES] >= 0)

    def start_gather(blk, s):
        def body(g):
            for u in range(SUBLANES):
                t = jnp.maximum(tok_ref[blk * MOE_ROWS + g * SUBLANES + u], 0)
                pltpu.make_async_copy(x_hbm.at[lax.shift_right_logical(t, 3), pl.ds(jnp.bitwise_and(t, SUBLANES - 1), 1)],
                                      xbuf.at[s, g, pl.ds(u, 1)], sem.at[s]).start(priority=GATHER_DMA_PRIORITY)
            return g + 1
        lax.while_loop(more_groups(blk), body, 0)

    def wait_gather(blk, s):
        def body(g):
            pltpu.make_async_copy(x_hbm.at[pl.ds(0, 1)], xbuf.at[s, pl.ds(g, 1)], sem.at[s]).wait()
            return g + 1
        lax.while_loop(more_groups(blk), body, 0)

    def weight_copies(e, b):
        return [pltpu.make_async_copy(w_hbm.at[layer, e], wbuf.at[b], wsem.at[b])
                for w_hbm, wbuf in ((w1_hbm, w1buf), (w3_hbm, w3buf), (w2_hbm, w2buf))]

    @pl.when((i == 0) & (n_used > 0))
    def _():
        run_ref[0] = 0
        xbuf[...] = jnp.zeros_like(xbuf)
        for cp in weight_copies(expert, 0):
            cp.start()
        start_gather(0, 0)

    @pl.when(i + 1 < n_used)
    def _():
        start_gather(i + 1, 1 - slot)

    @pl.when(first)
    def _():
        @pl.when(i > 0)
        def _():
            run_ref[0] = run_ref[0] + 1
        buf = jnp.bitwise_and(run_ref[0], 1)
        nxt = lax.while_loop(lambda j: (j < n_used) & (be_ref[jnp.minimum(j, nblk - 1)] == expert),
                             lambda j: j + 1, i + 1)

        @pl.when(nxt < n_used)
        def _():
            for cp in weight_copies(be_ref[jnp.minimum(nxt, nblk - 1)], 1 - buf):
                cp.start()
        for cp in weight_copies(expert, buf):
            cp.wait()

    @pl.when(i < n_used)
    def _():
        buf = jnp.bitwise_and(run_ref[0], 1)
        wait_gather(i, slot)
        xb = xbuf[slot].reshape(MOE_ROWS, xbuf.shape[-1]).astype(bf16)
        a = jnp.dot(xb, w1buf[buf].astype(bf16), preferred_element_type=f32)
        b = jnp.dot(xb, w3buf[buf].astype(bf16), preferred_element_type=f32)
        hid = (a * jax.nn.sigmoid(a) * b).astype(bf16)
        y_ref[...] = jnp.dot(hid, w2buf[buf].astype(bf16), preferred_element_type=f32)

    @pl.when(i >= n_used)
    def _():
        y_ref[...] = jnp.zeros_like(y_ref)


def moe_ffn(x, buf_tok, blk_expert, n_used, w1, w3, w2, layer):
    D = x.shape[1]
    nblk = blk_expert.shape[0]
    any_spec = pl.BlockSpec(memory_space=pl.ANY)
    return pl.pallas_call(
        functools.partial(_moe_ffn_kernel, layer),
        grid_spec=pltpu.PrefetchScalarGridSpec(
            num_scalar_prefetch=3,
            grid=(nblk,),
            in_specs=[any_spec, any_spec, any_spec, any_spec],
            out_specs=pl.BlockSpec((MOE_ROWS, D), lambda i, tok, be, nu: (i, 0)),
            scratch_shapes=[pltpu.VMEM((2, MOE_ROWS // SUBLANES, SUBLANES, D), f32),
                            pltpu.VMEM((2, D, EXPERT_FF), f32), pltpu.VMEM((2, D, EXPERT_FF), f32),
                            pltpu.VMEM((2, EXPERT_FF, D), f32),
                            pltpu.SemaphoreType.DMA((2,)), pltpu.SemaphoreType.DMA((2,)),
                            pltpu.SMEM((1,), jnp.int32)]),
        out_shape=jax.ShapeDtypeStruct((nblk * MOE_ROWS, D), f32),
        compiler_params=pltpu.CompilerParams(
            dimension_semantics=("arbitrary",), vmem_limit_bytes=MOE_VMEM_LIMIT_BYTES),
        name="moe_expert_ffn",
    )(buf_tok, blk_expert, n_used, x.reshape(x.shape[0] // SUBLANES, SUBLANES, D), w1, w3, w2)


def _moe_combine_kernel(emit_h, s0_ref, s1_ref, y_hbm, gate_ref, x_ref, mod_ref, g_ref, b_ref, *rest):
    if emit_h:
        o_ref, h_ref, buf, sem = rest
    else:
        o_ref, buf, sem = rest
    i = pl.program_id(0)
    n = pl.num_programs(0)
    slot = lax.rem(i, 2)

    def start_gather(blk, s):
        def body(g, carry):
            base = pl.multiple_of(g * SUBLANES, SUBLANES)
            for u in range(SUBLANES):
                t = blk * COMBINE_ROWS + base + u
                pltpu.make_async_copy(y_hbm.at[pl.ds(s0_ref[t], 1)], buf.at[s, 0, pl.ds(base + u, 1)], sem.at[s]).start()
                pltpu.make_async_copy(y_hbm.at[pl.ds(s1_ref[t], 1)], buf.at[s, 1, pl.ds(base + u, 1)], sem.at[s]).start()
            return carry
        lax.fori_loop(0, COMBINE_ROWS // SUBLANES, body, 0)

    @pl.when(i == 0)
    def _():
        start_gather(0, 0)

    @pl.when(i + 1 < n)
    def _():
        start_gather(i + 1, 1 - slot)

    for c in range(MOE_TOPK):
        pltpu.make_async_copy(y_hbm.at[pl.ds(0, COMBINE_ROWS)], buf.at[slot, c], sem.at[slot]).wait()
    y = buf[slot, 0] * gate_ref[:, 0:1] + buf[slot, 1] * gate_ref[:, 1:2]
    z = DEEPNORM_ALPHA * x_ref[...] + (1.0 + mod_ref[0, 0:1, :]) * y
    mu = jnp.mean(z, axis=-1, keepdims=True)
    zc = z - mu
    var = jnp.mean(zc * zc, axis=-1, keepdims=True)
    xn = zc * lax.rsqrt(var + LN_EPS) * g_ref[...] + b_ref[...]
    o_ref[...] = xn
    if emit_h:
        h_ref[...] = (xn * (1.0 + mod_ref[0, 1:2, :]) + mod_ref[0, 2:3, :]).astype(h_ref.dtype)


def moe_combine_norm(y_rows, slot0, slot1, gate, x, mod, ln_g, ln_b, S, emit_h):
    T, D = x.shape
    tiles_per_batch = S // COMBINE_ROWS
    const = lambda i, s0, s1: (0, 0)
    rowblk = pl.BlockSpec((COMBINE_ROWS, D), lambda i, s0, s1: (i, 0))
    out_specs = [rowblk, rowblk] if emit_h else [rowblk]
    out_shape = [jax.ShapeDtypeStruct((T, D), f32)] + ([jax.ShapeDtypeStruct((T, D), bf16)] if emit_h else [])
    return pl.pallas_call(
        functools.partial(_moe_combine_kernel, emit_h),
        grid_spec=pltpu.PrefetchScalarGridSpec(
            num_scalar_prefetch=2,
            grid=(T // COMBINE_ROWS,),
            in_specs=[pl.BlockSpec(memory_space=pl.ANY),
                      pl.BlockSpec((COMBINE_ROWS, MOE_TOPK), lambda i, s0, s1: (i, 0)), rowblk,
                      pl.BlockSpec((1, 3, D), lambda i, s0, s1: (i // tiles_per_batch, 0, 0)),
                      pl.BlockSpec((1, D), const), pl.BlockSpec((1, D), const)],
            out_specs=out_specs,
            scratch_shapes=[pltpu.VMEM((2, MOE_TOPK, COMBINE_ROWS, D), f32), pltpu.SemaphoreType.DMA((2,))]),
        out_shape=out_shape,
        compiler_params=pltpu.CompilerParams(
            dimension_semantics=("arbitrary",), vmem_limit_bytes=VMEM_LIMIT_BYTES),
        name="moe_combine_norm",
    )(slot0, slot1, y_rows, gate, x, mod, ln_g.reshape(1, D), ln_b.reshape(1, D))


def router_params(wg, bg, we, be):
    D = wg.shape[0]
    pad = ROUTER_PAD - MOE_GROUPS - N_EXPERTS
    w = jnp.concatenate([wg, we, jnp.zeros((D, pad), wg.dtype)], 1).astype(bf16)
    b = jnp.concatenate([bg, be, jnp.zeros((pad,), bg.dtype)]).reshape(1, ROUTER_PAD)
    return w, b


def moe_route(logits):
    g_prob = jax.nn.softmax(logits[:, :MOE_GROUPS], -1)
    g_top = jnp.argmax(g_prob, -1).astype(jnp.int32)
    p_top = jnp.max(g_prob, -1)
    ids = jnp.arange(N_EXPERTS, dtype=jnp.int32)[None, :]
    e_sel = jnp.where(ids // MOE_PER_GROUP == g_top[:, None], logits[:, MOE_GROUPS:MOE_GROUPS + N_EXPERTS], -jnp.inf)
    i1 = jnp.argmax(e_sel, -1).astype(jnp.int32)
    v1 = jnp.max(e_sel, -1)
    rest = jnp.where(ids == i1[:, None], -jnp.inf, e_sel)
    i2 = jnp.argmax(rest, -1).astype(jnp.int32)
    v2 = jnp.max(rest, -1)
    gate = p_top[:, None] * jax.nn.softmax(jnp.stack([v1, v2], -1), -1)
    return gate, jnp.stack([i1, i2], -1)


def moe_layout(expert):
    T = expert.shape[0]
    A = T * MOE_TOPK
    e_flat = expert.reshape(A)
    onehot = (e_flat[:, None] == jnp.arange(N_EXPERTS, dtype=jnp.int32)[None, :]).astype(jnp.int32)
    rank = jnp.cumsum(onehot, axis=0)
    counts = rank[-1]
    padded = (counts + MOE_ROWS - 1) // MOE_ROWS * MOE_ROWS
    pad_end = jnp.cumsum(padded)
    pad_start = pad_end - padded
    slot = jnp.sum(onehot * (pad_start[None, :] + rank - 1), axis=1).astype(jnp.int32)
    tok_flat = jnp.arange(A, dtype=jnp.int32) // MOE_TOPK
    nblk = A // MOE_ROWS + N_EXPERTS
    buf_tok = jnp.full((nblk * MOE_ROWS,), -1, jnp.int32).at[slot].set(tok_flat)
    n_used = (pad_end[-1] // MOE_ROWS).astype(jnp.int32)
    blk = jnp.minimum(jnp.arange(nblk, dtype=jnp.int32), n_used - 1) * MOE_ROWS
    blk_expert = jnp.minimum(jnp.sum(blk[:, None] >= pad_end[None, :], axis=1), N_EXPERTS - 1).astype(jnp.int32)
    slot2 = slot.reshape(T, MOE_TOPK)
    return buf_tok, blk_expert, n_used.reshape(1), slot2[:, 0], slot2[:, 1]


def kernel(x, c, positions, ada_w, ada_b, ln_g, ln_b, ev_win, ev_wo, ret_gn_g, ret_gn_b, rw_mu, rw_w0, rw_w2, rw_a0, rw_a2, rw_g2, rw_kk, rw_ka, rw_rk, rw_lnx_g, rw_lnx_b, rw_v0, rw_v1, rw_v2, od_win, od_wo, gla_a2, gla_ab, gla_norm_g, moe_wg, moe_bg, moe_we, moe_be, moe_w1, moe_w3, moe_w2):
    B, S, D = x.shape
    T = B * S
    depth = ada_w.shape[0]
    mods = ada_mod(c, ada_w, ada_b).reshape(depth, B, 6, D)
    h = (x * (1.0 + mods[0, :, 1])[:, None, :] + mods[0, :, 0][:, None, :]).astype(bf16).reshape(T, D)
    x = x.reshape(T, D)
    v_first = None
    for layer in range(depth):
        j = layer // 2
        mixer_mod = jnp.stack([mods[layer, :, 2], mods[layer, :, 4], mods[layer, :, 3]], axis=1)
        if layer % 2 == 0:
            p = matmul(h, even_win_layout(ev_win[j]), tn=EVEN_TN)
            out_a = retention_core(p, positions, ret_gn_g[j], ret_gn_b[j], B, S)
            vres = None if j == 0 else (rw_v0[j - 1], rw_v1[j - 1], rw_v2[j - 1])
            out_b, v_first = rwkv_core(p, v_first, vres, rw_mu[j], rw_w0[j], rw_w2[j], rw_a0[j], rw_a2[j], rw_g2[j],
                                       rw_kk[j], rw_ka[j], rw_rk[j], rw_lnx_g[j], rw_lnx_b[j], B, S)
            acts, wo = [out_a, out_b], ev_wo[j]
        else:
            p = matmul(h, jnp.pad(od_win[j], ((0, 0), (0, ODD_PROJ_PAD - ODD_PROJ))), tn=896)
            acts, wo = [gla_core(p, gla_a2[j], gla_ab[j], gla_norm_g[j], B, S)], od_wo[j]
        w_router, b_router = router_params(moe_wg[layer], moe_bg[layer], moe_we[layer], moe_be[layer])
        x, hf, logits = proj_norm(acts, wo, x, mixer_mod, ln_g[layer, 0], ln_b[layer, 0], w_router, b_router, S)
        gate, expert = moe_route(logits)
        buf_tok, blk_expert, n_used, slot0, slot1 = moe_layout(expert)
        y_rows = moe_ffn(hf, buf_tok, blk_expert, n_used, moe_w1, moe_w3, moe_w2, layer)
        last = layer == depth - 1
        nxt = layer if last else layer + 1
        moe_mod = jnp.stack([mods[layer, :, 5], mods[nxt, :, 1], mods[nxt, :, 0]], axis=1)
        outs = moe_combine_norm(y_rows, slot0, slot1, gate, x, moe_mod, ln_g[layer, 1], ln_b[layer, 1], S, not last)
        x = outs[0]
        if not last:
            h = outs[1]
    return x.reshape(B, S, D)
```

```python
import functools

import jax
import jax.numpy as jnp
import numpy as np
from jax import lax
from jax.experimental import pallas as pl
from jax.experimental.pallas import tpu as pltpu

D_MODEL = 2048
DEPTH = 4
MIX_A = D_MODEL // 2
RET_HEAD_DIM = 256
RET_HEADS = MIX_A // RET_HEAD_DIM
RET_CHUNK = 128
RET_COLS = 4 * MIX_A
ROPE_BASE = 10000.0
MIX_B = D_MODEL - MIX_A
RWKV_HEAD_DIM = 64
RWKV_HEADS = MIX_B // RWKV_HEAD_DIM
RWKV_LORA_W = 64
RWKV_LORA_A = 64
RWKV_LORA_G = 160
RWKV_COLS = 3 * MIX_B + RWKV_LORA_W + RWKV_LORA_A + RWKV_LORA_G
RWKV_SPLITS = (MIX_B, 2 * MIX_B, 3 * MIX_B, 3 * MIX_B + RWKV_LORA_W, 3 * MIX_B + RWKV_LORA_W + RWKV_LORA_A)
RWKV_GN_EPS = 64e-5
GLA_HEADS = 4
GLA_DK = D_MODEL // 2
GLA_DV = D_MODEL
GLA_HEAD_DK = GLA_DK // GLA_HEADS
GLA_HEAD_DV = GLA_DV // GLA_HEADS
GLA_LORA = 16
GLA_GATE_NORM = 16.0
GLA_CHUNK = 64
ODD_PROJ = 2 * GLA_DK + 2 * GLA_DV + GLA_LORA
MOE_GROUPS = 4
MOE_PER_GROUP = 8
N_EXPERTS = MOE_GROUPS * MOE_PER_GROUP
MOE_TOPK = 2
EXPERT_FF = D_MODEL // 4
DEEPNORM_ALPHA = (2 * DEPTH) ** 0.25
LN_EPS = 1e-5

f32 = jnp.float32
bf16 = jnp.bfloat16

VMEM_LIMIT_BYTES = 48 * 1024 * 1024
DENSE_VMEM_LIMIT_BYTES = 56 * 1024 * 1024


def _dotb(a, b):
    return jnp.dot(a.astype(bf16), b.astype(bf16), preferred_element_type=f32)


def _split_dots(xs, ones_bf16):
    n, rows = len(xs), xs[0].shape[0]
    his = [x.astype(bf16) for x in xs]
    los = [(x - h.astype(f32)).astype(bf16) for x, h in zip(xs, his)]
    out = jnp.dot(jnp.concatenate(his + los, axis=0), ones_bf16, preferred_element_type=f32)
    return [out[i * rows:(i + 1) * rows] + out[(n + i) * rows:(n + i + 1) * rows] for i in range(n)]


def _softplus(z):
    return jnp.maximum(z, 0.0) + jnp.log1p(jnp.exp(-jnp.abs(z)))


ADA_ROWS = 8
ADA_TN = 1024


def _ada_kernel(c_ref, w_ref, b_ref, o_ref):
    c = c_ref[...]
    act = (c * jax.nn.sigmoid(c)).astype(bf16)
    o_ref[0] = jnp.dot(act, w_ref[0].astype(bf16), preferred_element_type=f32) + b_ref[0]


def ada_mod(c, ada_w, ada_b):
    B, D = c.shape
    depth, _, N = ada_w.shape
    c_pad = jnp.pad(c, ((0, ADA_ROWS - B), (0, 0)))
    out = pl.pallas_call(
        _ada_kernel,
        grid=(depth, N // ADA_TN),
        in_specs=[pl.BlockSpec((ADA_ROWS, D), lambda l, j: (0, 0)),
                  pl.BlockSpec((1, D, ADA_TN), lambda l, j: (l, 0, j)),
                  pl.BlockSpec((1, 1, ADA_TN), lambda l, j: (l, 0, j))],
        out_specs=pl.BlockSpec((1, ADA_ROWS, ADA_TN), lambda l, j: (l, 0, j)),
        out_shape=jax.ShapeDtypeStruct((depth, ADA_ROWS, N), f32),
        compiler_params=pltpu.CompilerParams(
            dimension_semantics=("parallel", "parallel"), vmem_limit_bytes=VMEM_LIMIT_BYTES),
        name="ada_mod",
    )(c_pad, ada_w, ada_b.reshape(depth, 1, N))
    return out[:, :B]


def _matmul_kernel(x_ref, w_ref, o_ref):
    o_ref[...] = jnp.dot(x_ref[...], w_ref[0], preferred_element_type=f32)


def padded_bf16(w, cols):
    return jnp.pad(w, ((0, 0), (0, 0), (0, cols - w.shape[2]))).astype(bf16)


def matmul(x, w, layer, tm=2048, tn=512):
    M, K = x.shape
    Np = w.shape[2]
    assert Np % tn == 0 and M % min(tm, M) == 0
    tm = min(tm, M)
    return pl.pallas_call(
        _matmul_kernel,
        grid=(M // tm, Np // tn),
        in_specs=[pl.BlockSpec((tm, K), lambda i, j: (i, 0)),
                  pl.BlockSpec((1, K, tn), lambda i, j: (layer, 0, j))],
        out_specs=pl.BlockSpec((tm, tn), lambda i, j: (i, j)),
        out_shape=jax.ShapeDtypeStruct((M, Np), f32),
        compiler_params=pltpu.CompilerParams(
            dimension_semantics=("parallel", "parallel"), vmem_limit_bytes=DENSE_VMEM_LIMIT_BYTES),
        name="dense_matmul",
    )(x, w)


WKV_CHUNK = 64
WKV_PAIR = 2 * RWKV_HEAD_DIM
WKV_PAIRS = MIX_B // WKV_PAIR
RWKV_WA_PAD = 128
RWKV_G_PAD = 256
RWKV_V_PAD = 128
EV_R0 = RET_COLS
EV_WA0 = EV_R0 + 3 * MIX_B
EV_G0 = EV_WA0 + RWKV_WA_PAD
EVEN_TN = 512
EVEN_COLS = -(-(EV_G0 + RWKV_G_PAD) // EVEN_TN) * EVEN_TN


def _rwkv_kernel(has_vres, *refs):
    if has_vres:
        (r_ref, k_ref, v_ref, wa_ref, gd0_ref, gd1_ref, vf_ref, mu_ref, muwa_ref, mug_ref, w2_ref, a2_ref, g2_ref,
         vec_ref, v1_ref, v2_ref, o_ref, h_ref, prev_ref, prevwa_ref, prevg_ref) = refs
    else:
        (r_ref, k_ref, v_ref, wa_ref, gd0_ref, gd1_ref, mu_ref, muwa_ref, mug_ref, w2_ref, a2_ref, g2_ref,
         vec_ref, o_ref, vf_ref, h_ref, prev_ref, prevwa_ref, prevg_ref) = refs
    C = WKV_CHUNK
    P2 = WKV_PAIR

    @pl.when(pl.program_id(1) == 0)
    def _():
        h_ref[...] = jnp.zeros_like(h_ref)
        prev_ref[...] = jnp.zeros_like(prev_ref)
        prevwa_ref[...] = jnp.zeros_like(prevwa_ref)
        prevg_ref[...] = jnp.zeros_like(prevg_ref)

    def shift(x, carry_ref, j, mu):
        first = lax.broadcasted_iota(jnp.int32, x.shape, 0) == 0
        prev = jnp.where(first, jnp.broadcast_to(carry_ref[j, 7:8, :], x.shape), pltpu.roll(x, 1, 0))
        carry_ref[j] = x[C - 8:, :]
        return x + (prev - x) * mu

    r_all = shift(r_ref[...], prev_ref, 0, mu_ref[0:1, :])
    k_all = shift(k_ref[...], prev_ref, 1, mu_ref[1:2, :])
    v_all = shift(v_ref[...], prev_ref, 2, mu_ref[2:3, :])
    wa = shift(wa_ref[...], prevwa_ref, 0, muwa_ref[...])
    gd = shift(jnp.concatenate([gd0_ref[...], gd1_ref[...]], axis=1), prevg_ref, 0, mug_ref[...])
    w0, a0, k_k, k_a, r_k, lnx_g, lnx_b = (vec_ref[i:i + 1, :] for i in range(7))

    w_log = -_softplus(-(w0 + _dotb(jnp.tanh(wa), w2_ref[...]))) - 0.5
    lw_all = -jnp.exp(w_log)
    a_all = jax.nn.sigmoid(a0 + _dotb(wa, a2_ref[...]))
    gate_all = _dotb(jax.nn.sigmoid(gd), g2_ref[...])
    if has_vres:
        v0 = vec_ref[7:8, :]
        mix = jax.nn.sigmoid(v0 + _dotb(_dotb(v_all, v1_ref[...]), v2_ref[...]))
        v_all = v_all + (vf_ref[...] - v_all) * mix
    else:
        vf_ref[...] = v_all
    kkraw_all = k_all * k_k
    k_all = k_all * (1.0 + (a_all - 1.0) * k_a)

    row = lax.broadcasted_iota(jnp.int32, (P2, P2), 0)
    col = lax.broadcasted_iota(jnp.int32, (P2, P2), 1)
    same_head = (row >= C) == (col >= RWKV_HEAD_DIM)
    head_ones = jnp.where((row >= RWKV_HEAD_DIM) == (col >= RWKV_HEAD_DIM), 1.0, 0.0).astype(bf16)
    t_idx = jnp.bitwise_and(row, C - 1)
    s_idx = jnp.bitwise_and(col, C - 1)
    strict_lower = t_idx > s_idx
    incl_lower = t_idx >= s_idx
    eye = row == col
    tri = (lax.broadcasted_iota(jnp.int32, (C, C), 0) >= lax.broadcasted_iota(jnp.int32, (C, C), 1)).astype(bf16)
    zeros_pp = jnp.zeros((P2, P2), f32)
    nt = (((1,), (1,)), ((), ()))
    pairs = range(WKV_PAIRS)
    sl = [slice(p * P2, (p + 1) * P2) for p in pairs]

    def bd(x):
        return jnp.where(same_head, jnp.concatenate([x, x], axis=0), 0.0)

    kkraw = [kkraw_all[:, sl[p]] for p in pairs]
    sumsq = _split_dots([kkraw[p] * kkraw[p] for p in pairs], head_ones)
    kk = [kkraw[p] / jnp.maximum(jnp.sqrt(sumsq[p]), 1e-12) for p in pairs]
    lw = [lw_all[:, sl[p]] for p in pairs]
    cum = []
    for p in pairs:
        hi = lw[p].astype(bf16)
        r1 = lw[p] - hi.astype(f32)
        mid = r1.astype(bf16)
        lo = (r1 - mid.astype(f32)).astype(bf16)
        cum.append(jnp.dot(tri, hi, preferred_element_type=f32) + jnp.dot(tri, mid, preferred_element_type=f32)
                   + jnp.dot(tri, lo, preferred_element_type=f32))
    tot = [cum[p][C - 1:C, :] for p in pairs]
    w_incl = [jnp.exp(cum[p]) for p in pairs]
    w_excl = [jnp.exp(cum[p] - lw[p]) for p in pairs]
    w_inv = [jnp.exp(-cum[p]) for p in pairs]
    w_rest = [jnp.exp(tot[p] - cum[p]) for p in pairs]
    r_p = [r_all[:, sl[p]] for p in pairs]
    k_p = [k_all[:, sl[p]] for p in pairs]
    v_p = [v_all[:, sl[p]] for p in pairs]
    a_p = [a_all[:, sl[p]] for p in pairs]
    at = [bd(-kk[p] * w_excl[p]) for p in pairs]
    rt = [bd(r_p[p] * w_incl[p]) for p in pairs]
    bt = [bd(kk[p] * a_p[p] * w_inv[p]) for p in pairs]
    kt = [bd(k_p[p] * w_inv[p]) for p in pairs]
    bh = [bd(kk[p] * a_p[p] * w_rest[p]) for p in pairs]
    kh = [bd(k_p[p] * w_rest[p]) for p in pairs]
    vb = [bd(v_p[p]) for p in pairs]

    scores = [lax.dot_general(jnp.concatenate([at[p], rt[p]], axis=0).astype(bf16),
                              jnp.concatenate([bt[p], kt[p]], axis=0).astype(bf16),
                              nt, preferred_element_type=f32) for p in pairs]
    a_ab = [jnp.where(strict_lower, scores[p][:P2, :P2], 0.0) for p in pairs]
    a_ak = [jnp.where(strict_lower, scores[p][:P2, P2:], 0.0) for p in pairs]
    a_rb = [jnp.where(incl_lower, scores[p][P2:, :P2], 0.0) for p in pairs]
    a_rk = [jnp.where(incl_lower, scores[p][P2:, P2:], 0.0) for p in pairs]
    x0 = [jnp.concatenate([at[p], _dotb(a_ak[p], vb[p])], axis=1) for p in pairs]
    eye_f = jnp.where(eye, 1.0, 0.0)
    inv = [eye_f + a_ab[p] for p in pairs]
    pw = [_dotb(a_ab[p], a_ab[p]) for p in pairs]
    for _ in range(int(np.log2(C)) - 2):
        both = [_dotb(pw[p], jnp.concatenate([pw[p], inv[p]], axis=1)) for p in pairs]
        inv = [inv[p] + both[p][:, P2:] for p in pairs]
        pw = [both[p][:, :P2] for p in pairs]
    inv = [inv[p] + _dotb(pw[p], inv[p]) for p in pairs]
    x = [_dotb(inv[p], x0[p]) for p in pairs]
    rhs = [jnp.concatenate([x[p], jnp.concatenate([zeros_pp, vb[p]], axis=1)], axis=0) for p in pairs]
    gz = [_dotb(jnp.concatenate([a_rb[p], a_rk[p]], axis=1), rhs[p]) for p in pairs]
    mn = [_dotb(jnp.concatenate([bh[p], kh[p]], axis=0).T, rhs[p]) for p in pairs]
    yh = []
    for p in pairs:
        g = rt[p] + gz[p][:, :P2]
        m = mn[p][:, :P2] + jnp.where(eye, jnp.broadcast_to(jnp.exp(tot[p]), (P2, P2)), 0.0)
        yh.append(_dotb(jnp.concatenate([g, m], axis=0), h_ref[p]))
    ys = []
    for p in pairs:
        y2 = yh[p][:P2] + gz[p][:, P2:]
        h_ref[p] = yh[p][P2:] + mn[p][:, P2:]
        ys.append(y2[:C] + y2[C:])
    mean = _split_dots(ys, head_ones)
    yc = [ys[p] - mean[p] * (1.0 / RWKV_HEAD_DIM) for p in pairs]
    var = _split_dots([yc[p] * yc[p] for p in pairs], head_ones)
    rk = _split_dots([r_p[p] * k_p[p] * r_k[:, sl[p]] for p in pairs], head_ones)
    for p in pairs:
        yn = yc[p] * lax.rsqrt(var[p] * (1.0 / RWKV_HEAD_DIM) + RWKV_GN_EPS) * lnx_g[:, sl[p]] + lnx_b[:, sl[p]]
        o_ref[:, sl[p]] = ((yn + rk[p] * v_p[p]) * gate_all[:, sl[p]]).astype(o_ref.dtype)


def rwkv_core(p, v_first, vres, mu, w0, w2, a0, a2, g2, k_k, k_a, r_k, lnx_g, lnx_b, B, S):
    C = WKV_CHUNK
    n = S // C
    has_vres = vres is not None
    r_mu, k_mu, v_mu, wd_mu, ad_mu, gd_mu = jnp.split(mu, RWKV_SPLITS)
    mu3 = jnp.stack([r_mu, k_mu, v_mu])
    mu_wa = jnp.concatenate([wd_mu, ad_mu]).reshape(1, RWKV_WA_PAD)
    mu_g = jnp.pad(gd_mu, (0, RWKV_G_PAD - RWKV_LORA_G)).reshape(1, RWKV_G_PAD)
    w2p = jnp.pad(w2, ((0, RWKV_WA_PAD - RWKV_LORA_W), (0, 0))).astype(bf16)
    a2p = jnp.pad(a2, ((RWKV_LORA_W, 0), (0, 0))).astype(bf16)
    g2p = jnp.pad(g2, ((0, RWKV_G_PAD - RWKV_LORA_G), (0, 0))).astype(bf16)
    vecs = [w0, a0, k_k, k_a, r_k.reshape(MIX_B), lnx_g, lnx_b]
    if has_vres:
        vecs.append(vres[0])
    vec = jnp.stack(vecs)
    nv = vec.shape[0]
    row = lambda bi, ci: bi * n + ci
    const = lambda bi, ci: (0, 0)
    tile = lambda w, j: pl.BlockSpec((C, w), lambda bi, ci: (row(bi, ci), j))
    in_specs = [tile(MIX_B, EV_R0 // MIX_B), tile(MIX_B, EV_R0 // MIX_B + 1), tile(MIX_B, EV_R0 // MIX_B + 2),
                tile(RWKV_WA_PAD, EV_WA0 // RWKV_WA_PAD),
                tile(RWKV_G_PAD // 2, EV_G0 // (RWKV_G_PAD // 2)), tile(RWKV_G_PAD // 2, EV_G0 // (RWKV_G_PAD // 2) + 1)]
    args = [p, p, p, p, p, p]
    if has_vres:
        in_specs.append(tile(MIX_B, 0))
        args.append(v_first)
    in_specs += [pl.BlockSpec((3, MIX_B), const), pl.BlockSpec((1, RWKV_WA_PAD), const),
                 pl.BlockSpec((1, RWKV_G_PAD), const), pl.BlockSpec((RWKV_WA_PAD, MIX_B), const),
                 pl.BlockSpec((RWKV_WA_PAD, MIX_B), const), pl.BlockSpec((RWKV_G_PAD, MIX_B), const),
                 pl.BlockSpec((nv, MIX_B), const)]
    args += [mu3, mu_wa, mu_g, w2p, a2p, g2p, vec]
    out_specs = [tile(MIX_B, 0)]
    out_shape = [jax.ShapeDtypeStruct((B * S, MIX_B), bf16)]
    if has_vres:
        v1p = jnp.pad(vres[1], ((0, 0), (0, RWKV_V_PAD - vres[1].shape[1]))).astype(bf16)
        v2p = jnp.pad(vres[2], ((0, RWKV_V_PAD - vres[2].shape[0]), (0, 0))).astype(bf16)
        in_specs += [pl.BlockSpec((MIX_B, RWKV_V_PAD), const), pl.BlockSpec((RWKV_V_PAD, MIX_B), const)]
        args += [v1p, v2p]
    else:
        out_specs.append(tile(MIX_B, 0))
        out_shape.append(jax.ShapeDtypeStruct((B * S, MIX_B), f32))
    outs = pl.pallas_call(
        functools.partial(_rwkv_kernel, has_vres),
        grid=(B, n),
        in_specs=in_specs,
        out_specs=out_specs,
        out_shape=out_shape,
        scratch_shapes=[pltpu.VMEM((WKV_PAIRS, WKV_PAIR, WKV_PAIR), f32),
                        pltpu.VMEM((3, 8, MIX_B), f32), pltpu.VMEM((1, 8, RWKV_WA_PAD), f32),
                        pltpu.VMEM((1, 8, RWKV_G_PAD), f32)],
        compiler_params=pltpu.CompilerParams(
            dimension_semantics=("parallel", "arbitrary"), vmem_limit_bytes=VMEM_LIMIT_BYTES),
        name="rwkv7_time_mix",
    )(*args)
    return (outs[0], v_first) if has_vres else (outs[0], outs[1])


RET_LOG_GAMMA = tuple(float(np.log1p(-np.exp2(-5.0 - h))) for h in range(RET_HEADS))
ROPE_HALF = RET_HEAD_DIM // 2


def _ret_kernel(q_ref, k_ref, v_ref, g_ref, pos_ref, inv_ref, gng_ref, gnb_ref, o_ref, s_ref):
    C = RET_CHUNK
    d = RET_HEAD_DIM

    @pl.when(pl.program_id(1) == 0)
    def _():
        s_ref[...] = jnp.zeros_like(s_ref)

    ang = pos_ref[...] * inv_ref[...]
    cos = jnp.cos(ang)
    sin = jnp.sin(ang)

    def rot(x):
        x1, x2 = x[:, :ROPE_HALF], x[:, ROPE_HALF:]
        return jnp.concatenate([x1 * cos - x2 * sin, x1 * sin + x2 * cos], axis=-1)

    ri = lax.broadcasted_iota(jnp.int32, (C, C), 0)
    ci = lax.broadcasted_iota(jnp.int32, (C, C), 1)
    diff = (ri - ci).astype(f32)
    row = lax.broadcasted_iota(jnp.int32, (C, d), 0).astype(f32)
    nt = (((1,), (1,)), ((), ()))

    for h in range(RET_HEADS):
        hs = slice(h * d, (h + 1) * d)
        lg = RET_LOG_GAMMA[h]
        q = rot(q_ref[:, hs])
        k = rot(k_ref[:, hs]) * (d ** -0.5)
        v = v_ref[:, hs].astype(bf16)
        decay = jnp.where(diff >= 0, jnp.exp(lg * jnp.maximum(diff, 0.0)), 0.0)
        scores = lax.dot_general(q.astype(bf16), k.astype(bf16), nt, preferred_element_type=f32) * decay
        state = s_ref[h]
        y = jnp.dot(scores.astype(bf16), v, preferred_element_type=f32)
        y = y + jnp.dot((q * jnp.exp(lg * (row + 1.0))).astype(bf16), state.astype(bf16), preferred_element_type=f32)
        kw = (k * jnp.exp(lg * (C - 1.0 - row))).astype(bf16)
        s_ref[h] = state * float(np.exp(lg * C)) + jnp.dot(kw.T, v, preferred_element_type=f32)
        mu = jnp.mean(y, axis=-1, keepdims=True)
        yc = y - mu
        var = jnp.mean(yc * yc, axis=-1, keepdims=True)
        yn = yc * lax.rsqrt(var + LN_EPS) * gng_ref[:, hs] + gnb_ref[:, hs]
        g = g_ref[:, hs]
        o_ref[:, hs] = (g * jax.nn.sigmoid(g) * yn).astype(o_ref.dtype)


def retention_core(p, positions, gn_g, gn_b, B, S):
    C = RET_CHUNK
    n = S // C
    inv = (ROPE_BASE ** (-jnp.arange(ROPE_HALF, dtype=f32) / ROPE_HALF)).reshape(1, ROPE_HALF)
    pos = positions.astype(f32).reshape(B * S, 1)
    row = lambda bi, ci: bi * n + ci
    const = lambda bi, ci: (0, 0)
    col = lambda j: pl.BlockSpec((C, MIX_A), lambda bi, ci: (row(bi, ci), j))
    return pl.pallas_call(
        _ret_kernel,
        grid=(B, n),
        in_specs=[col(0), col(1), col(2), col(3),
                  pl.BlockSpec((C, 1), lambda bi, ci: (row(bi, ci), 0)),
                  pl.BlockSpec((1, ROPE_HALF), const),
                  pl.BlockSpec((1, MIX_A), const),
                  pl.BlockSpec((1, MIX_A), const)],
        out_specs=pl.BlockSpec((C, MIX_A), lambda bi, ci: (row(bi, ci), 0)),
        out_shape=jax.ShapeDtypeStruct((B * S, MIX_A), bf16),
        scratch_shapes=[pltpu.VMEM((RET_HEADS, RET_HEAD_DIM, RET_HEAD_DIM), f32)],
        compiler_params=pltpu.CompilerParams(
            dimension_semantics=("parallel", "arbitrary"), vmem_limit_bytes=VMEM_LIMIT_BYTES),
        name="retention_chunked",
    )(p, p, p, p, pos, inv, gn_g.reshape(1, MIX_A), gn_b.reshape(1, MIX_A))


GLA_LEVELS = int(np.log2(GLA_CHUNK))
GLA_DECAY_ROWS = (2 + GLA_LEVELS) * GLA_CHUNK
GLA_AD_PAD = 128
ODD_PROJ_PAD = 2 * GLA_DK + 2 * GLA_DV + GLA_AD_PAD
ODD_TN = ODD_PROJ_PAD // 7


def _gla_decay_operator():
    C = GLA_CHUNK
    i = np.arange(C)[:, None]
    j = np.arange(C)[None, :]
    blocks = [(j <= i), (j > i)]
    for lvl in range(GLA_LEVELS):
        s = C >> (lvl + 1)
        mid = (i // (2 * s)) * (2 * s) + s
        right = i >= mid
        blocks.append(np.where(right, (j >= mid) & (j <= i), (j > i) & (j < mid)))
    return np.concatenate(blocks, 0).astype(np.float32)


def _gla_kernel(q_ref, k_ref, v_ref, r_ref, ad_ref, a2_ref, ab_ref, g_ref, lop_ref, o_ref, s_ref):
    C = GLA_CHUNK

    @pl.when(pl.program_id(1) == 0)
    def _():
        s_ref[...] = jnp.zeros_like(s_ref)

    x = jnp.dot(ad_ref[...].astype(bf16), a2_ref[...], preferred_element_type=f32) + ab_ref[...]
    la = (jnp.minimum(x, 0.0) - jnp.log1p(jnp.exp(-jnp.abs(x)))) * (1.0 / GLA_GATE_NORM)
    hi = la.astype(bf16)
    r1 = la - hi.astype(f32)
    mid = r1.astype(bf16)
    lo = (r1 - mid.astype(f32)).astype(bf16)
    lop = lop_ref[...]
    dec = (jnp.dot(lop, hi, preferred_element_type=f32) + jnp.dot(lop, mid, preferred_element_type=f32)
           + jnp.dot(lop, lo, preferred_element_type=f32))

    ri = lax.broadcasted_iota(jnp.int32, (C, C), 0)
    ci = lax.broadcasted_iota(jnp.int32, (C, C), 1)
    level_masks = []
    for lvl in range(GLA_LEVELS):
        s = C >> (lvl + 1)
        same_block = (ri // (2 * s)) == (ci // (2 * s))
        level_masks.append(same_block & ((ri % (2 * s)) >= s) & ((ci % (2 * s)) < s))
    eye = ri == ci
    nt = (((1,), (1,)), ((), ()))

    for h in range(GLA_HEADS):
        ks = slice(h * GLA_HEAD_DK, (h + 1) * GLA_HEAD_DK)
        vs = slice(h * GLA_HEAD_DV, (h + 1) * GLA_HEAD_DV)
        q = q_ref[:, ks] * (GLA_HEAD_DK ** -0.5)
        k = k_ref[:, ks]
        v = v_ref[:, vs].astype(bf16)
        e = jnp.exp(dec[:, ks])
        state = s_ref[h]
        o = lax.dot_general((q * e[:C]).astype(bf16), state.astype(bf16), nt, preferred_element_type=f32)
        attn = jnp.where(eye, lax.dot_general(q.astype(bf16), k.astype(bf16), nt, preferred_element_type=f32), 0.0)
        for lvl in range(GLA_LEVELS):
            el = e[(2 + lvl) * C:(3 + lvl) * C]
            sc = lax.dot_general((q * el).astype(bf16), (k * el).astype(bf16), nt, preferred_element_type=f32)
            attn = attn + jnp.where(level_masks[lvl], sc, 0.0)
        o = o + jnp.dot(attn.astype(bf16), v, preferred_element_type=f32)
        kd = (k * e[C:2 * C]).astype(bf16)
        s_ref[h] = state * e[C - 1:C] + jnp.dot(v.T, kd, preferred_element_type=f32)
        o = o * lax.rsqrt(jnp.mean(o * o, axis=-1, keepdims=True) + LN_EPS) * g_ref[:, vs]
        rr = r_ref[:, vs]
        o_ref[:, vs] = (rr * jax.nn.sigmoid(rr) * o).astype(o_ref.dtype)


def gla_core(p, a2, ab, norm_g, B, S):
    C = GLA_CHUNK
    n = S // C
    a2p = jnp.pad(a2, ((0, GLA_AD_PAD - GLA_LORA), (0, 0))).astype(bf16)
    lop = jnp.asarray(_gla_decay_operator(), dtype=bf16)
    row = lambda bi, ci: bi * n + ci
    const = lambda bi, ci: (0, 0)
    return pl.pallas_call(
        _gla_kernel,
        grid=(B, n),
        in_specs=[pl.BlockSpec((C, GLA_DK), lambda bi, ci: (row(bi, ci), 0)),
                  pl.BlockSpec((C, GLA_DK), lambda bi, ci: (row(bi, ci), 1)),
                  pl.BlockSpec((C, GLA_DV), lambda bi, ci: (row(bi, ci), 1)),
                  pl.BlockSpec((C, GLA_DV), lambda bi, ci: (row(bi, ci), 2)),
                  pl.BlockSpec((C, GLA_AD_PAD), lambda bi, ci: (row(bi, ci), (2 * GLA_DK + 2 * GLA_DV) // GLA_AD_PAD)),
                  pl.BlockSpec((GLA_AD_PAD, GLA_DK), const),
                  pl.BlockSpec((1, GLA_DK), const),
                  pl.BlockSpec((1, GLA_DV), const),
                  pl.BlockSpec((GLA_DECAY_ROWS, C), const)],
        out_specs=pl.BlockSpec((C, GLA_DV), lambda bi, ci: (row(bi, ci), 0)),
        out_shape=jax.ShapeDtypeStruct((B * S, GLA_DV), bf16),
        scratch_shapes=[pltpu.VMEM((GLA_HEADS, GLA_HEAD_DV, GLA_HEAD_DK), f32)],
        compiler_params=pltpu.CompilerParams(
            dimension_semantics=("parallel", "arbitrary"), vmem_limit_bytes=VMEM_LIMIT_BYTES),
        name="gla_chunked",
    )(p, p, p, p, p, a2p, ab.reshape(1, GLA_DK), norm_g.reshape(1, GLA_DV), lop)


PROJ_TM = 256
ROUTER_PAD = 128


def _proj_norm_kernel(n_in, *refs):
    a_refs = refs[:n_in]
    w_ref, x_ref, mod_ref, g_ref, b_ref, wr_ref, br_ref, xo_ref, ho_ref, lo_ref = refs[n_in:]
    acc = None
    k0 = 0
    for a_ref in a_refs:
        kw = a_ref.shape[1]
        part = jnp.dot(a_ref[...], w_ref[k0:k0 + kw, :], preferred_element_type=f32)
        acc = part if acc is None else acc + part
        k0 += kw
    gt, sc, sh = mod_ref[0, 0:1, :], mod_ref[0, 1:2, :], mod_ref[0, 2:3, :]
    z = DEEPNORM_ALPHA * x_ref[...] + (1.0 + gt) * acc
    mu = jnp.mean(z, axis=-1, keepdims=True)
    zc = z - mu
    var = jnp.mean(zc * zc, axis=-1, keepdims=True)
    xn = zc * lax.rsqrt(var + LN_EPS) * g_ref[...] + b_ref[...]
    xo_ref[...] = xn
    hf = xn * (1.0 + sc) + sh
    ho_ref[...] = hf
    lo_ref[...] = jnp.dot(hf.astype(bf16), wr_ref[...], preferred_element_type=f32) + br_ref[...]


def proj_norm(acts, wo, x, mod, ln_g, ln_b, w_router, b_router, S):
    T, D = x.shape
    tiles_per_batch = S // PROJ_TM
    const = lambda i: (0, 0)
    rowblk = lambda w: pl.BlockSpec((PROJ_TM, w), lambda i: (i, 0))
    in_specs = [rowblk(a.shape[1]) for a in acts]
    in_specs += [pl.BlockSpec(wo.shape, const), rowblk(D),
                 pl.BlockSpec((1, 3, D), lambda i: (i // tiles_per_batch, 0, 0)),
                 pl.BlockSpec((1, D), const), pl.BlockSpec((1, D), const),
                 pl.BlockSpec((D, ROUTER_PAD), const), pl.BlockSpec((1, ROUTER_PAD), const)]
    return pl.pallas_call(
        functools.partial(_proj_norm_kernel, len(acts)),
        grid=(T // PROJ_TM,),
        in_specs=in_specs,
        out_specs=[rowblk(D), rowblk(D), rowblk(ROUTER_PAD)],
        out_shape=[jax.ShapeDtypeStruct((T, D), f32), jax.ShapeDtypeStruct((T, D), f32),
                   jax.ShapeDtypeStruct((T, ROUTER_PAD), f32)],
        compiler_params=pltpu.CompilerParams(
            dimension_semantics=("parallel",), vmem_limit_bytes=VMEM_LIMIT_BYTES),
        name="proj_norm",
    )(*acts, wo.astype(bf16), x, mod, ln_g.reshape(1, D), ln_b.reshape(1, D), w_router, b_router)


MOE_ROWS = 256
MOE_VMEM_LIMIT_BYTES = 56 * 1024 * 1024
COMBINE_ROWS = 256
SUBLANES = 8
GATHER_DMA_PRIORITY = 1


def _moe_ffn_kernel(layer, tok_ref, be_ref, nused_ref, x_hbm, w1_hbm, w3_hbm, w2_hbm, y_ref,
                    xbuf, w1buf, w3buf, w2buf, sem, wsem, run_ref):
    i = pl.program_id(0)
    nblk = pl.num_programs(0)
    n_used = nused_ref[0]
    slot = lax.rem(i, 2)
    expert = be_ref[i]
    first = (i < n_used) & ((i == 0) | (expert != be_ref[jnp.maximum(i - 1, 0)]))

    n_groups = MOE_ROWS // SUBLANES

    def more_groups(blk):
        return lambda g: (g < n_groups) & (tok_ref[blk * MOE_ROWS + jnp.minimum(g, n_groups - 1) * SUBLANES] >= 0)

    def start_gather(blk, s):
        def body(g):
            for u in range(SUBLANES):
                t = jnp.maximum(tok_ref[blk * MOE_ROWS + g * SUBLANES + u], 0)
                pltpu.make_async_copy(x_hbm.at[lax.shift_right_logical(t, 3), pl.ds(jnp.bitwise_and(t, SUBLANES - 1), 1)],
                                      xbuf.at[s, g, pl.ds(u, 1)], sem.at[s]).start(priority=GATHER_DMA_PRIORITY)
            return g + 1
        lax.while_loop(more_groups(blk), body, 0)

    def wait_gather(blk, s):
        def body(g):
            pltpu.make_async_copy(x_hbm.at[pl.ds(0, 1)], xbuf.at[s, pl.ds(g, 1)], sem.at[s]).wait()
            return g + 1
        lax.while_loop(more_groups(blk), body, 0)

    def weight_copies(e, b):
        return [pltpu.make_async_copy(w_hbm.at[layer, e], wbuf.at[b], wsem.at[b])
                for w_hbm, wbuf in ((w1_hbm, w1buf), (w3_hbm, w3buf), (w2_hbm, w2buf))]

    @pl.when((i == 0) & (n_used > 0))
    def _():
        run_ref[0] = 0
        xbuf[...] = jnp.zeros_like(xbuf)
        for cp in weight_copies(expert, 0):
            cp.start()
        start_gather(0, 0)

    @pl.when(i + 1 < n_used)
    def _():
        start_gather(i + 1, 1 - slot)

    @pl.when(first)
    def _():
        @pl.when(i > 0)
        def _():
            run_ref[0] = run_ref[0] + 1
        buf = jnp.bitwise_and(run_ref[0], 1)
        nxt = lax.while_loop(lambda j: (j < n_used) & (be_ref[jnp.minimum(j, nblk - 1)] == expert),
                             lambda j: j + 1, i + 1)

        @pl.when(nxt < n_used)
        def _():
            for cp in weight_copies(be_ref[jnp.minimum(nxt, nblk - 1)], 1 - buf):
                cp.start()
        for cp in weight_copies(expert, buf):
            cp.wait()

    @pl.when(i < n_used)
    def _():
        buf = jnp.bitwise_and(run_ref[0], 1)
        wait_gather(i, slot)
        xb = xbuf[slot].reshape(MOE_ROWS, xbuf.shape[-1]).astype(bf16)
        a = jnp.dot(xb, w1buf[buf].astype(bf16), preferred_element_type=f32)
        b = jnp.dot(xb, w3buf[buf].astype(bf16), preferred_element_type=f32)
        hid = (a * jax.nn.sigmoid(a) * b).astype(bf16)
        y_ref[...] = jnp.dot(hid, w2buf[buf].astype(bf16), preferred_element_type=f32)

    @pl.when(i >= n_used)
    def _():
        y_ref[...] = jnp.zeros_like(y_ref)


def moe_ffn(x, buf_tok, blk_expert, n_used, w1, w3, w2, layer):
    D = x.shape[1]
    nblk = blk_expert.shape[0]
    any_spec = pl.BlockSpec(memory_space=pl.ANY)
    return pl.pallas_call(
        functools.partial(_moe_ffn_kernel, layer),
        grid_spec=pltpu.PrefetchScalarGridSpec(
            num_scalar_prefetch=3,
            grid=(nblk,),
            in_specs=[any_spec, any_spec, any_spec, any_spec],
            out_specs=pl.BlockSpec((MOE_ROWS, D), lambda i, tok, be, nu: (i, 0)),
            scratch_shapes=[pltpu.VMEM((2, MOE_ROWS // SUBLANES, SUBLANES, D), f32),
                            pltpu.VMEM((2, D, EXPERT_FF), f32), pltpu.VMEM((2, D, EXPERT_FF), f32),
                            pltpu.VMEM((2, EXPERT_FF, D), f32),
                            pltpu.SemaphoreType.DMA((2,)), pltpu.SemaphoreType.DMA((2,)),
                            pltpu.SMEM((1,), jnp.int32)]),
        out_shape=jax.ShapeDtypeStruct((nblk * MOE_ROWS, D), f32),
        compiler_params=pltpu.CompilerParams(
            dimension_semantics=("arbitrary",), vmem_limit_bytes=MOE_VMEM_LIMIT_BYTES),
        name="moe_expert_ffn",
    )(buf_tok, blk_expert, n_used, x.reshape(x.shape[0] // SUBLANES, SUBLANES, D), w1, w3, w2)


def _moe_combine_kernel(emit_h, s0_ref, s1_ref, y_hbm, gate_ref, x_ref, mod_ref, g_ref, b_ref, *rest):
    if emit_h:
        o_ref, h_ref, buf, sem = rest
    else:
        o_ref, buf, sem = rest
    i = pl.program_id(0)
    n = pl.num_programs(0)
    slot = lax.rem(i, 2)

    def start_gather(blk, s):
        def body(g, carry):
            base = pl.multiple_of(g * SUBLANES, SUBLANES)
            for u in range(SUBLANES):
                t = blk * COMBINE_ROWS + base + u
                pltpu.make_async_copy(y_hbm.at[pl.ds(s0_ref[t], 1)], buf.at[s, 0, pl.ds(base + u, 1)], sem.at[s]).start()
                pltpu.make_async_copy(y_hbm.at[pl.ds(s1_ref[t], 1)], buf.at[s, 1, pl.ds(base + u, 1)], sem.at[s]).start()
            return carry
        lax.fori_loop(0, COMBINE_ROWS // SUBLANES, body, 0)

    @pl.when(i == 0)
    def _():
        start_gather(0, 0)

    @pl.when(i + 1 < n)
    def _():
        start_gather(i + 1, 1 - slot)

    for c in range(MOE_TOPK):
        pltpu.make_async_copy(y_hbm.at[pl.ds(0, COMBINE_ROWS)], buf.at[slot, c], sem.at[slot]).wait()
    y = buf[slot, 0] * gate_ref[:, 0:1] + buf[slot, 1] * gate_ref[:, 1:2]
    z = DEEPNORM_ALPHA * x_ref[...] + (1.0 + mod_ref[0, 0:1, :]) * y
    mu = jnp.mean(z, axis=-1, keepdims=True)
    zc = z - mu
    var = jnp.mean(zc * zc, axis=-1, keepdims=True)
    xn = zc * lax.rsqrt(var + LN_EPS) * g_ref[...] + b_ref[...]
    o_ref[...] = xn
    if emit_h:
        h_ref[...] = (xn * (1.0 + mod_ref[0, 1:2, :]) + mod_ref[0, 2:3, :]).astype(h_ref.dtype)


def moe_combine_norm(y_rows, slot0, slot1, gate, x, mod, ln_g, ln_b, S, emit_h):
    T, D = x.shape
    tiles_per_batch = S // COMBINE_ROWS
    const = lambda i, s0, s1: (0, 0)
    rowblk = pl.BlockSpec((COMBINE_ROWS, D), lambda i, s0, s1: (i, 0))
    out_specs = [rowblk, rowblk] if emit_h else [rowblk]
    out_shape = [jax.ShapeDtypeStruct((T, D), f32)] + ([jax.ShapeDtypeStruct((T, D), bf16)] if emit_h else [])
    return pl.pallas_call(
        functools.partial(_moe_combine_kernel, emit_h),
        grid_spec=pltpu.PrefetchScalarGridSpec(
            num_scalar_prefetch=2,
            grid=(T // COMBINE_ROWS,),
            in_specs=[pl.BlockSpec(memory_space=pl.ANY),
                      pl.BlockSpec((COMBINE_ROWS, MOE_TOPK), lambda i, s0, s1: (i, 0)), rowblk,
                      pl.BlockSpec((1, 3, D), lambda i, s0, s1: (i // tiles_per_batch, 0, 0)),
                      pl.BlockSpec((1, D), const), pl.BlockSpec((1, D), const)],
            out_specs=out_specs,
            scratch_shapes=[pltpu.VMEM((2, MOE_TOPK, COMBINE_ROWS, D), f32), pltpu.SemaphoreType.DMA((2,))]),
        out_shape=out_shape,
        compiler_params=pltpu.CompilerParams(
            dimension_semantics=("arbitrary",), vmem_limit_bytes=VMEM_LIMIT_BYTES),
        name="moe_combine_norm",
    )(slot0, slot1, y_rows, gate, x, mod, ln_g.reshape(1, D), ln_b.reshape(1, D))


def router_params(wg, bg, we, be):
    D = wg.shape[0]
    pad = ROUTER_PAD - MOE_GROUPS - N_EXPERTS
    w = jnp.concatenate([wg, we, jnp.zeros((D, pad), wg.dtype)], 1).astype(bf16)
    b = jnp.concatenate([bg, be, jnp.zeros((pad,), bg.dtype)]).reshape(1, ROUTER_PAD)
    return w, b


def moe_route(logits):
    g_prob = jax.nn.softmax(logits[:, :MOE_GROUPS], -1)
    g_top = jnp.argmax(g_prob, -1).astype(jnp.int32)
    p_top = jnp.max(g_prob, -1)
    ids = jnp.arange(N_EXPERTS, dtype=jnp.int32)[None, :]
    e_sel = jnp.where(ids // MOE_PER_GROUP == g_top[:, None], logits[:, MOE_GROUPS:MOE_GROUPS + N_EXPERTS], -jnp.inf)
    i1 = jnp.argmax(e_sel, -1).astype(jnp.int32)
    v1 = jnp.max(e_sel, -1)
    rest = jnp.where(ids == i1[:, None], -jnp.inf, e_sel)
    i2 = jnp.argmax(rest, -1).astype(jnp.int32)
    v2 = jnp.max(rest, -1)
    gate = p_top[:, None] * jax.nn.softmax(jnp.stack([v1, v2], -1), -1)
    return gate, jnp.stack([i1, i2], -1)


def moe_layout(expert):
    T = expert.shape[0]
    A = T * MOE_TOPK
    e_flat = expert.reshape(A)
    onehot = (e_flat[:, None] == jnp.arange(N_EXPERTS, dtype=jnp.int32)[None, :]).astype(jnp.int32)
    rank = jnp.cumsum(onehot, axis=0)
    counts = rank[-1]
    padded = (counts + MOE_ROWS - 1) // MOE_ROWS * MOE_ROWS
    pad_end = jnp.cumsum(padded)
    pad_start = pad_end - padded
    slot = jnp.sum(onehot * (pad_start[None, :] + rank - 1), axis=1).astype(jnp.int32)
    tok_flat = jnp.arange(A, dtype=jnp.int32) // MOE_TOPK
    nblk = A // MOE_ROWS + N_EXPERTS
    buf_tok = jnp.full((nblk * MOE_ROWS,), -1, jnp.int32).at[slot].set(tok_flat)
    n_used = (pad_end[-1] // MOE_ROWS).astype(jnp.int32)
    blk = jnp.minimum(jnp.arange(nblk, dtype=jnp.int32), n_used - 1) * MOE_ROWS
    blk_expert = jnp.minimum(jnp.sum(blk[:, None] >= pad_end[None, :], axis=1), N_EXPERTS - 1).astype(jnp.int32)
    slot2 = slot.reshape(T, MOE_TOPK)
    return buf_tok, blk_expert, n_used.reshape(1), slot2[:, 0], slot2[:, 1]


def kernel(x, c, positions, ada_w, ada_b, ln_g, ln_b, ev_win, ev_wo, ret_gn_g, ret_gn_b, rw_mu, rw_w0, rw_w2, rw_a0, rw_a2, rw_g2, rw_kk, rw_ka, rw_rk, rw_lnx_g, rw_lnx_b, rw_v0, rw_v1, rw_v2, od_win, od_wo, gla_a2, gla_ab, gla_norm_g, moe_wg, moe_bg, moe_we, moe_be, moe_w1, moe_w3, moe_w2):
    B, S, D = x.shape
    T = B * S
    depth = ada_w.shape[0]
    mods = ada_mod(c, ada_w, ada_b).reshape(depth, B, 6, D)
    h = (x * (1.0 + mods[0, :, 1])[:, None, :] + mods[0, :, 0][:, None, :]).astype(bf16).reshape(T, D)
    x = x.reshape(T, D)
    ev_w = padded_bf16(ev_win, EVEN_COLS)
    od_w = padded_bf16(od_win, ODD_PROJ_PAD)
    v_first = None
    for layer in range(depth):
        j = layer // 2
        mixer_mod = jnp.stack([mods[layer, :, 2], mods[layer, :, 4], mods[layer, :, 3]], axis=1)
        if layer % 2 == 0:
            p = matmul(h, ev_w, j, tn=EVEN_TN)
            out_a = retention_core(p, positions, ret_gn_g[j], ret_gn_b[j], B, S)
            vres = None if j == 0 else (rw_v0[j - 1], rw_v1[j - 1], rw_v2[j - 1])
            out_b, v_first = rwkv_core(p, v_first, vres, rw_mu[j], rw_w0[j], rw_w2[j], rw_a0[j], rw_a2[j], rw_g2[j],
                                       rw_kk[j], rw_ka[j], rw_rk[j], rw_lnx_g[j], rw_lnx_b[j], B, S)
            acts, wo = [out_a, out_b], ev_wo[j]
        else:
            p = matmul(h, od_w, j, tn=ODD_TN)
            acts, wo = [gla_core(p, gla_a2[j], gla_ab[j], gla_norm_g[j], B, S)], od_wo[j]
        w_router, b_router = router_params(moe_wg[layer], moe_bg[layer], moe_we[layer], moe_be[layer])
        x, hf, logits = proj_norm(acts, wo, x, mixer_mod, ln_g[layer, 0], ln_b[layer, 0], w_router, b_router, S)
        gate, expert = moe_route(logits)
        buf_tok, blk_expert, n_used, slot0, slot1 = moe_layout(expert)
        y_rows = moe_ffn(hf, buf_tok, blk_expert, n_used, moe_w1, moe_w3, moe_w2, layer)
        last = layer == depth - 1
        nxt = layer if last else layer + 1
        moe_mod = jnp.stack([mods[layer, :, 5], mods[nxt, :, 1], mods[nxt, :, 0]], axis=1)
        outs = moe_combine_norm(y_rows, slot0, slot1, gate, x, moe_mod, ln_g[layer, 1], ln_b[layer, 1], S, not last)
        x = outs[0]
        if not last:
            h = outs[1]
    return x.reshape(B, S, D)
```

```python
import functools

import jax
import jax.numpy as jnp
import numpy as np
from jax import lax
from jax.experimental import pallas as pl
from jax.experimental.pallas import tpu as pltpu

D_MODEL = 2048
DEPTH = 4
MIX_A = D_MODEL // 2
RET_HEAD_DIM = 256
RET_HEADS = MIX_A // RET_HEAD_DIM
RET_CHUNK = 128
RET_COLS = 4 * MIX_A
ROPE_BASE = 10000.0
MIX_B = D_MODEL - MIX_A
RWKV_HEAD_DIM = 64
RWKV_HEADS = MIX_B // RWKV_HEAD_DIM
RWKV_LORA_W = 64
RWKV_LORA_A = 64
RWKV_LORA_G = 160
RWKV_COLS = 3 * MIX_B + RWKV_LORA_W + RWKV_LORA_A + RWKV_LORA_G
RWKV_SPLITS = (MIX_B, 2 * MIX_B, 3 * MIX_B, 3 * MIX_B + RWKV_LORA_W, 3 * MIX_B + RWKV_LORA_W + RWKV_LORA_A)
RWKV_GN_EPS = 64e-5
GLA_HEADS = 4
GLA_DK = D_MODEL // 2
GLA_DV = D_MODEL
GLA_HEAD_DK = GLA_DK // GLA_HEADS
GLA_HEAD_DV = GLA_DV // GLA_HEADS
GLA_LORA = 16
GLA_GATE_NORM = 16.0
GLA_CHUNK = 64
ODD_PROJ = 2 * GLA_DK + 2 * GLA_DV + GLA_LORA
MOE_GROUPS = 4
MOE_PER_GROUP = 8
N_EXPERTS = MOE_GROUPS * MOE_PER_GROUP
MOE_TOPK = 2
EXPERT_FF = D_MODEL // 4
DEEPNORM_ALPHA = (2 * DEPTH) ** 0.25
LN_EPS = 1e-5

f32 = jnp.float32
bf16 = jnp.bfloat16

VMEM_LIMIT_BYTES = 48 * 1024 * 1024
DENSE_VMEM_LIMIT_BYTES = 56 * 1024 * 1024


def _dotb(a, b):
    return jnp.dot(a.astype(bf16), b.astype(bf16), preferred_element_type=f32)


def _split_dots(xs, ones_bf16):
    n, rows = len(xs), xs[0].shape[0]
    his = [x.astype(bf16) for x in xs]
    los = [(x - h.astype(f32)).astype(bf16) for x, h in zip(xs, his)]
    out = jnp.dot(jnp.concatenate(his + los, axis=0), ones_bf16, preferred_element_type=f32)
    return [out[i * rows:(i + 1) * rows] + out[(n + i) * rows:(n + i + 1) * rows] for i in range(n)]


def _softplus(z):
    return jnp.maximum(z, 0.0) + jnp.log1p(jnp.exp(-jnp.abs(z)))


ADA_ROWS = 8
ADA_TN = 1024


def _ada_kernel(c_ref, w_ref, b_ref, o_ref):
    c = c_ref[...]
    act = (c * jax.nn.sigmoid(c)).astype(bf16)
    o_ref[0] = jnp.dot(act, w_ref[0].astype(bf16), preferred_element_type=f32) + b_ref[0]


def ada_mod(c, ada_w, ada_b):
    B, D = c.shape
    depth, _, N = ada_w.shape
    c_pad = jnp.pad(c, ((0, ADA_ROWS - B), (0, 0)))
    out = pl.pallas_call(
        _ada_kernel,
        grid=(depth, N // ADA_TN),
        in_specs=[pl.BlockSpec((ADA_ROWS, D), lambda l, j: (0, 0)),
                  pl.BlockSpec((1, D, ADA_TN), lambda l, j: (l, 0, j)),
                  pl.BlockSpec((1, 1, ADA_TN), lambda l, j: (l, 0, j))],
        out_specs=pl.BlockSpec((1, ADA_ROWS, ADA_TN), lambda l, j: (l, 0, j)),
        out_shape=jax.ShapeDtypeStruct((depth, ADA_ROWS, N), f32),
        compiler_params=pltpu.CompilerParams(
            dimension_semantics=("parallel", "parallel"), vmem_limit_bytes=VMEM_LIMIT_BYTES),
        name="ada_mod",
    )(c_pad, ada_w, ada_b.reshape(depth, 1, N))
    return out[:, :B]


def _matmul_kernel(x_ref, w_ref, o_ref):
    o_ref[...] = jnp.dot(x_ref[...], w_ref[0], preferred_element_type=f32)


def padded_bf16(w, cols):
    return jnp.pad(w, ((0, 0), (0, 0), (0, cols - w.shape[2]))).astype(bf16)


def matmul(x, w, layer, tm=2048, tn=512):
    M, K = x.shape
    Np = w.shape[2]
    assert Np % tn == 0 and M % min(tm, M) == 0
    tm = min(tm, M)
    return pl.pallas_call(
        _matmul_kernel,
        grid=(M // tm, Np // tn),
        in_specs=[pl.BlockSpec((tm, K), lambda i, j: (i, 0)),
                  pl.BlockSpec((1, K, tn), lambda i, j: (layer, 0, j))],
        out_specs=pl.BlockSpec((tm, tn), lambda i, j: (i, j)),
        out_shape=jax.ShapeDtypeStruct((M, Np), f32),
        compiler_params=pltpu.CompilerParams(
            dimension_semantics=("parallel", "parallel"), vmem_limit_bytes=DENSE_VMEM_LIMIT_BYTES),
        name="dense_matmul",
    )(x, w)


WKV_CHUNK = 64
WKV_PAIR = 2 * RWKV_HEAD_DIM
WKV_PAIRS = MIX_B // WKV_PAIR
RWKV_WA_PAD = 128
RWKV_G_PAD = 256
RWKV_V_PAD = 128
EV_R0 = RET_COLS
EV_WA0 = EV_R0 + 3 * MIX_B
EV_G0 = EV_WA0 + RWKV_WA_PAD
EVEN_TN = 512
EVEN_COLS = -(-(EV_G0 + RWKV_G_PAD) // EVEN_TN) * EVEN_TN


def _rwkv_kernel(has_vres, *refs):
    if has_vres:
        (r_ref, k_ref, v_ref, wa_ref, gd0_ref, gd1_ref, vf_ref, mu_ref, muwa_ref, mug_ref, w2_ref, a2_ref, g2_ref,
         vec_ref, v1_ref, v2_ref, o_ref, h_ref, prev_ref, prevwa_ref, prevg_ref) = refs
    else:
        (r_ref, k_ref, v_ref, wa_ref, gd0_ref, gd1_ref, mu_ref, muwa_ref, mug_ref, w2_ref, a2_ref, g2_ref,
         vec_ref, o_ref, vf_ref, h_ref, prev_ref, prevwa_ref, prevg_ref) = refs
    C = WKV_CHUNK
    P2 = WKV_PAIR

    @pl.when(pl.program_id(1) == 0)
    def _():
        h_ref[...] = jnp.zeros_like(h_ref)
        prev_ref[...] = jnp.zeros_like(prev_ref)
        prevwa_ref[...] = jnp.zeros_like(prevwa_ref)
        prevg_ref[...] = jnp.zeros_like(prevg_ref)

    def shift(x, carry_ref, j, mu):
        first = lax.broadcasted_iota(jnp.int32, x.shape, 0) == 0
        prev = jnp.where(first, jnp.broadcast_to(carry_ref[j, 7:8, :], x.shape), pltpu.roll(x, 1, 0))
        carry_ref[j] = x[C - 8:, :]
        return x + (prev - x) * mu

    r_all = shift(r_ref[...], prev_ref, 0, mu_ref[0:1, :])
    k_all = shift(k_ref[...], prev_ref, 1, mu_ref[1:2, :])
    v_all = shift(v_ref[...], prev_ref, 2, mu_ref[2:3, :])
    wa = shift(wa_ref[...], prevwa_ref, 0, muwa_ref[...])
    gd = shift(jnp.concatenate([gd0_ref[...], gd1_ref[...]], axis=1), prevg_ref, 0, mug_ref[...])
    w0, a0, k_k, k_a, r_k, lnx_g, lnx_b = (vec_ref[i:i + 1, :] for i in range(7))

    w_log = -_softplus(-(w0 + _dotb(jnp.tanh(wa), w2_ref[...]))) - 0.5
    lw_all = -jnp.exp(w_log)
    a_all = jax.nn.sigmoid(a0 + _dotb(wa, a2_ref[...]))
    gate_all = _dotb(jax.nn.sigmoid(gd), g2_ref[...])
    if has_vres:
        v0 = vec_ref[7:8, :]
        mix = jax.nn.sigmoid(v0 + _dotb(_dotb(v_all, v1_ref[...]), v2_ref[...]))
        v_all = v_all + (vf_ref[...] - v_all) * mix
    else:
        vf_ref[...] = v_all
    kkraw_all = k_all * k_k
    k_all = k_all * (1.0 + (a_all - 1.0) * k_a)

    row = lax.broadcasted_iota(jnp.int32, (P2, P2), 0)
    col = lax.broadcasted_iota(jnp.int32, (P2, P2), 1)
    same_head = (row >= C) == (col >= RWKV_HEAD_DIM)
    head_ones = jnp.where((row >= RWKV_HEAD_DIM) == (col >= RWKV_HEAD_DIM), 1.0, 0.0).astype(bf16)
    t_idx = jnp.bitwise_and(row, C - 1)
    s_idx = jnp.bitwise_and(col, C - 1)
    strict_lower = t_idx > s_idx
    incl_lower = t_idx >= s_idx
    eye = row == col
    tri = (lax.broadcasted_iota(jnp.int32, (C, C), 0) >= lax.broadcasted_iota(jnp.int32, (C, C), 1)).astype(bf16)
    zeros_pp = jnp.zeros((P2, P2), f32)
    nt = (((1,), (1,)), ((), ()))
    pairs = range(WKV_PAIRS)
    sl = [slice(p * P2, (p + 1) * P2) for p in pairs]

    def bd(x):
        return jnp.where(same_head, jnp.concatenate([x, x], axis=0), 0.0)

    kkraw = [kkraw_all[:, sl[p]] for p in pairs]
    sumsq = _split_dots([kkraw[p] * kkraw[p] for p in pairs], head_ones)
    kk = [kkraw[p] / jnp.maximum(jnp.sqrt(sumsq[p]), 1e-12) for p in pairs]
    lw = [lw_all[:, sl[p]] for p in pairs]
    cum = []
    for p in pairs:
        hi = lw[p].astype(bf16)
        r1 = lw[p] - hi.astype(f32)
        mid = r1.astype(bf16)
        lo = (r1 - mid.astype(f32)).astype(bf16)
        cum.append(jnp.dot(tri, hi, preferred_element_type=f32) + jnp.dot(tri, mid, preferred_element_type=f32)
                   + jnp.dot(tri, lo, preferred_element_type=f32))
    tot = [cum[p][C - 1:C, :] for p in pairs]
    w_incl = [jnp.exp(cum[p]) for p in pairs]
    w_excl = [jnp.exp(cum[p] - lw[p]) for p in pairs]
    w_inv = [jnp.exp(-cum[p]) for p in pairs]
    w_rest = [jnp.exp(tot[p] - cum[p]) for p in pairs]
    r_p = [r_all[:, sl[p]] for p in pairs]
    k_p = [k_all[:, sl[p]] for p in pairs]
    v_p = [v_all[:, sl[p]] for p in pairs]
    a_p = [a_all[:, sl[p]] for p in pairs]
    at = [bd(-kk[p] * w_excl[p]) for p in pairs]
    rt = [bd(r_p[p] * w_incl[p]) for p in pairs]
    bt = [bd(kk[p] * a_p[p] * w_inv[p]) for p in pairs]
    kt = [bd(k_p[p] * w_inv[p]) for p in pairs]
    bh = [bd(kk[p] * a_p[p] * w_rest[p]) for p in pairs]
    kh = [bd(k_p[p] * w_rest[p]) for p in pairs]
    vb = [bd(v_p[p]) for p in pairs]

    scores = [lax.dot_general(jnp.concatenate([at[p], rt[p]], axis=0).astype(bf16),
                              jnp.concatenate([bt[p], kt[p]], axis=0).astype(bf16),
                              nt, preferred_element_type=f32) for p in pairs]
    a_ab = [jnp.where(strict_lower, scores[p][:P2, :P2], 0.0) for p in pairs]
    a_ak = [jnp.where(strict_lower, scores[p][:P2, P2:], 0.0) for p in pairs]
    a_rb = [jnp.where(incl_lower, scores[p][P2:, :P2], 0.0) for p in pairs]
    a_rk = [jnp.where(incl_lower, scores[p][P2:, P2:], 0.0) for p in pairs]
    x0 = [jnp.concatenate([at[p], _dotb(a_ak[p], vb[p])], axis=1) for p in pairs]
    eye_f = jnp.where(eye, 1.0, 0.0)
    inv = [eye_f + a_ab[p] for p in pairs]
    pw = [_dotb(a_ab[p], a_ab[p]) for p in pairs]
    for _ in range(int(np.log2(C)) - 2):
        both = [_dotb(pw[p], jnp.concatenate([pw[p], inv[p]], axis=1)) for p in pairs]
        inv = [inv[p] + both[p][:, P2:] for p in pairs]
        pw = [both[p][:, :P2] for p in pairs]
    inv = [inv[p] + _dotb(pw[p], inv[p]) for p in pairs]
    x = [_dotb(inv[p], x0[p]) for p in pairs]
    rhs = [jnp.concatenate([x[p], jnp.concatenate([zeros_pp, vb[p]], axis=1)], axis=0) for p in pairs]
    gz = [_dotb(jnp.concatenate([a_rb[p], a_rk[p]], axis=1), rhs[p]) for p in pairs]
    mn = [_dotb(jnp.concatenate([bh[p], kh[p]], axis=0).T, rhs[p]) for p in pairs]
    yh = []
    for p in pairs:
        g = rt[p] + gz[p][:, :P2]
        m = mn[p][:, :P2] + jnp.where(eye, jnp.broadcast_to(jnp.exp(tot[p]), (P2, P2)), 0.0)
        yh.append(_dotb(jnp.concatenate([g, m], axis=0), h_ref[p]))
    ys = []
    for p in pairs:
        y2 = yh[p][:P2] + gz[p][:, P2:]
        h_ref[p] = yh[p][P2:] + mn[p][:, P2:]
        ys.append(y2[:C] + y2[C:])
    mean = _split_dots(ys, head_ones)
    yc = [ys[p] - mean[p] * (1.0 / RWKV_HEAD_DIM) for p in pairs]
    var = _split_dots([yc[p] * yc[p] for p in pairs], head_ones)
    rk = _split_dots([r_p[p] * k_p[p] * r_k[:, sl[p]] for p in pairs], head_ones)
    for p in pairs:
        yn = yc[p] * lax.rsqrt(var[p] * (1.0 / RWKV_HEAD_DIM) + RWKV_GN_EPS) * lnx_g[:, sl[p]] + lnx_b[:, sl[p]]
        o_ref[:, sl[p]] = ((yn + rk[p] * v_p[p]) * gate_all[:, sl[p]]).astype(o_ref.dtype)


def rwkv_core(p, v_first, vres, mu, w0, w2, a0, a2, g2, k_k, k_a, r_k, lnx_g, lnx_b, B, S):
    C = WKV_CHUNK
    n = S // C
    has_vres = vres is not None
    r_mu, k_mu, v_mu, wd_mu, ad_mu, gd_mu = jnp.split(mu, RWKV_SPLITS)
    mu3 = jnp.stack([r_mu, k_mu, v_mu])
    mu_wa = jnp.concatenate([wd_mu, ad_mu]).reshape(1, RWKV_WA_PAD)
    mu_g = jnp.pad(gd_mu, (0, RWKV_G_PAD - RWKV_LORA_G)).reshape(1, RWKV_G_PAD)
    w2p = jnp.pad(w2, ((0, RWKV_WA_PAD - RWKV_LORA_W), (0, 0))).astype(bf16)
    a2p = jnp.pad(a2, ((RWKV_LORA_W, 0), (0, 0))).astype(bf16)
    g2p = jnp.pad(g2, ((0, RWKV_G_PAD - RWKV_LORA_G), (0, 0))).astype(bf16)
    vecs = [w0, a0, k_k, k_a, r_k.reshape(MIX_B), lnx_g, lnx_b]
    if has_vres:
        vecs.append(vres[0])
    vec = jnp.stack(vecs)
    nv = vec.shape[0]
    row = lambda bi, ci: bi * n + ci
    const = lambda bi, ci: (0, 0)
    tile = lambda w, j: pl.BlockSpec((C, w), lambda bi, ci: (row(bi, ci), j))
    in_specs = [tile(MIX_B, EV_R0 // MIX_B), tile(MIX_B, EV_R0 // MIX_B + 1), tile(MIX_B, EV_R0 // MIX_B + 2),
                tile(RWKV_WA_PAD, EV_WA0 // RWKV_WA_PAD),
                tile(RWKV_G_PAD // 2, EV_G0 // (RWKV_G_PAD // 2)), tile(RWKV_G_PAD // 2, EV_G0 // (RWKV_G_PAD // 2) + 1)]
    args = [p, p, p, p, p, p]
    if has_vres:
        in_specs.append(tile(MIX_B, 0))
        args.append(v_first)
    in_specs += [pl.BlockSpec((3, MIX_B), const), pl.BlockSpec((1, RWKV_WA_PAD), const),
                 pl.BlockSpec((1, RWKV_G_PAD), const), pl.BlockSpec((RWKV_WA_PAD, MIX_B), const),
                 pl.BlockSpec((RWKV_WA_PAD, MIX_B), const), pl.BlockSpec((RWKV_G_PAD, MIX_B), const),
                 pl.BlockSpec((nv, MIX_B), const)]
    args += [mu3, mu_wa, mu_g, w2p, a2p, g2p, vec]
    out_specs = [tile(MIX_B, 0)]
    out_shape = [jax.ShapeDtypeStruct((B * S, MIX_B), bf16)]
    if has_vres:
        v1p = jnp.pad(vres[1], ((0, 0), (0, RWKV_V_PAD - vres[1].shape[1]))).astype(bf16)
        v2p = jnp.pad(vres[2], ((0, RWKV_V_PAD - vres[2].shape[0]), (0, 0))).astype(bf16)
        in_specs += [pl.BlockSpec((MIX_B, RWKV_V_PAD), const), pl.BlockSpec((RWKV_V_PAD, MIX_B), const)]
        args += [v1p, v2p]
    else:
        out_specs.append(tile(MIX_B, 0))
        out_shape.append(jax.ShapeDtypeStruct((B * S, MIX_B), f32))
    outs = pl.pallas_call(
        functools.partial(_rwkv_kernel, has_vres),
        grid=(B, n),
        in_specs=in_specs,
        out_specs=out_specs,
        out_shape=out_shape,
        scratch_shapes=[pltpu.VMEM((WKV_PAIRS, WKV_PAIR, WKV_PAIR), f32),
                        pltpu.VMEM((3, 8, MIX_B), f32), pltpu.VMEM((1, 8, RWKV_WA_PAD), f32),
                        pltpu.VMEM((1, 8, RWKV_G_PAD), f32)],
        compiler_params=pltpu.CompilerParams(
            dimension_semantics=("parallel", "arbitrary"), vmem_limit_bytes=VMEM_LIMIT_BYTES),
        name="rwkv7_time_mix",
    )(*args)
    return (outs[0], v_first) if has_vres else (outs[0], outs[1])


RET_LOG_GAMMA = tuple(float(np.log1p(-np.exp2(-5.0 - h))) for h in range(RET_HEADS))
ROPE_HALF = RET_HEAD_DIM // 2


def _ret_kernel(q_ref, k_ref, v_ref, g_ref, pos_ref, inv_ref, gng_ref, gnb_ref, o_ref, s_ref):
    C = RET_CHUNK
    d = RET_HEAD_DIM

    @pl.when(pl.program_id(1) == 0)
    def _():
        s_ref[...] = jnp.zeros_like(s_ref)

    ang = pos_ref[...] * inv_ref[...]
    cos = jnp.cos(ang)
    sin = jnp.sin(ang)

    def rot(x):
        x1, x2 = x[:, :ROPE_HALF], x[:, ROPE_HALF:]
        return jnp.concatenate([x1 * cos - x2 * sin, x1 * sin + x2 * cos], axis=-1)

    ri = lax.broadcasted_iota(jnp.int32, (C, C), 0)
    ci = lax.broadcasted_iota(jnp.int32, (C, C), 1)
    diff = (ri - ci).astype(f32)
    row = lax.broadcasted_iota(jnp.int32, (C, d), 0).astype(f32)
    nt = (((1,), (1,)), ((), ()))

    for h in range(RET_HEADS):
        hs = slice(h * d, (h + 1) * d)
        lg = RET_LOG_GAMMA[h]
        q = rot(q_ref[:, hs])
        k = rot(k_ref[:, hs]) * (d ** -0.5)
        v = v_ref[:, hs].astype(bf16)
        decay = jnp.where(diff >= 0, jnp.exp(lg * jnp.maximum(diff, 0.0)), 0.0)
        scores = lax.dot_general(q.astype(bf16), k.astype(bf16), nt, preferred_element_type=f32) * decay
        state = s_ref[h]
        y = jnp.dot(scores.astype(bf16), v, preferred_element_type=f32)
        y = y + jnp.dot((q * jnp.exp(lg * (row + 1.0))).astype(bf16), state.astype(bf16), preferred_element_type=f32)
        kw = (k * jnp.exp(lg * (C - 1.0 - row))).astype(bf16)
        s_ref[h] = state * float(np.exp(lg * C)) + jnp.dot(kw.T, v, preferred_element_type=f32)
        mu = jnp.mean(y, axis=-1, keepdims=True)
        yc = y - mu
        var = jnp.mean(yc * yc, axis=-1, keepdims=True)
        yn = yc * lax.rsqrt(var + LN_EPS) * gng_ref[:, hs] + gnb_ref[:, hs]
        g = g_ref[:, hs]
        o_ref[:, hs] = (g * jax.nn.sigmoid(g) * yn).astype(o_ref.dtype)


def retention_core(p, positions, gn_g, gn_b, B, S):
    C = RET_CHUNK
    n = S // C
    inv = (ROPE_BASE ** (-jnp.arange(ROPE_HALF, dtype=f32) / ROPE_HALF)).reshape(1, ROPE_HALF)
    pos = positions.astype(f32).reshape(B * S, 1)
    row = lambda bi, ci: bi * n + ci
    const = lambda bi, ci: (0, 0)
    col = lambda j: pl.BlockSpec((C, MIX_A), lambda bi, ci: (row(bi, ci), j))
    return pl.pallas_call(
        _ret_kernel,
        grid=(B, n),
        in_specs=[col(0), col(1), col(2), col(3),
                  pl.BlockSpec((C, 1), lambda bi, ci: (row(bi, ci), 0)),
                  pl.BlockSpec((1, ROPE_HALF), const),
                  pl.BlockSpec((1, MIX_A), const),
                  pl.BlockSpec((1, MIX_A), const)],
        out_specs=pl.BlockSpec((C, MIX_A), lambda bi, ci: (row(bi, ci), 0)),
        out_shape=jax.ShapeDtypeStruct((B * S, MIX_A), bf16),
        scratch_shapes=[pltpu.VMEM((RET_HEADS, RET_HEAD_DIM, RET_HEAD_DIM), f32)],
        compiler_params=pltpu.CompilerParams(
            dimension_semantics=("parallel", "arbitrary"), vmem_limit_bytes=VMEM_LIMIT_BYTES),
        name="retention_chunked",
    )(p, p, p, p, pos, inv, gn_g.reshape(1, MIX_A), gn_b.reshape(1, MIX_A))


GLA_LEVELS = int(np.log2(GLA_CHUNK))
GLA_DECAY_ROWS = (2 + GLA_LEVELS) * GLA_CHUNK
GLA_AD_PAD = 128
ODD_PROJ_PAD = 2 * GLA_DK + 2 * GLA_DV + GLA_AD_PAD
ODD_TN = ODD_PROJ_PAD // 7


def _gla_decay_operator():
    C = GLA_CHUNK
    i = np.arange(C)[:, None]
    j = np.arange(C)[None, :]
    blocks = [(j <= i), (j > i)]
    for lvl in range(GLA_LEVELS):
        s = C >> (lvl + 1)
        mid = (i // (2 * s)) * (2 * s) + s
        right = i >= mid
        blocks.append(np.where(right, (j >= mid) & (j <= i), (j > i) & (j < mid)))
    return np.concatenate(blocks, 0).astype(np.float32)


def _gla_kernel(q_ref, k_ref, v_ref, r_ref, ad_ref, a2_ref, ab_ref, g_ref, lop_ref, o_ref, s_ref):
    C = GLA_CHUNK

    @pl.when(pl.program_id(1) == 0)
    def _():
        s_ref[...] = jnp.zeros_like(s_ref)

    x = jnp.dot(ad_ref[...].astype(bf16), a2_ref[...], preferred_element_type=f32) + ab_ref[...]
    la = (jnp.minimum(x, 0.0) - jnp.log1p(jnp.exp(-jnp.abs(x)))) * (1.0 / GLA_GATE_NORM)
    hi = la.astype(bf16)
    r1 = la - hi.astype(f32)
    mid = r1.astype(bf16)
    lo = (r1 - mid.astype(f32)).astype(bf16)
    lop = lop_ref[...]
    dec = (jnp.dot(lop, hi, preferred_element_type=f32) + jnp.dot(lop, mid, preferred_element_type=f32)
           + jnp.dot(lop, lo, preferred_element_type=f32))

    ri = lax.broadcasted_iota(jnp.int32, (C, C), 0)
    ci = lax.broadcasted_iota(jnp.int32, (C, C), 1)
    level_masks = []
    for lvl in range(GLA_LEVELS):
        s = C >> (lvl + 1)
        same_block = (ri // (2 * s)) == (ci // (2 * s))
        level_masks.append(same_block & ((ri % (2 * s)) >= s) & ((ci % (2 * s)) < s))
    eye = ri == ci
    nt = (((1,), (1,)), ((), ()))

    heads = range(GLA_HEADS)
    ks = [slice(h * GLA_HEAD_DK, (h + 1) * GLA_HEAD_DK) for h in heads]
    vs = [slice(h * GLA_HEAD_DV, (h + 1) * GLA_HEAD_DV) for h in heads]
    q = [q_ref[:, ks[h]] * (GLA_HEAD_DK ** -0.5) for h in heads]
    k = [k_ref[:, ks[h]] for h in heads]
    v = [v_ref[:, vs[h]].astype(bf16) for h in heads]
    e = [jnp.exp(dec[:, ks[h]]) for h in heads]
    state = [s_ref[h] for h in heads]
    o = [lax.dot_general((q[h] * e[h][:C]).astype(bf16), state[h].astype(bf16), nt, preferred_element_type=f32)
         for h in heads]
    attn = [jnp.where(eye, lax.dot_general(q[h].astype(bf16), k[h].astype(bf16), nt, preferred_element_type=f32), 0.0)
            for h in heads]
    for lvl in range(GLA_LEVELS):
        rows = slice((2 + lvl) * C, (3 + lvl) * C)
        sc = [lax.dot_general((q[h] * e[h][rows]).astype(bf16), (k[h] * e[h][rows]).astype(bf16), nt,
                              preferred_element_type=f32) for h in heads]
        attn = [attn[h] + jnp.where(level_masks[lvl], sc[h], 0.0) for h in heads]
    o = [o[h] + jnp.dot(attn[h].astype(bf16), v[h], preferred_element_type=f32) for h in heads]
    for h in heads:
        kd = (k[h] * e[h][C:2 * C]).astype(bf16)
        s_ref[h] = state[h] * e[h][C - 1:C] + jnp.dot(v[h].T, kd, preferred_element_type=f32)
    for h in heads:
        on = o[h] * lax.rsqrt(jnp.mean(o[h] * o[h], axis=-1, keepdims=True) + LN_EPS) * g_ref[:, vs[h]]
        rr = r_ref[:, vs[h]]
        o_ref[:, vs[h]] = (rr * jax.nn.sigmoid(rr) * on).astype(o_ref.dtype)


def gla_core(p, a2, ab, norm_g, B, S):
    C = GLA_CHUNK
    n = S // C
    a2p = jnp.pad(a2, ((0, GLA_AD_PAD - GLA_LORA), (0, 0))).astype(bf16)
    lop = jnp.asarray(_gla_decay_operator(), dtype=bf16)
    row = lambda bi, ci: bi * n + ci
    const = lambda bi, ci: (0, 0)
    return pl.pallas_call(
        _gla_kernel,
        grid=(B, n),
        in_specs=[pl.BlockSpec((C, GLA_DK), lambda bi, ci: (row(bi, ci), 0)),
                  pl.BlockSpec((C, GLA_DK), lambda bi, ci: (row(bi, ci), 1)),
                  pl.BlockSpec((C, GLA_DV), lambda bi, ci: (row(bi, ci), 1)),
                  pl.BlockSpec((C, GLA_DV), lambda bi, ci: (row(bi, ci), 2)),
                  pl.BlockSpec((C, GLA_AD_PAD), lambda bi, ci: (row(bi, ci), (2 * GLA_DK + 2 * GLA_DV) // GLA_AD_PAD)),
                  pl.BlockSpec((GLA_AD_PAD, GLA_DK), const),
                  pl.BlockSpec((1, GLA_DK), const),
                  pl.BlockSpec((1, GLA_DV), const),
                  pl.BlockSpec((GLA_DECAY_ROWS, C), const)],
        out_specs=pl.BlockSpec((C, GLA_DV), lambda bi, ci: (row(bi, ci), 0)),
        out_shape=jax.ShapeDtypeStruct((B * S, GLA_DV), bf16),
        scratch_shapes=[pltpu.VMEM((GLA_HEADS, GLA_HEAD_DV, GLA_HEAD_DK), f32)],
        compiler_params=pltpu.CompilerParams(
            dimension_semantics=("parallel", "arbitrary"), vmem_limit_bytes=VMEM_LIMIT_BYTES),
        name="gla_chunked",
    )(p, p, p, p, p, a2p, ab.reshape(1, GLA_DK), norm_g.reshape(1, GLA_DV), lop)


PROJ_TM = 256
ROUTER_PAD = 128


def _proj_norm_kernel(n_in, *refs):
    a_refs = refs[:n_in]
    w_ref, x_ref, mod_ref, g_ref, b_ref, wr_ref, br_ref, xo_ref, ho_ref, lo_ref = refs[n_in:]
    acc = None
    k0 = 0
    for a_ref in a_refs:
        kw = a_ref.shape[1]
        part = jnp.dot(a_ref[...], w_ref[k0:k0 + kw, :], preferred_element_type=f32)
        acc = part if acc is None else acc + part
        k0 += kw
    gt, sc, sh = mod_ref[0, 0:1, :], mod_ref[0, 1:2, :], mod_ref[0, 2:3, :]
    z = DEEPNORM_ALPHA * x_ref[...] + (1.0 + gt) * acc
    mu = jnp.mean(z, axis=-1, keepdims=True)
    zc = z - mu
    var = jnp.mean(zc * zc, axis=-1, keepdims=True)
    xn = zc * lax.rsqrt(var + LN_EPS) * g_ref[...] + b_ref[...]
    xo_ref[...] = xn
    hf = xn * (1.0 + sc) + sh
    ho_ref[...] = hf
    lo_ref[...] = jnp.dot(hf.astype(bf16), wr_ref[...], preferred_element_type=f32) + br_ref[...]


def proj_norm(acts, wo, x, mod, ln_g, ln_b, w_router, b_router, S):
    T, D = x.shape
    tiles_per_batch = S // PROJ_TM
    const = lambda i: (0, 0)
    rowblk = lambda w: pl.BlockSpec((PROJ_TM, w), lambda i: (i, 0))
    in_specs = [rowblk(a.shape[1]) for a in acts]
    in_specs += [pl.BlockSpec(wo.shape, const), rowblk(D),
                 pl.BlockSpec((1, 3, D), lambda i: (i // tiles_per_batch, 0, 0)),
                 pl.BlockSpec((1, D), const), pl.BlockSpec((1, D), const),
                 pl.BlockSpec((D, ROUTER_PAD), const), pl.BlockSpec((1, ROUTER_PAD), const)]
    return pl.pallas_call(
        functools.partial(_proj_norm_kernel, len(acts)),
        grid=(T // PROJ_TM,),
        in_specs=in_specs,
        out_specs=[rowblk(D), rowblk(D), rowblk(ROUTER_PAD)],
        out_shape=[jax.ShapeDtypeStruct((T, D), f32), jax.ShapeDtypeStruct((T, D), f32),
                   jax.ShapeDtypeStruct((T, ROUTER_PAD), f32)],
        compiler_params=pltpu.CompilerParams(
            dimension_semantics=("parallel",), vmem_limit_bytes=VMEM_LIMIT_BYTES),
        name="proj_norm",
    )(*acts, wo.astype(bf16), x, mod, ln_g.reshape(1, D), ln_b.reshape(1, D), w_router, b_router)


MOE_ROWS = 256
MOE_VMEM_LIMIT_BYTES = 56 * 1024 * 1024
COMBINE_ROWS = 256
SUBLANES = 8
GATHER_DMA_PRIORITY = 1


def _moe_ffn_kernel(layer, tok_ref, be_ref, nused_ref, x_hbm, w1_hbm, w3_hbm, w2_hbm, y_ref,
                    xbuf, w1buf, w3buf, w2buf, sem, wsem, run_ref):
    i = pl.program_id(0)
    nblk = pl.num_programs(0)
    n_used = nused_ref[0]
    slot = lax.rem(i, 2)
    expert = be_ref[i]
    first = (i < n_used) & ((i == 0) | (expert != be_ref[jnp.maximum(i - 1, 0)]))

    n_groups = MOE_ROWS // SUBLANES

    def more_groups(blk):
        return lambda g: (g < n_groups) & (tok_ref[blk * MOE_ROWS + jnp.minimum(g, n_groups - 1) * SUBLANES] >= 0)

    def start_gather(blk, s):
        def body(g):
            for u in range(SUBLANES):
                t = jnp.maximum(tok_ref[blk * MOE_ROWS + g * SUBLANES + u], 0)
                pltpu.make_async_copy(x_hbm.at[lax.shift_right_logical(t, 3), pl.ds(jnp.bitwise_and(t, SUBLANES - 1), 1)],
                                      xbuf.at[s, g, pl.ds(u, 1)], sem.at[s]).start(priority=GATHER_DMA_PRIORITY)
            return g + 1
        lax.while_loop(more_groups(blk), body, 0)

    def wait_gather(blk, s):
        def body(g):
            pltpu.make_async_copy(x_hbm.at[pl.ds(0, 1)], xbuf.at[s, pl.ds(g, 1)], sem.at[s]).wait()
            return g + 1
        lax.while_loop(more_groups(blk), body, 0)

    def weight_copies(e, b):
        return [pltpu.make_async_copy(w_hbm.at[layer, e], wbuf.at[b], wsem.at[b])
                for w_hbm, wbuf in ((w1_hbm, w1buf), (w3_hbm, w3buf), (w2_hbm, w2buf))]

    @pl.when((i == 0) & (n_used > 0))
    def _():
        run_ref[0] = 0
        xbuf[...] = jnp.zeros_like(xbuf)
        for cp in weight_copies(expert, 0):
            cp.start()
        start_gather(0, 0)

    @pl.when(i + 1 < n_used)
    def _():
        start_gather(i + 1, 1 - slot)

    @pl.when(first)
    def _():
        @pl.when(i > 0)
        def _():
            run_ref[0] = run_ref[0] + 1
        buf = jnp.bitwise_and(run_ref[0], 1)
        nxt = lax.while_loop(lambda j: (j < n_used) & (be_ref[jnp.minimum(j, nblk - 1)] == expert),
                             lambda j: j + 1, i + 1)

        @pl.when(nxt < n_used)
        def _():
            for cp in weight_copies(be_ref[jnp.minimum(nxt, nblk - 1)], 1 - buf):
                cp.start()
        for cp in weight_copies(expert, buf):
            cp.wait()

    @pl.when(i < n_used)
    def _():
        buf = jnp.bitwise_and(run_ref[0], 1)
        wait_gather(i, slot)
        xb = xbuf[slot].reshape(MOE_ROWS, xbuf.shape[-1]).astype(bf16)
        a = jnp.dot(xb, w1buf[buf].astype(bf16), preferred_element_type=f32)
        b = jnp.dot(xb, w3buf[buf].astype(bf16), preferred_element_type=f32)
        hid = (a * jax.nn.sigmoid(a) * b).astype(bf16)
        y_ref[...] = jnp.dot(hid, w2buf[buf].astype(bf16), preferred_element_type=f32)

    @pl.when(i >= n_used)
    def _():
        y_ref[...] = jnp.zeros_like(y_ref)


def moe_ffn(x, buf_tok, blk_expert, n_used, w1, w3, w2, layer):
    D = x.shape[1]
    nblk = blk_expert.shape[0]
    any_spec = pl.BlockSpec(memory_space=pl.ANY)
    return pl.pallas_call(
        functools.partial(_moe_ffn_kernel, layer),
        grid_spec=pltpu.PrefetchScalarGridSpec(
            num_scalar_prefetch=3,
            grid=(nblk,),
            in_specs=[any_spec, any_spec, any_spec, any_spec],
            out_specs=pl.BlockSpec((MOE_ROWS, D), lambda i, tok, be, nu: (i, 0)),
            scratch_shapes=[pltpu.VMEM((2, MOE_ROWS // SUBLANES, SUBLANES, D), f32),
                            pltpu.VMEM((2, D, EXPERT_FF), f32), pltpu.VMEM((2, D, EXPERT_FF), f32),
                            pltpu.VMEM((2, EXPERT_FF, D), f32),
                            pltpu.SemaphoreType.DMA((2,)), pltpu.SemaphoreType.DMA((2,)),
                            pltpu.SMEM((1,), jnp.int32)]),
        out_shape=jax.ShapeDtypeStruct((nblk * MOE_ROWS, D), f32),
        compiler_params=pltpu.CompilerParams(
            dimension_semantics=("arbitrary",), vmem_limit_bytes=MOE_VMEM_LIMIT_BYTES),
        name="moe_expert_ffn",
    )(buf_tok, blk_expert, n_used, x.reshape(x.shape[0] // SUBLANES, SUBLANES, D), w1, w3, w2)


def _moe_combine_kernel(emit_h, s0_ref, s1_ref, y_hbm, gate_ref, x_ref, mod_ref, g_ref, b_ref, *rest):
    if emit_h:
        o_ref, h_ref, buf, sem = rest
    else:
        o_ref, buf, sem = rest
    i = pl.program_id(0)
    n = pl.num_programs(0)
    slot = lax.rem(i, 2)

    def start_gather(blk, s):
        def body(g, carry):
            base = pl.multiple_of(g * SUBLANES, SUBLANES)
            for u in range(SUBLANES):
                t = blk * COMBINE_ROWS + base + u
                pltpu.make_async_copy(y_hbm.at[pl.ds(s0_ref[t], 1)], buf.at[s, 0, pl.ds(base + u, 1)], sem.at[s]).start()
                pltpu.make_async_copy(y_hbm.at[pl.ds(s1_ref[t], 1)], buf.at[s, 1, pl.ds(base + u, 1)], sem.at[s]).start()
            return carry
        lax.fori_loop(0, COMBINE_ROWS // SUBLANES, body, 0)

    @pl.when(i == 0)
    def _():
        start_gather(0, 0)

    @pl.when(i + 1 < n)
    def _():
        start_gather(i + 1, 1 - slot)

    for c in range(MOE_TOPK):
        pltpu.make_async_copy(y_hbm.at[pl.ds(0, COMBINE_ROWS)], buf.at[slot, c], sem.at[slot]).wait()
    y = buf[slot, 0] * gate_ref[:, 0:1] + buf[slot, 1] * gate_ref[:, 1:2]
    z = DEEPNORM_ALPHA * x_ref[...] + (1.0 + mod_ref[0, 0:1, :]) * y
    mu = jnp.mean(z, axis=-1, keepdims=True)
    zc = z - mu
    var = jnp.mean(zc * zc, axis=-1, keepdims=True)
    xn = zc * lax.rsqrt(var + LN_EPS) * g_ref[...] + b_ref[...]
    o_ref[...] = xn
    if emit_h:
        h_ref[...] = (xn * (1.0 + mod_ref[0, 1:2, :]) + mod_ref[0, 2:3, :]).astype(h_ref.dtype)


def moe_combine_norm(y_rows, slot0, slot1, gate, x, mod, ln_g, ln_b, S, emit_h):
    T, D = x.shape
    tiles_per_batch = S // COMBINE_ROWS
    const = lambda i, s0, s1: (0, 0)
    rowblk = pl.BlockSpec((COMBINE_ROWS, D), lambda i, s0, s1: (i, 0))
    out_specs = [rowblk, rowblk] if emit_h else [rowblk]
    out_shape = [jax.ShapeDtypeStruct((T, D), f32)] + ([jax.ShapeDtypeStruct((T, D), bf16)] if emit_h else [])
    return pl.pallas_call(
        functools.partial(_moe_combine_kernel, emit_h),
        grid_spec=pltpu.PrefetchScalarGridSpec(
            num_scalar_prefetch=2,
            grid=(T // COMBINE_ROWS,),
            in_specs=[pl.BlockSpec(memory_space=pl.ANY),
                      pl.BlockSpec((COMBINE_ROWS, MOE_TOPK), lambda i, s0, s1: (i, 0)), rowblk,
                      pl.BlockSpec((1, 3, D), lambda i, s0, s1: (i // tiles_per_batch, 0, 0)),
                      pl.BlockSpec((1, D), const), pl.BlockSpec((1, D), const)],
            out_specs=out_specs,
            scratch_shapes=[pltpu.VMEM((2, MOE_TOPK, COMBINE_ROWS, D), f32), pltpu.SemaphoreType.DMA((2,))]),
        out_shape=out_shape,
        compiler_params=pltpu.CompilerParams(
            dimension_semantics=("arbitrary",), vmem_limit_bytes=VMEM_LIMIT_BYTES),
        name="moe_combine_norm",
    )(slot0, slot1, y_rows, gate, x, mod, ln_g.reshape(1, D), ln_b.reshape(1, D))


def router_params(wg, bg, we, be):
    D = wg.shape[0]
    pad = ROUTER_PAD - MOE_GROUPS - N_EXPERTS
    w = jnp.concatenate([wg, we, jnp.zeros((D, pad), wg.dtype)], 1).astype(bf16)
    b = jnp.concatenate([bg, be, jnp.zeros((pad,), bg.dtype)]).reshape(1, ROUTER_PAD)
    return w, b


def moe_route(logits):
    g_prob = jax.nn.softmax(logits[:, :MOE_GROUPS], -1)
    g_top = jnp.argmax(g_prob, -1).astype(jnp.int32)
    p_top = jnp.max(g_prob, -1)
    ids = jnp.arange(N_EXPERTS, dtype=jnp.int32)[None, :]
    e_sel = jnp.where(ids // MOE_PER_GROUP == g_top[:, None], logits[:, MOE_GROUPS:MOE_GROUPS + N_EXPERTS], -jnp.inf)
    i1 = jnp.argmax(e_sel, -1).astype(jnp.int32)
    v1 = jnp.max(e_sel, -1)
    rest = jnp.where(ids == i1[:, None], -jnp.inf, e_sel)
    i2 = jnp.argmax(rest, -1).astype(jnp.int32)
    v2 = jnp.max(rest, -1)
    gate = p_top[:, None] * jax.nn.softmax(jnp.stack([v1, v2], -1), -1)
    return gate, jnp.stack([i1, i2], -1)


def moe_layout(expert):
    T = expert.shape[0]
    A = T * MOE_TOPK
    e_flat = expert.reshape(A)
    onehot = (e_flat[:, None] == jnp.arange(N_EXPERTS, dtype=jnp.int32)[None, :]).astype(jnp.int32)
    rank = jnp.cumsum(onehot, axis=0)
    counts = rank[-1]
    padded = (counts + MOE_ROWS - 1) // MOE_ROWS * MOE_ROWS
    pad_end = jnp.cumsum(padded)
    pad_start = pad_end - padded
    slot = jnp.sum(onehot * (pad_start[None, :] + rank - 1), axis=1).astype(jnp.int32)
    tok_flat = jnp.arange(A, dtype=jnp.int32) // MOE_TOPK
    nblk = A // MOE_ROWS + N_EXPERTS
    buf_tok = jnp.full((nblk * MOE_ROWS,), -1, jnp.int32).at[slot].set(tok_flat)
    n_used = (pad_end[-1] // MOE_ROWS).astype(jnp.int32)
    blk = jnp.minimum(jnp.arange(nblk, dtype=jnp.int32), n_used - 1) * MOE_ROWS
    blk_expert = jnp.minimum(jnp.sum(blk[:, None] >= pad_end[None, :], axis=1), N_EXPERTS - 1).astype(jnp.int32)
    slot2 = slot.reshape(T, MOE_TOPK)
    return buf_tok, blk_expert, n_used.reshape(1), slot2[:, 0], slot2[:, 1]


def kernel(x, c, positions, ada_w, ada_b, ln_g, ln_b, ev_win, ev_wo, ret_gn_g, ret_gn_b, rw_mu, rw_w0, rw_w2, rw_a0, rw_a2, rw_g2, rw_kk, rw_ka, rw_rk, rw_lnx_g, rw_lnx_b, rw_v0, rw_v1, rw_v2, od_win, od_wo, gla_a2, gla_ab, gla_norm_g, moe_wg, moe_bg, moe_we, moe_be, moe_w1, moe_w3, moe_w2):
    B, S, D = x.shape
    T = B * S
    depth = ada_w.shape[0]
    mods = ada_mod(c, ada_w, ada_b).reshape(depth, B, 6, D)
    h = (x * (1.0 + mods[0, :, 1])[:, None, :] + mods[0, :, 0][:, None, :]).astype(bf16).reshape(T, D)
    x = x.reshape(T, D)
    ev_w = padded_bf16(ev_win, EVEN_COLS)
    od_w = padded_bf16(od_win, ODD_PROJ_PAD)
    v_first = None
    for layer in range(depth):
        j = layer // 2
        mixer_mod = jnp.stack([mods[layer, :, 2], mods[layer, :, 4], mods[layer, :, 3]], axis=1)
        if layer % 2 == 0:
            p = matmul(h, ev_w, j, tn=EVEN_TN)
            out_a = retention_core(p, positions, ret_gn_g[j], ret_gn_b[j], B, S)
            vres = None if j == 0 else (rw_v0[j - 1], rw_v1[j - 1], rw_v2[j - 1])
            out_b, v_first = rwkv_core(p, v_first, vres, rw_mu[j], rw_w0[j], rw_w2[j], rw_a0[j], rw_a2[j], rw_g2[j],
                                       rw_kk[j], rw_ka[j], rw_rk[j], rw_lnx_g[j], rw_lnx_b[j], B, S)
            acts, wo = [out_a, out_b], ev_wo[j]
        else:
            p = matmul(h, od_w, j, tn=ODD_TN)
            acts, wo = [gla_core(p, gla_a2[j], gla_ab[j], gla_norm_g[j], B, S)], od_wo[j]
        w_router, b_router = router_params(moe_wg[layer], moe_bg[layer], moe_we[layer], moe_be[layer])
        x, hf, logits = proj_norm(acts, wo, x, mixer_mod, ln_g[layer, 0], ln_b[layer, 0], w_router, b_router, S)
        gate, expert = moe_route(logits)
        buf_tok, blk_expert, n_used, slot0, slot1 = moe_layout(expert)
        y_rows = moe_ffn(hf, buf_tok, blk_expert, n_used, moe_w1, moe_w3, moe_w2, layer)
        last = layer == depth - 1
        nxt = layer if last else layer + 1
        moe_mod = jnp.stack([mods[layer, :, 5], mods[nxt, :, 1], mods[nxt, :, 0]], axis=1)
        outs = moe_combine_norm(y_rows, slot0, slot1, gate, x, moe_mod, ln_g[layer, 1], ln_b[layer, 1], S, not last)
        x = outs[0]
        if not last:
            h = outs[1]
    return x.reshape(B, S, D)
```

```python
import functools

import jax
import jax.numpy as jnp
import numpy as np
from jax import lax
from jax.experimental import pallas as pl
from jax.experimental.pallas import tpu as pltpu

D_MODEL = 2048
DEPTH = 4
MIX_A = D_MODEL // 2
RET_HEAD_DIM = 256
RET_HEADS = MIX_A // RET_HEAD_DIM
RET_CHUNK = 128
RET_COLS = 4 * MIX_A
ROPE_BASE = 10000.0
MIX_B = D_MODEL - MIX_A
RWKV_HEAD_DIM = 64
RWKV_HEADS = MIX_B // RWKV_HEAD_DIM
RWKV_LORA_W = 64
RWKV_LORA_A = 64
RWKV_LORA_G = 160
RWKV_COLS = 3 * MIX_B + RWKV_LORA_W + RWKV_LORA_A + RWKV_LORA_G
RWKV_SPLITS = (MIX_B, 2 * MIX_B, 3 * MIX_B, 3 * MIX_B + RWKV_LORA_W, 3 * MIX_B + RWKV_LORA_W + RWKV_LORA_A)
RWKV_GN_EPS = 64e-5
GLA_HEADS = 4
GLA_DK = D_MODEL // 2
GLA_DV = D_MODEL
GLA_HEAD_DK = GLA_DK // GLA_HEADS
GLA_HEAD_DV = GLA_DV // GLA_HEADS
GLA_LORA = 16
GLA_GATE_NORM = 16.0
GLA_CHUNK = 64
ODD_PROJ = 2 * GLA_DK + 2 * GLA_DV + GLA_LORA
MOE_GROUPS = 4
MOE_PER_GROUP = 8
N_EXPERTS = MOE_GROUPS * MOE_PER_GROUP
MOE_TOPK = 2
EXPERT_FF = D_MODEL // 4
DEEPNORM_ALPHA = (2 * DEPTH) ** 0.25
LN_EPS = 1e-5

f32 = jnp.float32
bf16 = jnp.bfloat16

VMEM_LIMIT_BYTES = 48 * 1024 * 1024
DENSE_VMEM_LIMIT_BYTES = 56 * 1024 * 1024


def _dotb(a, b):
    return jnp.dot(a.astype(bf16), b.astype(bf16), preferred_element_type=f32)


def _split_dots(xs, ones_bf16):
    n, rows = len(xs), xs[0].shape[0]
    his = [x.astype(bf16) for x in xs]
    los = [(x - h.astype(f32)).astype(bf16) for x, h in zip(xs, his)]
    out = jnp.dot(jnp.concatenate(his + los, axis=0), ones_bf16, preferred_element_type=f32)
    return [out[i * rows:(i + 1) * rows] + out[(n + i) * rows:(n + i + 1) * rows] for i in range(n)]


def _softplus(z):
    return jnp.maximum(z, 0.0) + jnp.log1p(jnp.exp(-jnp.abs(z)))


ADA_ROWS = 8
ADA_TN = 2048


def _ada_kernel(c_ref, w_ref, b_ref, o_ref):
    c = c_ref[...]
    act = (c * jax.nn.sigmoid(c)).astype(bf16)
    o_ref[0] = jnp.dot(act, w_ref[0].astype(bf16), preferred_element_type=f32) + b_ref[0]


def ada_mod(c, ada_w, ada_b):
    B, D = c.shape
    depth, _, N = ada_w.shape
    c_pad = jnp.pad(c, ((0, ADA_ROWS - B), (0, 0)))
    out = pl.pallas_call(
        _ada_kernel,
        grid=(depth, N // ADA_TN),
        in_specs=[pl.BlockSpec((ADA_ROWS, D), lambda l, j: (0, 0)),
                  pl.BlockSpec((1, D, ADA_TN), lambda l, j: (l, 0, j)),
                  pl.BlockSpec((1, 1, ADA_TN), lambda l, j: (l, 0, j))],
        out_specs=pl.BlockSpec((1, ADA_ROWS, ADA_TN), lambda l, j: (l, 0, j)),
        out_shape=jax.ShapeDtypeStruct((depth, ADA_ROWS, N), f32),
        compiler_params=pltpu.CompilerParams(
            dimension_semantics=("parallel", "parallel"), vmem_limit_bytes=VMEM_LIMIT_BYTES),
        name="ada_mod",
    )(c_pad, ada_w, ada_b.reshape(depth, 1, N))
    return out[:, :B]


def _matmul_kernel(x_ref, w_ref, o_ref):
    o_ref[...] = jnp.dot(x_ref[...], w_ref[0], preferred_element_type=f32)


def padded_bf16(w, cols):
    return jnp.pad(w, ((0, 0), (0, 0), (0, cols - w.shape[2]))).astype(bf16)


def matmul(x, w, layer, tm=2048, tn=512):
    M, K = x.shape
    Np = w.shape[2]
    assert Np % tn == 0 and M % min(tm, M) == 0
    tm = min(tm, M)
    return pl.pallas_call(
        _matmul_kernel,
        grid=(M // tm, Np // tn),
        in_specs=[pl.BlockSpec((tm, K), lambda i, j: (i, 0)),
                  pl.BlockSpec((1, K, tn), lambda i, j: (layer, 0, j))],
        out_specs=pl.BlockSpec((tm, tn), lambda i, j: (i, j)),
        out_shape=jax.ShapeDtypeStruct((M, Np), f32),
        compiler_params=pltpu.CompilerParams(
            dimension_semantics=("parallel", "parallel"), vmem_limit_bytes=DENSE_VMEM_LIMIT_BYTES),
        name="dense_matmul",
    )(x, w)


WKV_CHUNK = 64
WKV_PAIR = 2 * RWKV_HEAD_DIM
WKV_PAIRS = MIX_B // WKV_PAIR
RWKV_WA_PAD = 128
RWKV_G_PAD = 256
RWKV_V_PAD = 128
EV_R0 = RET_COLS
EV_WA0 = EV_R0 + 3 * MIX_B
EV_G0 = EV_WA0 + RWKV_WA_PAD
EVEN_TN = 512
EVEN_COLS = -(-(EV_G0 + RWKV_G_PAD) // EVEN_TN) * EVEN_TN


def _rwkv_kernel(has_vres, *refs):
    if has_vres:
        (r_ref, k_ref, v_ref, wa_ref, gd0_ref, gd1_ref, vf_ref, mu_ref, muwa_ref, mug_ref, w2_ref, a2_ref, g2_ref,
         vec_ref, v1_ref, v2_ref, o_ref, h_ref, prev_ref, prevwa_ref, prevg_ref) = refs
    else:
        (r_ref, k_ref, v_ref, wa_ref, gd0_ref, gd1_ref, mu_ref, muwa_ref, mug_ref, w2_ref, a2_ref, g2_ref,
         vec_ref, o_ref, vf_ref, h_ref, prev_ref, prevwa_ref, prevg_ref) = refs
    C = WKV_CHUNK
    P2 = WKV_PAIR

    @pl.when(pl.program_id(1) == 0)
    def _():
        h_ref[...] = jnp.zeros_like(h_ref)
        prev_ref[...] = jnp.zeros_like(prev_ref)
        prevwa_ref[...] = jnp.zeros_like(prevwa_ref)
        prevg_ref[...] = jnp.zeros_like(prevg_ref)

    def shift(x, carry_ref, j, mu):
        first = lax.broadcasted_iota(jnp.int32, x.shape, 0) == 0
        prev = jnp.where(first, jnp.broadcast_to(carry_ref[j, 7:8, :], x.shape), pltpu.roll(x, 1, 0))
        carry_ref[j] = x[C - 8:, :]
        return x + (prev - x) * mu

    r_all = shift(r_ref[...], prev_ref, 0, mu_ref[0:1, :])
    k_all = shift(k_ref[...], prev_ref, 1, mu_ref[1:2, :])
    v_all = shift(v_ref[...], prev_ref, 2, mu_ref[2:3, :])
    wa = shift(wa_ref[...], prevwa_ref, 0, muwa_ref[...])
    gd = shift(jnp.concatenate([gd0_ref[...], gd1_ref[...]], axis=1), prevg_ref, 0, mug_ref[...])
    w0, a0, k_k, k_a, r_k, lnx_g, lnx_b = (vec_ref[i:i + 1, :] for i in range(7))

    w_log = -_softplus(-(w0 + _dotb(jnp.tanh(wa), w2_ref[...]))) - 0.5
    lw_all = -jnp.exp(w_log)
    a_all = jax.nn.sigmoid(a0 + _dotb(wa, a2_ref[...]))
    gate_all = _dotb(jax.nn.sigmoid(gd), g2_ref[...])
    if has_vres:
        v0 = vec_ref[7:8, :]
        mix = jax.nn.sigmoid(v0 + _dotb(_dotb(v_all, v1_ref[...]), v2_ref[...]))
        v_all = v_all + (vf_ref[...] - v_all) * mix
    else:
        vf_ref[...] = v_all
    kkraw_all = k_all * k_k
    k_all = k_all * (1.0 + (a_all - 1.0) * k_a)

    row = lax.broadcasted_iota(jnp.int32, (P2, P2), 0)
    col = lax.broadcasted_iota(jnp.int32, (P2, P2), 1)
    same_head = (row >= C) == (col >= RWKV_HEAD_DIM)
    head_ones = jnp.where((row >= RWKV_HEAD_DIM) == (col >= RWKV_HEAD_DIM), 1.0, 0.0).astype(bf16)
    t_idx = jnp.bitwise_and(row, C - 1)
    s_idx = jnp.bitwise_and(col, C - 1)
    strict_lower = t_idx > s_idx
    incl_lower = t_idx >= s_idx
    eye = row == col
    tri = (lax.broadcasted_iota(jnp.int32, (C, C), 0) >= lax.broadcasted_iota(jnp.int32, (C, C), 1)).astype(bf16)
    zeros_pp = jnp.zeros((P2, P2), f32)
    nt = (((1,), (1,)), ((), ()))
    pairs = range(WKV_PAIRS)
    sl = [slice(p * P2, (p + 1) * P2) for p in pairs]

    def bd(x):
        return jnp.where(same_head, jnp.concatenate([x, x], axis=0), 0.0)

    kkraw = [kkraw_all[:, sl[p]] for p in pairs]
    sumsq = _split_dots([kkraw[p] * kkraw[p] for p in pairs], head_ones)
    kk = [kkraw[p] / jnp.maximum(jnp.sqrt(sumsq[p]), 1e-12) for p in pairs]
    lw = [lw_all[:, sl[p]] for p in pairs]
    cum = []
    for p in pairs:
        hi = lw[p].astype(bf16)
        r1 = lw[p] - hi.astype(f32)
        mid = r1.astype(bf16)
        lo = (r1 - mid.astype(f32)).astype(bf16)
        cum.append(jnp.dot(tri, hi, preferred_element_type=f32) + jnp.dot(tri, mid, preferred_element_type=f32)
                   + jnp.dot(tri, lo, preferred_element_type=f32))
    tot = [cum[p][C - 1:C, :] for p in pairs]
    w_incl = [jnp.exp(cum[p]) for p in pairs]
    w_excl = [jnp.exp(cum[p] - lw[p]) for p in pairs]
    w_inv = [jnp.exp(-cum[p]) for p in pairs]
    w_rest = [jnp.exp(tot[p] - cum[p]) for p in pairs]
    r_p = [r_all[:, sl[p]] for p in pairs]
    k_p = [k_all[:, sl[p]] for p in pairs]
    v_p = [v_all[:, sl[p]] for p in pairs]
    a_p = [a_all[:, sl[p]] for p in pairs]
    at = [bd(-kk[p] * w_excl[p]) for p in pairs]
    rt = [bd(r_p[p] * w_incl[p]) for p in pairs]
    bt = [bd(kk[p] * a_p[p] * w_inv[p]) for p in pairs]
    kt = [bd(k_p[p] * w_inv[p]) for p in pairs]
    bh = [bd(kk[p] * a_p[p] * w_rest[p]) for p in pairs]
    kh = [bd(k_p[p] * w_rest[p]) for p in pairs]
    vb = [bd(v_p[p]) for p in pairs]

    scores = [lax.dot_general(jnp.concatenate([at[p], rt[p]], axis=0).astype(bf16),
                              jnp.concatenate([bt[p], kt[p]], axis=0).astype(bf16),
                              nt, preferred_element_type=f32) for p in pairs]
    a_ab = [jnp.where(strict_lower, scores[p][:P2, :P2], 0.0) for p in pairs]
    a_ak = [jnp.where(strict_lower, scores[p][:P2, P2:], 0.0) for p in pairs]
    a_rb = [jnp.where(incl_lower, scores[p][P2:, :P2], 0.0) for p in pairs]
    a_rk = [jnp.where(incl_lower, scores[p][P2:, P2:], 0.0) for p in pairs]
    x0 = [jnp.concatenate([at[p], _dotb(a_ak[p], vb[p])], axis=1) for p in pairs]
    eye_f = jnp.where(eye, 1.0, 0.0)
    inv = [eye_f + a_ab[p] for p in pairs]
    pw = [_dotb(a_ab[p], a_ab[p]) for p in pairs]
    for _ in range(int(np.log2(C)) - 2):
        both = [_dotb(pw[p], jnp.concatenate([pw[p], inv[p]], axis=1)) for p in pairs]
        inv = [inv[p] + both[p][:, P2:] for p in pairs]
        pw = [both[p][:, :P2] for p in pairs]
    inv = [inv[p] + _dotb(pw[p], inv[p]) for p in pairs]
    x = [_dotb(inv[p], x0[p]) for p in pairs]
    rhs = [jnp.concatenate([x[p], jnp.concatenate([zeros_pp, vb[p]], axis=1)], axis=0) for p in pairs]
    gz = [_dotb(jnp.concatenate([a_rb[p], a_rk[p]], axis=1), rhs[p]) for p in pairs]
    mn = [_dotb(jnp.concatenate([bh[p], kh[p]], axis=0).T, rhs[p]) for p in pairs]
    yh = []
    for p in pairs:
        g = rt[p] + gz[p][:, :P2]
        m = mn[p][:, :P2] + jnp.where(eye, jnp.broadcast_to(jnp.exp(tot[p]), (P2, P2)), 0.0)
        yh.append(_dotb(jnp.concatenate([g, m], axis=0), h_ref[p]))
    ys = []
    for p in pairs:
        y2 = yh[p][:P2] + gz[p][:, P2:]
        h_ref[p] = yh[p][P2:] + mn[p][:, P2:]
        ys.append(y2[:C] + y2[C:])
    mean = _split_dots(ys, head_ones)
    yc = [ys[p] - mean[p] * (1.0 / RWKV_HEAD_DIM) for p in pairs]
    var = _split_dots([yc[p] * yc[p] for p in pairs], head_ones)
    rk = _split_dots([r_p[p] * k_p[p] * r_k[:, sl[p]] for p in pairs], head_ones)
    for p in pairs:
        yn = yc[p] * lax.rsqrt(var[p] * (1.0 / RWKV_HEAD_DIM) + RWKV_GN_EPS) * lnx_g[:, sl[p]] + lnx_b[:, sl[p]]
        o_ref[:, sl[p]] = ((yn + rk[p] * v_p[p]) * gate_all[:, sl[p]]).astype(o_ref.dtype)


def rwkv_core(p, v_first, vres, mu, w0, w2, a0, a2, g2, k_k, k_a, r_k, lnx_g, lnx_b, B, S):
    C = WKV_CHUNK
    n = S // C
    has_vres = vres is not None
    r_mu, k_mu, v_mu, wd_mu, ad_mu, gd_mu = jnp.split(mu, RWKV_SPLITS)
    mu3 = jnp.stack([r_mu, k_mu, v_mu])
    mu_wa = jnp.concatenate([wd_mu, ad_mu]).reshape(1, RWKV_WA_PAD)
    mu_g = jnp.pad(gd_mu, (0, RWKV_G_PAD - RWKV_LORA_G)).reshape(1, RWKV_G_PAD)
    w2p = jnp.pad(w2, ((0, RWKV_WA_PAD - RWKV_LORA_W), (0, 0))).astype(bf16)
    a2p = jnp.pad(a2, ((RWKV_LORA_W, 0), (0, 0))).astype(bf16)
    g2p = jnp.pad(g2, ((0, RWKV_G_PAD - RWKV_LORA_G), (0, 0))).astype(bf16)
    vecs = [w0, a0, k_k, k_a, r_k.reshape(MIX_B), lnx_g, lnx_b]
    if has_vres:
        vecs.append(vres[0])
    vec = jnp.stack(vecs)
    nv = vec.shape[0]
    row = lambda bi, ci: bi * n + ci
    const = lambda bi, ci: (0, 0)
    tile = lambda w, j: pl.BlockSpec((C, w), lambda bi, ci: (row(bi, ci), j))
    in_specs = [tile(MIX_B, EV_R0 // MIX_B), tile(MIX_B, EV_R0 // MIX_B + 1), tile(MIX_B, EV_R0 // MIX_B + 2),
                tile(RWKV_WA_PAD, EV_WA0 // RWKV_WA_PAD),
                tile(RWKV_G_PAD // 2, EV_G0 // (RWKV_G_PAD // 2)), tile(RWKV_G_PAD // 2, EV_G0 // (RWKV_G_PAD // 2) + 1)]
    args = [p, p, p, p, p, p]
    if has_vres:
        in_specs.append(tile(MIX_B, 0))
        args.append(v_first)
    in_specs += [pl.BlockSpec((3, MIX_B), const), pl.BlockSpec((1, RWKV_WA_PAD), const),
                 pl.BlockSpec((1, RWKV_G_PAD), const), pl.BlockSpec((RWKV_WA_PAD, MIX_B), const),
                 pl.BlockSpec((RWKV_WA_PAD, MIX_B), const), pl.BlockSpec((RWKV_G_PAD, MIX_B), const),
                 pl.BlockSpec((nv, MIX_B), const)]
    args += [mu3, mu_wa, mu_g, w2p, a2p, g2p, vec]
    out_specs = [tile(MIX_B, 0)]
    out_shape = [jax.ShapeDtypeStruct((B * S, MIX_B), bf16)]
    if has_vres:
        v1p = jnp.pad(vres[1], ((0, 0), (0, RWKV_V_PAD - vres[1].shape[1]))).astype(bf16)
        v2p = jnp.pad(vres[2], ((0, RWKV_V_PAD - vres[2].shape[0]), (0, 0))).astype(bf16)
        in_specs += [pl.BlockSpec((MIX_B, RWKV_V_PAD), const), pl.BlockSpec((RWKV_V_PAD, MIX_B), const)]
        args += [v1p, v2p]
    else:
        out_specs.append(tile(MIX_B, 0))
        out_shape.append(jax.ShapeDtypeStruct((B * S, MIX_B), f32))
    outs = pl.pallas_call(
        functools.partial(_rwkv_kernel, has_vres),
        grid=(B, n),
        in_specs=in_specs,
        out_specs=out_specs,
        out_shape=out_shape,
        scratch_shapes=[pltpu.VMEM((WKV_PAIRS, WKV_PAIR, WKV_PAIR), f32),
                        pltpu.VMEM((3, 8, MIX_B), f32), pltpu.VMEM((1, 8, RWKV_WA_PAD), f32),
                        pltpu.VMEM((1, 8, RWKV_G_PAD), f32)],
        compiler_params=pltpu.CompilerParams(
            dimension_semantics=("parallel", "arbitrary"), vmem_limit_bytes=VMEM_LIMIT_BYTES),
        name="rwkv7_time_mix",
    )(*args)
    return (outs[0], v_first) if has_vres else (outs[0], outs[1])


RET_LOG_GAMMA = tuple(float(np.log1p(-np.exp2(-5.0 - h))) for h in range(RET_HEADS))
ROPE_HALF = RET_HEAD_DIM // 2


def _ret_kernel(q_ref, k_ref, v_ref, g_ref, pos_ref, inv_ref, gng_ref, gnb_ref, o_ref, s_ref):
    C = RET_CHUNK
    d = RET_HEAD_DIM

    @pl.when(pl.program_id(1) == 0)
    def _():
        s_ref[...] = jnp.zeros_like(s_ref)

    ang = pos_ref[...] * inv_ref[...]
    cos = jnp.cos(ang)
    sin = jnp.sin(ang)

    def rot(x):
        x1, x2 = x[:, :ROPE_HALF], x[:, ROPE_HALF:]
        return jnp.concatenate([x1 * cos - x2 * sin, x1 * sin + x2 * cos], axis=-1)

    ri = lax.broadcasted_iota(jnp.int32, (C, C), 0)
    ci = lax.broadcasted_iota(jnp.int32, (C, C), 1)
    diff = (ri - ci).astype(f32)
    row = lax.broadcasted_iota(jnp.int32, (C, d), 0).astype(f32)
    nt = (((1,), (1,)), ((), ()))

    heads = range(RET_HEADS)
    hsl = [slice(h * d, (h + 1) * d) for h in heads]
    qs = [rot(q_ref[:, hsl[h]]) for h in heads]
    kr = [rot(k_ref[:, hsl[h]]) * (d ** -0.5) for h in heads]
    vb = [v_ref[:, hsl[h]].astype(bf16) for h in heads]
    sc = [lax.dot_general(qs[h].astype(bf16), kr[h].astype(bf16), nt, preferred_element_type=f32)
          * jnp.where(diff >= 0, jnp.exp(RET_LOG_GAMMA[h] * jnp.maximum(diff, 0.0)), 0.0) for h in heads]
    st = [s_ref[h] for h in heads]
    ys = [jnp.dot(sc[h].astype(bf16), vb[h], preferred_element_type=f32) for h in heads]
    ys = [ys[h] + jnp.dot((qs[h] * jnp.exp(RET_LOG_GAMMA[h] * (row + 1.0))).astype(bf16), st[h].astype(bf16),
                          preferred_element_type=f32) for h in heads]
    for h in heads:
        kw = (kr[h] * jnp.exp(RET_LOG_GAMMA[h] * (C - 1.0 - row))).astype(bf16)
        s_ref[h] = st[h] * float(np.exp(RET_LOG_GAMMA[h] * C)) + jnp.dot(kw.T, vb[h], preferred_element_type=f32)
    for h in heads:
        hs = hsl[h]
        y = ys[h]
        mu = jnp.mean(y, axis=-1, keepdims=True)
        yc = y - mu
        var = jnp.mean(yc * yc, axis=-1, keepdims=True)
        yn = yc * lax.rsqrt(var + LN_EPS) * gng_ref[:, hs] + gnb_ref[:, hs]
        g = g_ref[:, hs]
        o_ref[:, hs] = (g * jax.nn.sigmoid(g) * yn).astype(o_ref.dtype)


def retention_core(p, positions, gn_g, gn_b, B, S):
    C = RET_CHUNK
    n = S // C
    inv = (ROPE_BASE ** (-jnp.arange(ROPE_HALF, dtype=f32) / ROPE_HALF)).reshape(1, ROPE_HALF)
    pos = positions.astype(f32).reshape(B * S, 1)
    row = lambda bi, ci: bi * n + ci
    const = lambda bi, ci: (0, 0)
    col = lambda j: pl.BlockSpec((C, MIX_A), lambda bi, ci: (row(bi, ci), j))
    return pl.pallas_call(
        _ret_kernel,
        grid=(B, n),
        in_specs=[col(0), col(1), col(2), col(3),
                  pl.BlockSpec((C, 1), lambda bi, ci: (row(bi, ci), 0)),
                  pl.BlockSpec((1, ROPE_HALF), const),
                  pl.BlockSpec((1, MIX_A), const),
                  pl.BlockSpec((1, MIX_A), const)],
        out_specs=pl.BlockSpec((C, MIX_A), lambda bi, ci: (row(bi, ci), 0)),
        out_shape=jax.ShapeDtypeStruct((B * S, MIX_A), bf16),
        scratch_shapes=[pltpu.VMEM((RET_HEADS, RET_HEAD_DIM, RET_HEAD_DIM), f32)],
        compiler_params=pltpu.CompilerParams(
            dimension_semantics=("parallel", "arbitrary"), vmem_limit_bytes=VMEM_LIMIT_BYTES),
        name="retention_chunked",
    )(p, p, p, p, pos, inv, gn_g.reshape(1, MIX_A), gn_b.reshape(1, MIX_A))


GLA_LEVELS = int(np.log2(GLA_CHUNK))
GLA_DECAY_ROWS = (2 + GLA_LEVELS) * GLA_CHUNK
GLA_AD_PAD = 128
ODD_PROJ_PAD = 2 * GLA_DK + 2 * GLA_DV + GLA_AD_PAD
ODD_TN = ODD_PROJ_PAD // 7


def _gla_decay_operator():
    C = GLA_CHUNK
    i = np.arange(C)[:, None]
    j = np.arange(C)[None, :]
    blocks = [(j <= i), (j > i)]
    for lvl in range(GLA_LEVELS):
        s = C >> (lvl + 1)
        mid = (i // (2 * s)) * (2 * s) + s
        right = i >= mid
        blocks.append(np.where(right, (j >= mid) & (j <= i), (j > i) & (j < mid)))
    return np.concatenate(blocks, 0).astype(np.float32)


def _gla_kernel(q_ref, k_ref, v_ref, r_ref, ad_ref, a2_ref, ab_ref, g_ref, lop_ref, o_ref, s_ref):
    C = GLA_CHUNK

    @pl.when(pl.program_id(1) == 0)
    def _():
        s_ref[...] = jnp.zeros_like(s_ref)

    x = jnp.dot(ad_ref[...].astype(bf16), a2_ref[...], preferred_element_type=f32) + ab_ref[...]
    la = (jnp.minimum(x, 0.0) - jnp.log1p(jnp.exp(-jnp.abs(x)))) * (1.0 / GLA_GATE_NORM)
    hi = la.astype(bf16)
    r1 = la - hi.astype(f32)
    mid = r1.astype(bf16)
    lo = (r1 - mid.astype(f32)).astype(bf16)
    lop = lop_ref[...]
    dec = (jnp.dot(lop, hi, preferred_element_type=f32) + jnp.dot(lop, mid, preferred_element_type=f32)
           + jnp.dot(lop, lo, preferred_element_type=f32))

    ri = lax.broadcasted_iota(jnp.int32, (C, C), 0)
    ci = lax.broadcasted_iota(jnp.int32, (C, C), 1)
    level_masks = []
    for lvl in range(GLA_LEVELS):
        s = C >> (lvl + 1)
        same_block = (ri // (2 * s)) == (ci // (2 * s))
        level_masks.append(same_block & ((ri % (2 * s)) >= s) & ((ci % (2 * s)) < s))
    eye = ri == ci
    nt = (((1,), (1,)), ((), ()))

    heads = range(GLA_HEADS)
    ks = [slice(h * GLA_HEAD_DK, (h + 1) * GLA_HEAD_DK) for h in heads]
    vs = [slice(h * GLA_HEAD_DV, (h + 1) * GLA_HEAD_DV) for h in heads]
    q = [q_ref[:, ks[h]] * (GLA_HEAD_DK ** -0.5) for h in heads]
    k = [k_ref[:, ks[h]] for h in heads]
    v = [v_ref[:, vs[h]].astype(bf16) for h in heads]
    e = [jnp.exp(dec[:, ks[h]]) for h in heads]
    state = [s_ref[h] for h in heads]
    o = [lax.dot_general((q[h] * e[h][:C]).astype(bf16), state[h].astype(bf16), nt, preferred_element_type=f32)
         for h in heads]
    attn = [jnp.where(eye, lax.dot_general(q[h].astype(bf16), k[h].astype(bf16), nt, preferred_element_type=f32), 0.0)
            for h in heads]
    for lvl in range(GLA_LEVELS):
        rows = slice((2 + lvl) * C, (3 + lvl) * C)
        sc = [lax.dot_general((q[h] * e[h][rows]).astype(bf16), (k[h] * e[h][rows]).astype(bf16), nt,
                              preferred_element_type=f32) for h in heads]
        attn = [attn[h] + jnp.where(level_masks[lvl], sc[h], 0.0) for h in heads]
    o = [o[h] + jnp.dot(attn[h].astype(bf16), v[h], preferred_element_type=f32) for h in heads]
    for h in heads:
        kd = (k[h] * e[h][C:2 * C]).astype(bf16)
        s_ref[h] = state[h] * e[h][C - 1:C] + jnp.dot(v[h].T, kd, preferred_element_type=f32)
    for h in heads:
        on = o[h] * lax.rsqrt(jnp.mean(o[h] * o[h], axis=-1, keepdims=True) + LN_EPS) * g_ref[:, vs[h]]
        rr = r_ref[:, vs[h]]
        o_ref[:, vs[h]] = (rr * jax.nn.sigmoid(rr) * on).astype(o_ref.dtype)


def gla_core(p, a2, ab, norm_g, B, S):
    C = GLA_CHUNK
    n = S // C
    a2p = jnp.pad(a2, ((0, GLA_AD_PAD - GLA_LORA), (0, 0))).astype(bf16)
    lop = jnp.asarray(_gla_decay_operator(), dtype=bf16)
    row = lambda bi, ci: bi * n + ci
    const = lambda bi, ci: (0, 0)
    return pl.pallas_call(
        _gla_kernel,
        grid=(B, n),
        in_specs=[pl.BlockSpec((C, GLA_DK), lambda bi, ci: (row(bi, ci), 0)),
                  pl.BlockSpec((C, GLA_DK), lambda bi, ci: (row(bi, ci), 1)),
                  pl.BlockSpec((C, GLA_DV), lambda bi, ci: (row(bi, ci), 1)),
                  pl.BlockSpec((C, GLA_DV), lambda bi, ci: (row(bi, ci), 2)),
                  pl.BlockSpec((C, GLA_AD_PAD), lambda bi, ci: (row(bi, ci), (2 * GLA_DK + 2 * GLA_DV) // GLA_AD_PAD)),
                  pl.BlockSpec((GLA_AD_PAD, GLA_DK), const),
                  pl.BlockSpec((1, GLA_DK), const),
                  pl.BlockSpec((1, GLA_DV), const),
                  pl.BlockSpec((GLA_DECAY_ROWS, C), const)],
        out_specs=pl.BlockSpec((C, GLA_DV), lambda bi, ci: (row(bi, ci), 0)),
        out_shape=jax.ShapeDtypeStruct((B * S, GLA_DV), bf16),
        scratch_shapes=[pltpu.VMEM((GLA_HEADS, GLA_HEAD_DV, GLA_HEAD_DK), f32)],
        compiler_params=pltpu.CompilerParams(
            dimension_semantics=("parallel", "arbitrary"), vmem_limit_bytes=VMEM_LIMIT_BYTES),
        name="gla_chunked",
    )(p, p, p, p, p, a2p, ab.reshape(1, GLA_DK), norm_g.reshape(1, GLA_DV), lop)


PROJ_TM = 256
ROUTER_PAD = 128


def _proj_norm_kernel(n_in, *refs):
    a_refs = refs[:n_in]
    w_ref, x_ref, mod_ref, g_ref, b_ref, wr_ref, br_ref, xo_ref, ho_ref, lo_ref = refs[n_in:]
    acc = None
    k0 = 0
    for a_ref in a_refs:
        kw = a_ref.shape[1]
        part = jnp.dot(a_ref[...], w_ref[k0:k0 + kw, :], preferred_element_type=f32)
        acc = part if acc is None else acc + part
        k0 += kw
    gt, sc, sh = mod_ref[0, 0:1, :], mod_ref[0, 1:2, :], mod_ref[0, 2:3, :]
    z = DEEPNORM_ALPHA * x_ref[...] + (1.0 + gt) * acc
    mu = jnp.mean(z, axis=-1, keepdims=True)
    zc = z - mu
    var = jnp.mean(zc * zc, axis=-1, keepdims=True)
    xn = zc * lax.rsqrt(var + LN_EPS) * g_ref[...] + b_ref[...]
    xo_ref[...] = xn
    hf = xn * (1.0 + sc) + sh
    ho_ref[...] = hf
    lo_ref[...] = jnp.dot(hf.astype(bf16), wr_ref[...], preferred_element_type=f32) + br_ref[...]


def proj_norm(acts, wo, x, mod, ln_g, ln_b, w_router, b_router, S):
    T, D = x.shape
    tiles_per_batch = S // PROJ_TM
    const = lambda i: (0, 0)
    rowblk = lambda w: pl.BlockSpec((PROJ_TM, w), lambda i: (i, 0))
    in_specs = [rowblk(a.shape[1]) for a in acts]
    in_specs += [pl.BlockSpec(wo.shape, const), rowblk(D),
                 pl.BlockSpec((1, 3, D), lambda i: (i // tiles_per_batch, 0, 0)),
                 pl.BlockSpec((1, D), const), pl.BlockSpec((1, D), const),
                 pl.BlockSpec((D, ROUTER_PAD), const), pl.BlockSpec((1, ROUTER_PAD), const)]
    return pl.pallas_call(
        functools.partial(_proj_norm_kernel, len(acts)),
        grid=(T // PROJ_TM,),
        in_specs=in_specs,
        out_specs=[rowblk(D), rowblk(D), rowblk(ROUTER_PAD)],
        out_shape=[jax.ShapeDtypeStruct((T, D), f32), jax.ShapeDtypeStruct((T, D), f32),
                   jax.ShapeDtypeStruct((T, ROUTER_PAD), f32)],
        compiler_params=pltpu.CompilerParams(
            dimension_semantics=("parallel",), vmem_limit_bytes=VMEM_LIMIT_BYTES),
        name="proj_norm",
    )(*acts, wo.astype(bf16), x, mod, ln_g.reshape(1, D), ln_b.reshape(1, D), w_router, b_router)


MOE_ROWS = 256
MOE_VMEM_LIMIT_BYTES = 56 * 1024 * 1024
COMBINE_ROWS = 256
SUBLANES = 8
GATHER_DMA_PRIORITY = 1


def _moe_ffn_kernel(layer, tok_ref, be_ref, nused_ref, x_hbm, w1_hbm, w3_hbm, w2_hbm, y_ref,
                    xbuf, w1buf, w3buf, w2buf, sem, wsem, run_ref):
    i = pl.program_id(0)
    nblk = pl.num_programs(0)
    n_used = nused_ref[0]
    slot = lax.rem(i, 2)
    expert = be_ref[i]
    first = (i < n_used) & ((i == 0) | (expert != be_ref[jnp.maximum(i - 1, 0)]))

    n_groups = MOE_ROWS // SUBLANES

    def more_groups(blk):
        return lambda g: (g < n_groups) & (tok_ref[blk * MOE_ROWS + jnp.minimum(g, n_groups - 1) * SUBLANES] >= 0)

    def start_gather(blk, s):
        def body(g):
            for u in range(SUBLANES):
                t = jnp.maximum(tok_ref[blk * MOE_ROWS + g * SUBLANES + u], 0)
                pltpu.make_async_copy(x_hbm.at[lax.shift_right_logical(t, 3), pl.ds(jnp.bitwise_and(t, SUBLANES - 1), 1)],
                                      xbuf.at[s, g, pl.ds(u, 1)], sem.at[s]).start(priority=GATHER_DMA_PRIORITY)
            return g + 1
        lax.while_loop(more_groups(blk), body, 0)

    def wait_gather(blk, s):
        def body(g):
            pltpu.make_async_copy(x_hbm.at[pl.ds(0, 1)], xbuf.at[s, pl.ds(g, 1)], sem.at[s]).wait()
            return g + 1
        lax.while_loop(more_groups(blk), body, 0)

    def weight_copies(e, b):
        return [pltpu.make_async_copy(w_hbm.at[layer, e], wbuf.at[b], wsem.at[b])
                for w_hbm, wbuf in ((w1_hbm, w1buf), (w3_hbm, w3buf), (w2_hbm, w2buf))]

    @pl.when((i == 0) & (n_used > 0))
    def _():
        run_ref[0] = 0
        xbuf[...] = jnp.zeros_like(xbuf)
        for cp in weight_copies(expert, 0):
            cp.start()
        start_gather(0, 0)

    @pl.when(i + 1 < n_used)
    def _():
        start_gather(i + 1, 1 - slot)

    @pl.when(first)
    def _():
        @pl.when(i > 0)
        def _():
            run_ref[0] = run_ref[0] + 1
        buf = jnp.bitwise_and(run_ref[0], 1)
        nxt = lax.while_loop(lambda j: (j < n_used) & (be_ref[jnp.minimum(j, nblk - 1)] == expert),
                             lambda j: j + 1, i + 1)

        @pl.when(nxt < n_used)
        def _():
            for cp in weight_copies(be_ref[jnp.minimum(nxt, nblk - 1)], 1 - buf):
                cp.start()
        for cp in weight_copies(expert, buf):
            cp.wait()

    @pl.when(i < n_used)
    def _():
        buf = jnp.bitwise_and(run_ref[0], 1)
        wait_gather(i, slot)
        xb = xbuf[slot].reshape(MOE_ROWS, xbuf.shape[-1]).astype(bf16)
        a = jnp.dot(xb, w1buf[buf].astype(bf16), preferred_element_type=f32)
        b = jnp.dot(xb, w3buf[buf].astype(bf16), preferred_element_type=f32)
        hid = (a * jax.nn.sigmoid(a) * b).astype(bf16)
        y_ref[...] = jnp.dot(hid, w2buf[buf].astype(bf16), preferred_element_type=f32)

    @pl.when(i >= n_used)
    def _():
        y_ref[...] = jnp.zeros_like(y_ref)


def moe_ffn(x, buf_tok, blk_expert, n_used, w1, w3, w2, layer):
    D = x.shape[1]
    nblk = blk_expert.shape[0]
    any_spec = pl.BlockSpec(memory_space=pl.ANY)
    return pl.pallas_call(
        functools.partial(_moe_ffn_kernel, layer),
        grid_spec=pltpu.PrefetchScalarGridSpec(
            num_scalar_prefetch=3,
            grid=(nblk,),
            in_specs=[any_spec, any_spec, any_spec, any_spec],
            out_specs=pl.BlockSpec((MOE_ROWS, D), lambda i, tok, be, nu: (i, 0)),
            scratch_shapes=[pltpu.VMEM((2, MOE_ROWS // SUBLANES, SUBLANES, D), f32),
                            pltpu.VMEM((2, D, EXPERT_FF), f32), pltpu.VMEM((2, D, EXPERT_FF), f32),
                            pltpu.VMEM((2, EXPERT_FF, D), f32),
                            pltpu.SemaphoreType.DMA((2,)), pltpu.SemaphoreType.DMA((2,)),
                            pltpu.SMEM((1,), jnp.int32)]),
        out_shape=jax.ShapeDtypeStruct((nblk * MOE_ROWS, D), f32),
        compiler_params=pltpu.CompilerParams(
            dimension_semantics=("arbitrary",), vmem_limit_bytes=MOE_VMEM_LIMIT_BYTES),
        name="moe_expert_ffn",
    )(buf_tok, blk_expert, n_used, x.reshape(x.shape[0] // SUBLANES, SUBLANES, D), w1, w3, w2)


def _moe_combine_kernel(emit_h, s0_ref, s1_ref, y_hbm, gate_ref, x_ref, mod_ref, g_ref, b_ref, *rest):
    if emit_h:
        o_ref, h_ref, buf, sem = rest
    else:
        o_ref, buf, sem = rest
    i = pl.program_id(0)
    n = pl.num_programs(0)
    slot = lax.rem(i, 2)

    def start_gather(blk, s):
        def body(g, carry):
            base = pl.multiple_of(g * SUBLANES, SUBLANES)
            for u in range(SUBLANES):
                t = blk * COMBINE_ROWS + base + u
                pltpu.make_async_copy(y_hbm.at[pl.ds(s0_ref[t], 1)], buf.at[s, 0, pl.ds(base + u, 1)], sem.at[s]).start()
                pltpu.make_async_copy(y_hbm.at[pl.ds(s1_ref[t], 1)], buf.at[s, 1, pl.ds(base + u, 1)], sem.at[s]).start()
            return carry
        lax.fori_loop(0, COMBINE_ROWS // SUBLANES, body, 0)

    @pl.when(i == 0)
    def _():
        start_gather(0, 0)

    @pl.when(i + 1 < n)
    def _():
        start_gather(i + 1, 1 - slot)

    for c in range(MOE_TOPK):
        pltpu.make_async_copy(y_hbm.at[pl.ds(0, COMBINE_ROWS)], buf.at[slot, c], sem.at[slot]).wait()
    y = buf[slot, 0] * gate_ref[:, 0:1] + buf[slot, 1] * gate_ref[:, 1:2]
    z = DEEPNORM_ALPHA * x_ref[...] + (1.0 + mod_ref[0, 0:1, :]) * y
    mu = jnp.mean(z, axis=-1, keepdims=True)
    zc = z - mu
    var = jnp.mean(zc * zc, axis=-1, keepdims=True)
    xn = zc * lax.rsqrt(var + LN_EPS) * g_ref[...] + b_ref[...]
    o_ref[...] = xn
    if emit_h:
        h_ref[...] = (xn * (1.0 + mod_ref[0, 1:2, :]) + mod_ref[0, 2:3, :]).astype(h_ref.dtype)


def moe_combine_norm(y_rows, slot0, slot1, gate, x, mod, ln_g, ln_b, S, emit_h):
    T, D = x.shape
    tiles_per_batch = S // COMBINE_ROWS
    const = lambda i, s0, s1: (0, 0)
    rowblk = pl.BlockSpec((COMBINE_ROWS, D), lambda i, s0, s1: (i, 0))
    out_specs = [rowblk, rowblk] if emit_h else [rowblk]
    out_shape = [jax.ShapeDtypeStruct((T, D), f32)] + ([jax.ShapeDtypeStruct((T, D), bf16)] if emit_h else [])
    return pl.pallas_call(
        functools.partial(_moe_combine_kernel, emit_h),
        grid_spec=pltpu.PrefetchScalarGridSpec(
            num_scalar_prefetch=2,
            grid=(T // COMBINE_ROWS,),
            in_specs=[pl.BlockSpec(memory_space=pl.ANY),
                      pl.BlockSpec((COMBINE_ROWS, MOE_TOPK), lambda i, s0, s1: (i, 0)), rowblk,
                      pl.BlockSpec((1, 3, D), lambda i, s0, s1: (i // tiles_per_batch, 0, 0)),
                      pl.BlockSpec((1, D), const), pl.BlockSpec((1, D), const)],
            out_specs=out_specs,
            scratch_shapes=[pltpu.VMEM((2, MOE_TOPK, COMBINE_ROWS, D), f32), pltpu.SemaphoreType.DMA((2,))]),
        out_shape=out_shape,
        compiler_params=pltpu.CompilerParams(
            dimension_semantics=("arbitrary",), vmem_limit_bytes=VMEM_LIMIT_BYTES),
        name="moe_combine_norm",
    )(slot0, slot1, y_rows, gate, x, mod, ln_g.reshape(1, D), ln_b.reshape(1, D))


def router_params(wg, bg, we, be):
    D = wg.shape[0]
    pad = ROUTER_PAD - MOE_GROUPS - N_EXPERTS
    w = jnp.concatenate([wg, we, jnp.zeros((D, pad), wg.dtype)], 1).astype(bf16)
    b = jnp.concatenate([bg, be, jnp.zeros((pad,), bg.dtype)]).reshape(1, ROUTER_PAD)
    return w, b


def moe_route(logits):
    g_prob = jax.nn.softmax(logits[:, :MOE_GROUPS], -1)
    g_top = jnp.argmax(g_prob, -1).astype(jnp.int32)
    p_top = jnp.max(g_prob, -1)
    ids = jnp.arange(N_EXPERTS, dtype=jnp.int32)[None, :]
    e_sel = jnp.where(ids // MOE_PER_GROUP == g_top[:, None], logits[:, MOE_GROUPS:MOE_GROUPS + N_EXPERTS], -jnp.inf)
    i1 = jnp.argmax(e_sel, -1).astype(jnp.int32)
    v1 = jnp.max(e_sel, -1)
    rest = jnp.where(ids == i1[:, None], -jnp.inf, e_sel)
    i2 = jnp.argmax(rest, -1).astype(jnp.int32)
    v2 = jnp.max(rest, -1)
    gate = p_top[:, None] * jax.nn.softmax(jnp.stack([v1, v2], -1), -1)
    return gate, jnp.stack([i1, i2], -1)


def moe_layout(expert):
    T = expert.shape[0]
    A = T * MOE_TOPK
    e_flat = expert.reshape(A)
    onehot = (e_flat[:, None] == jnp.arange(N_EXPERTS, dtype=jnp.int32)[None, :]).astype(jnp.int32)
    rank = jnp.cumsum(onehot, axis=0)
    counts = rank[-1]
    padded = (counts + MOE_ROWS - 1) // MOE_ROWS * MOE_ROWS
    pad_end = jnp.cumsum(padded)
    pad_start = pad_end - padded
    slot = jnp.sum(onehot * (pad_start[None, :] + rank - 1), axis=1).astype(jnp.int32)
    tok_flat = jnp.arange(A, dtype=jnp.int32) // MOE_TOPK
    nblk = A // MOE_ROWS + N_EXPERTS
    buf_tok = jnp.full((nblk * MOE_ROWS,), -1, jnp.int32).at[slot].set(tok_flat)
    n_used = (pad_end[-1] // MOE_ROWS).astype(jnp.int32)
    blk = jnp.minimum(jnp.arange(nblk, dtype=jnp.int32), n_used - 1) * MOE_ROWS
    blk_expert = jnp.minimum(jnp.sum(blk[:, None] >= pad_end[None, :], axis=1), N_EXPERTS - 1).astype(jnp.int32)
    slot2 = slot.reshape(T, MOE_TOPK)
    return buf_tok, blk_expert, n_used.reshape(1), slot2[:, 0], slot2[:, 1]


def kernel(x, c, positions, ada_w, ada_b, ln_g, ln_b, ev_win, ev_wo, ret_gn_g, ret_gn_b, rw_mu, rw_w0, rw_w2, rw_a0, rw_a2, rw_g2, rw_kk, rw_ka, rw_rk, rw_lnx_g, rw_lnx_b, rw_v0, rw_v1, rw_v2, od_win, od_wo, gla_a2, gla_ab, gla_norm_g, moe_wg, moe_bg, moe_we, moe_be, moe_w1, moe_w3, moe_w2):
    B, S, D = x.shape
    T = B * S
    depth = ada_w.shape[0]
    mods = ada_mod(c, ada_w, ada_b).reshape(depth, B, 6, D)
    h = (x * (1.0 + mods[0, :, 1])[:, None, :] + mods[0, :, 0][:, None, :]).astype(bf16).reshape(T, D)
    x = x.reshape(T, D)
    ev_w = padded_bf16(ev_win, EVEN_COLS)
    od_w = padded_bf16(od_win, ODD_PROJ_PAD)
    v_first = None
    for layer in range(depth):
        j = layer // 2
        mixer_mod = jnp.stack([mods[layer, :, 2], mods[layer, :, 4], mods[layer, :, 3]], axis=1)
        if layer % 2 == 0:
            p = matmul(h, ev_w, j, tn=EVEN_TN)
            out_a = retention_core(p, positions, ret_gn_g[j], ret_gn_b[j], B, S)
            vres = None if j == 0 else (rw_v0[j - 1], rw_v1[j - 1], rw_v2[j - 1])
            out_b, v_first = rwkv_core(p, v_first, vres, rw_mu[j], rw_w0[j], rw_w2[j], rw_a0[j], rw_a2[j], rw_g2[j],
                                       rw_kk[j], rw_ka[j], rw_rk[j], rw_lnx_g[j], rw_lnx_b[j], B, S)
            acts, wo = [out_a, out_b], ev_wo[j]
        else:
            p = matmul(h, od_w, j, tn=ODD_TN)
            acts, wo = [gla_core(p, gla_a2[j], gla_ab[j], gla_norm_g[j], B, S)], od_wo[j]
        w_router, b_router = router_params(moe_wg[layer], moe_bg[layer], moe_we[layer], moe_be[layer])
        x, hf, logits = proj_norm(acts, wo, x, mixer_mod, ln_g[layer, 0], ln_b[layer, 0], w_router, b_router, S)
        gate, expert = moe_route(logits)
        buf_tok, blk_expert, n_used, slot0, slot1 = moe_layout(expert)
        y_rows = moe_ffn(hf, buf_tok, blk_expert, n_used, moe_w1, moe_w3, moe_w2, layer)
        last = layer == depth - 1
        nxt = layer if last else layer + 1
        moe_mod = jnp.stack([mods[layer, :, 5], mods[nxt, :, 1], mods[nxt, :, 0]], axis=1)
        outs = moe_combine_norm(y_rows, slot0, slot1, gate, x, moe_mod, ln_g[layer, 1], ln_b[layer, 1], S, not last)
        x = outs[0]
        if not last:
            h = outs[1]
    return x.reshape(B, S, D)
```
